```python
import math
import jax
import jax.numpy as jnp
from jax import lax
import numpy as np

D_MODEL = 4096
BATCH = 4
SEQ = 4096
DEPTH = 1

N_META = 16
CHUNK = 128
RET_HEADS = D_MODEL // 512
RET_DK = 256
RET_DV = 256
SB_HEADS = D_MODEL // 256
SB_DH = 128
RET_WIDTH = RET_HEADS * RET_DV
RET_QK_WIDTH = RET_HEADS * RET_DK
SB_WIDTH = SB_HEADS * SB_DH
MIX_WIDTH = RET_WIDTH + SB_WIDTH
IN_COLS = 2 * RET_QK_WIDTH + 2 * RET_WIDTH + 3 * SB_WIDTH
D_FF = ((8 * D_MODEL // 3 + 255) // 256) * 256
CONV_W = 3
EPS = 1e-6
ROPE_BASE = 10000.0

kernel_name = "hymba_retention_stickbreaking_convffn"


def rms_norm(x, w):
    xf = x.astype(jnp.float32)
    y = xf * lax.rsqrt(jnp.mean(xf * xf, axis=-1, keepdims=True) + EPS)
    return (y * w.astype(jnp.float32)).astype(x.dtype)


def head_rms_norm(y, w):
    b, t, h, d = y.shape
    yf = y.astype(jnp.float32)
    yf = yf * lax.rsqrt(jnp.mean(yf * yf, axis=-1, keepdims=True) + EPS)
    yf = yf * w.astype(jnp.float32).reshape(h, d)
    return yf.reshape(b, t, h * d)


def apply_rotary(x, pos):
    half = x.shape[-1] // 2
    inv = ROPE_BASE ** (-jnp.arange(half, dtype=jnp.float32) / half)
    ang = pos[:, None] * inv[None, :]
    cos = jnp.cos(ang)[None, :, None, :].astype(x.dtype)
    sin = jnp.sin(ang)[None, :, None, :].astype(x.dtype)
    x1, x2 = x[..., :half], x[..., half:]
    return jnp.concatenate([x1 * cos - x2 * sin, x1 * sin + x2 * cos], axis=-1)


def retention_chunkwise(q, k, v):
    b, tp, h, dk = q.shape
    dv = v.shape[-1]
    nc = tp // CHUNK
    log_gamma = jnp.log(1.0 - 2.0 ** (-5.0 - jnp.arange(h, dtype=jnp.float32)))
    idx = jnp.arange(CHUNK, dtype=jnp.float32)
    diff = idx[:, None] - idx[None, :]
    intra_decay = jnp.where(diff[None] >= 0, jnp.exp(jnp.maximum(diff, 0.0)[None] * log_gamma[:, None, None]), 0.0)
    q_decay = jnp.exp((idx[None, :] + 1.0) * log_gamma[:, None])
    k_decay = jnp.exp((CHUNK - 1.0 - idx[None, :]) * log_gamma[:, None])
    chunk_decay = jnp.exp(CHUNK * log_gamma)

    def to_chunks(a):
        return a.reshape(b, nc, CHUNK, h, a.shape[-1]).transpose(1, 0, 3, 2, 4)

    qc, kc, vc = to_chunks(q), to_chunks(k), to_chunks(v)

    def body(state, inp):
        qi, ki, vi = inp
        s = jnp.einsum('bhnd,bhmd->bhnm', qi, ki) * intra_decay[None]
        intra = jnp.einsum('bhnm,bhme->bhne', s, vi)
        cross = jnp.einsum('bhnd,bhde->bhne', qi, state) * q_decay[None, :, :, None]
        new_state = state * chunk_decay[None, :, None, None] + jnp.einsum(
            'bhmd,bhme->bhde', ki * k_decay[None, :, :, None], vi)
        return new_state, intra + cross

    state0 = jnp.zeros((b, h, dk, dv), jnp.float32)
    _, ys = lax.scan(body, state0, (qc, kc, vc))
    return ys.transpose(1, 0, 3, 2, 4).reshape(b, tp, h, dv)


def stick_breaking_attention(q, k, v, key_valid):
    b, tp, h, d = q.shape
    nb = tp // CHUNK
    scale = 1.0 / math.sqrt(d)
    qh = q.transpose(0, 2, 1, 3)
    kh = k.transpose(0, 2, 1, 3)
    vh = v.transpose(0, 2, 1, 3)
    qb = qh.reshape(b, h, nb, CHUNK, d).transpose(2, 0, 1, 3, 4)
    starts = jnp.arange(nb, dtype=jnp.int32) * CHUNK
    key_pos = jnp.arange(tp, dtype=jnp.int32)

    def block(args):
        qblk, start = args
        z = jnp.einsum('bhqd,bhkd->bhqk', qblk, kh).astype(jnp.float32) * scale
        qpos = start + jnp.arange(CHUNK, dtype=jnp.int32)
        mask = (key_pos[None, :] < qpos[:, None]) & key_valid[None, :]
        log_beta = jax.nn.log_sigmoid(z)
        log_keep = jnp.where(mask, jax.nn.log_sigmoid(-z), 0.0)
        rev = lax.cumsum(log_keep, axis=3, reverse=True)
        after = jnp.concatenate([rev[..., 1:], jnp.zeros_like(rev[..., :1])], axis=-1)
        w = jnp.where(mask, jnp.exp(log_beta + after), 0.0)
        return jnp.einsum('bhqk,bhkd->bhqd', w.astype(vh.dtype), vh)

    out = lax.map(block, (qb, starts))
    return out.transpose(1, 0, 3, 2, 4).reshape(b, tp, h, d)


def hybrid_mixer(h, w_in, ret_gn_w, sb_norm_w, w_out):
    b, t, _ = h.shape
    n_pad = CHUNK - N_META
    tp = t + n_pad
    proj = h @ w_in
    proj = jnp.pad(proj, ((0, 0), (n_pad, 0), (0, 0)))
    valid = jnp.arange(tp) >= n_pad
    vmask = valid.astype(proj.dtype)[None, :, None, None]
    pos = (jnp.arange(tp) - n_pad).astype(jnp.float32)

    o = 0
    rq = proj[..., o:o + RET_QK_WIDTH]; o += RET_QK_WIDTH
    rk = proj[..., o:o + RET_QK_WIDTH]; o += RET_QK_WIDTH
    rv = proj[..., o:o + RET_WIDTH]; o += RET_WIDTH
    rg = proj[..., o:o + RET_WIDTH]; o += RET_WIDTH
    sq = proj[..., o:o + SB_WIDTH]; o += SB_WIDTH
    sk = proj[..., o:o + SB_WIDTH]; o += SB_WIDTH
    sv = proj[..., o:o + SB_WIDTH]

    rq = apply_rotary(rq.reshape(b, tp, RET_HEADS, RET_DK), pos) * (RET_DK ** -0.5)
    rk = apply_rotary(rk.reshape(b, tp, RET_HEADS, RET_DK), pos) * vmask
    rv = rv.reshape(b, tp, RET_HEADS, RET_DV) * vmask
    ry = retention_chunkwise(rq, rk, rv)
    ret_out = jax.nn.silu(rg.astype(jnp.float32)) * head_rms_norm(ry, ret_gn_w)

    sq = sq.reshape(b, tp, SB_HEADS, SB_DH)
    sk = sk.reshape(b, tp, SB_HEADS, SB_DH) * vmask
    sv = sv.reshape(b, tp, SB_HEADS, SB_DH) * vmask
    sy = stick_breaking_attention(sq, sk, sv, valid)
    sb_out = head_rms_norm(sy, sb_norm_w)

    mixed = jnp.concatenate([ret_out, sb_out], axis=-1)[:, n_pad:].astype(h.dtype)
    return mixed @ w_out


def conv_ffn(h, w_up, conv_w, conv_b, w_down):
    t = h.shape[1]
    gu = h @ w_up
    gate, up = gu[..., :D_FF], gu[..., D_FF:]
    gp = jnp.pad(gate, ((0, 0), (CONV_W - 1, 0), (0, 0)))
    conv = conv_b[None, None, :]
    for i in range(CONV_W):
        conv = conv + gp[:, i:i + t] * conv_w[i][None, None, :]
    return (jax.nn.silu(conv) * up) @ w_down


def setup_inputs(seed: int = 0) -> dict:
    key = jax.random.key(seed)
    ks = jax.random.split(key, 16)
    f32 = jnp.float32

    def gain(k, n):
        return 1.0 + 0.01 * jax.random.normal(k, (DEPTH, n), f32)

    return {
        "x": jax.random.normal(ks[0], (BATCH, SEQ, D_MODEL), f32),
        "meta_tokens": jax.random.normal(ks[1], (N_META, D_MODEL), f32),
        "attn_pre_norm_w": gain(ks[2], D_MODEL),
        "w_in": jax.random.normal(ks[3], (DEPTH, D_MODEL, IN_COLS), f32) * D_MODEL ** -0.5,
        "ret_gn_w": gain(ks[4], RET_WIDTH),
        "sb_norm_w": gain(ks[5], SB_WIDTH),
        "w_out": jax.random.normal(ks[6], (DEPTH, MIX_WIDTH, D_MODEL), f32) * MIX_WIDTH ** -0.5,
        "attn_post_norm_w": gain(ks[7], D_MODEL),
        "ffn_pre_norm_w": gain(ks[8], D_MODEL),
        "w_up": jax.random.normal(ks[9], (DEPTH, D_MODEL, 2 * D_FF), f32) * D_MODEL ** -0.5,
        "conv_w": jax.random.normal(ks[10], (DEPTH, CONV_W, D_FF), f32) * CONV_W ** -0.5,
        "conv_b": 0.01 * jax.random.normal(ks[11], (DEPTH, D_FF), f32),
        "w_down": jax.random.normal(ks[12], (DEPTH, D_FF, D_MODEL), f32) * D_FF ** -0.5,
        "ffn_post_norm_w": gain(ks[13], D_MODEL),
    }


def reference(x, meta_tokens, attn_pre_norm_w, w_in, ret_gn_w, sb_norm_w, w_out,
              attn_post_norm_w, ffn_pre_norm_w, w_up, conv_w, conv_b, w_down, ffn_post_norm_w):
    b = x.shape[0]
    meta = jnp.broadcast_to(meta_tokens[None].astype(x.dtype), (b, N_META, x.shape[-1]))
    h = jnp.concatenate([meta, x], axis=1)
    for l in range(DEPTH):
        a = hybrid_mixer(rms_norm(h, attn_pre_norm_w[l]), w_in[l], ret_gn_w[l], sb_norm_w[l], w_out[l])
        h = h + rms_norm(a, attn_post_norm_w[l])
        f = conv_ffn(rms_norm(h, ffn_pre_norm_w[l]), w_up[l], conv_w[l], conv_b[l], w_down[l])
        h = h + rms_norm(f, ffn_post_norm_w[l])
    return h[:, N_META:]
```

```python
import functools
import math

import numpy as np
import jax
import jax.numpy as jnp
from jax import lax
from jax.experimental import pallas as pl
from jax.experimental.pallas import tpu as pltpu

N_META = 16
CHUNK = 128
N_PAD = CHUNK - N_META
RET_HEADS = 8
RET_DK = 256
RET_DV = 256
SB_HEADS = 16
SB_DH = 128
RET_WIDTH = RET_HEADS * RET_DV
RET_QK_WIDTH = RET_HEADS * RET_DK
SB_WIDTH = SB_HEADS * SB_DH
CONV_W = 3
EPS = 1e-6
ROPE_BASE = 10000.0

BF16_SUBLANES = 16
VMEM_LIMIT_BYTES = 56 * 1024 * 1024

F32 = jnp.float32
BF16 = jnp.bfloat16

_NT = (((1,), (1,)), ((), ()))
_TN = (((0,), (0,)), ((), ()))


def _params(*semantics):
    return pltpu.CompilerParams(dimension_semantics=semantics,
                                vmem_limit_bytes=VMEM_LIMIT_BYTES)


def _largest_divisor(n, target, multiple):
    best = None
    for d in range(multiple, min(n, target) + 1, multiple):
        if n % d == 0:
            best = d
    assert best is not None, (n, target, multiple)
    return best


def _rms(v, w):
    ms = jnp.mean(v * v, axis=-1, keepdims=True)
    return v * lax.rsqrt(ms + EPS) * w


def _prep_kernel(x_ref, meta_ref, w_ref, h_ref, hn_ref):
    r = pl.program_id(1)
    w = w_ref[...]
    d = h_ref.shape[-1]

    @pl.when(r == 0)
    def _():
        m = meta_ref[...]
        h_ref[0, :N_PAD, :] = jnp.zeros((N_PAD, d), F32)
        h_ref[0, N_PAD:, :] = m
        hn_ref[0, :N_PAD, :] = jnp.zeros((N_PAD, d), BF16)
        hn_ref[0, N_PAD:, :] = _rms(m, w).astype(BF16)

    @pl.when(r > 0)
    def _():
        v = x_ref[0]
        h_ref[0] = v
        hn_ref[0] = _rms(v, w).astype(BF16)


def _prep(x, meta, w):
    b, seq, d = x.shape
    nblk = seq // CHUNK + 1
    tp = nblk * CHUNK
    return pl.pallas_call(
        _prep_kernel,
        grid=(b, nblk),
        in_specs=[
            pl.BlockSpec((1, CHUNK, d), lambda i, r: (i, jnp.maximum(r - 1, 0), 0)),
            pl.BlockSpec((N_META, d), lambda i, r: (0, 0)),
            pl.BlockSpec((1, d), lambda i, r: (0, 0)),
        ],
        out_specs=[
            pl.BlockSpec((1, CHUNK, d), lambda i, r: (i, r, 0)),
            pl.BlockSpec((1, CHUNK, d), lambda i, r: (i, r, 0)),
        ],
        out_shape=[jax.ShapeDtypeStruct((b, tp, d), F32),
                   jax.ShapeDtypeStruct((b, tp, d), BF16)],
        compiler_params=_params("parallel", "arbitrary"),
        name="prep",
    )(x, meta, w)


def _proj_rot_kernel(x_ref, w_ref, cos_ref, sin_ref, o_ref, *, n_q_blocks):
    j = pl.program_id(1)
    acc = jnp.dot(x_ref[...], w_ref[...], preferred_element_type=F32)
    scale = jnp.where(j < n_q_blocks, RET_DK ** -0.5, 1.0).astype(F32)
    cos = cos_ref[...]
    sin = sin_ref[...]
    half = RET_DK // 2
    for h in range(o_ref.shape[1] // RET_DK):
        lo = h * RET_DK
        x1 = acc[:, lo:lo + half]
        x2 = acc[:, lo + half:lo + RET_DK]
        o_ref[:, lo:lo + half] = ((x1 * cos - x2 * sin) * scale).astype(BF16)
        o_ref[:, lo + half:lo + RET_DK] = ((x1 * sin + x2 * cos) * scale).astype(BF16)


def _proj_plain_kernel(x_ref, w_ref, o_ref):
    o_ref[...] = jnp.dot(x_ref[...], w_ref[...],
                         preferred_element_type=F32).astype(o_ref.dtype)


def _in_proj(hn, w_in, cos, sin, tp):
    m, d = hn.shape
    bm = _largest_divisor(tp, 1056, BF16_SUBLANES)
    bn = 1024
    pos_blocks = tp // bm
    n_rot = 2 * RET_QK_WIDTH
    n_rest = w_in.shape[1] - n_rot
    qk = pl.pallas_call(
        functools.partial(_proj_rot_kernel, n_q_blocks=RET_QK_WIDTH // bn),
        grid=(m // bm, n_rot // bn),
        in_specs=[
            pl.BlockSpec((bm, d), lambda i, j: (i, 0)),
            pl.BlockSpec((d, bn), lambda i, j: (0, j)),
            pl.BlockSpec((bm, RET_DK // 2), lambda i, j: (i % pos_blocks, 0)),
            pl.BlockSpec((bm, RET_DK // 2), lambda i, j: (i % pos_blocks, 0)),
        ],
        out_specs=pl.BlockSpec((bm, bn), lambda i, j: (i, j)),
        out_shape=jax.ShapeDtypeStruct((m, n_rot), BF16),
        compiler_params=_params("parallel", "arbitrary"),
        name="in_proj_rot",
    )(hn, w_in, cos, sin)
    rot_blocks = n_rot // bn
    rest = pl.pallas_call(
        _proj_plain_kernel,
        grid=(m // bm, n_rest // bn),
        in_specs=[
            pl.BlockSpec((bm, d), lambda i, j: (i, 0)),
            pl.BlockSpec((d, bn), lambda i, j: (0, j + rot_blocks)),
        ],
        out_specs=pl.BlockSpec((bm, bn), lambda i, j: (i, j)),
        out_shape=jax.ShapeDtypeStruct((m, n_rest), BF16),
        compiler_params=_params("parallel", "arbitrary"),
        name="in_proj_rest",
    )(hn, w_in)
    return qk, rest


def _ret_log_gamma():
    return np.log(1.0 - 2.0 ** (-5.0 - np.arange(RET_HEADS, dtype=np.float64)))


def _ret_kernel(q_ref, k_ref, v_ref, g_ref, intra_ref, qd_ref, kd_ref, gn_ref,
                o_ref, state_ref):
    c = pl.program_id(1)

    @pl.when(c == 0)
    def _():
        state_ref[...] = jnp.zeros(state_ref.shape, F32)

    chunk_decay = np.exp(CHUNK * _ret_log_gamma()).astype(np.float32)
    for h in range(RET_HEADS):
        sl = slice(h * RET_DK, (h + 1) * RET_DK)
        q = q_ref[:, sl]
        k = k_ref[:, sl]
        v = v_ref[:, sl]
        s = lax.dot_general(q, k, _NT, preferred_element_type=F32) * intra_ref[h]
        intra = jnp.dot(s.astype(BF16), v, preferred_element_type=F32)
        st = state_ref[h]
        qd = qd_ref[h]
        kd = kd_ref[h]
        cross = jnp.dot(q, st.astype(BF16), preferred_element_type=F32)
        cross = cross * jnp.concatenate([qd, qd], axis=1)
        kdk = (k.astype(F32) * jnp.concatenate([kd, kd], axis=1)).astype(BF16)
        upd = lax.dot_general(kdk, v, _TN, preferred_element_type=F32)
        state_ref[h] = st * float(chunk_decay[h]) + upd
        y = intra + cross
        yn = _rms(y, gn_ref[:, sl])
        g = g_ref[:, sl].astype(F32)
        o_ref[:, sl] = (g * jax.nn.sigmoid(g) * yn).astype(BF16)


def _retention(qk, rest, gn_w, b, tp):
    m = qk.shape[0]
    nblk = tp // CHUNK
    lg = jnp.log(1.0 - 2.0 ** (-5.0 - jnp.arange(RET_HEADS, dtype=F32)))
    idx = jnp.arange(CHUNK, dtype=F32)
    diff = idx[:, None] - idx[None, :]
    intra = jnp.where(diff[None] >= 0,
                      jnp.exp(jnp.maximum(diff, 0.0)[None] * lg[:, None, None]), 0.0)
    qd = jnp.exp((idx[None, :] + 1.0) * lg[:, None])
    kd = jnp.exp((CHUNK - 1.0 - idx[None, :]) * lg[:, None])
    qd = jnp.broadcast_to(qd[:, :, None], (RET_HEADS, CHUNK, CHUNK))
    kd = jnp.broadcast_to(kd[:, :, None], (RET_HEADS, CHUNK, CHUNK))
    row = lambda i, c: (i * nblk + c)
    table = pl.BlockSpec((RET_HEADS, CHUNK, CHUNK), lambda i, c: (0, 0, 0))
    return pl.pallas_call(
        _ret_kernel,
        grid=(b, nblk),
        in_specs=[
            pl.BlockSpec((CHUNK, RET_QK_WIDTH), lambda i, c: (row(i, c), 0)),
            pl.BlockSpec((CHUNK, RET_QK_WIDTH), lambda i, c: (row(i, c), 1)),
            pl.BlockSpec((CHUNK, RET_WIDTH), lambda i, c: (row(i, c), 0)),
            pl.BlockSpec((CHUNK, RET_WIDTH), lambda i, c: (row(i, c), 1)),
            table, table, table,
            pl.BlockSpec((1, RET_WIDTH), lambda i, c: (0, 0)),
        ],
        out_specs=pl.BlockSpec((CHUNK, RET_WIDTH), lambda i, c: (row(i, c), 0)),
        out_shape=jax.ShapeDtypeStruct((m, RET_WIDTH), BF16),
        scratch_shapes=[pltpu.VMEM((RET_HEADS, RET_DK, RET_DV), F32)],
        compiler_params=_params("parallel", "arbitrary"),
        name="retention",
    )(qk, qk, rest, rest, intra, qd, kd, gn_w)


def _sb_kernel(q_ref, k_ref, v_ref, tt_ref, nw_ref, o_ref, *, nblk):
    scale = 1.0 / math.sqrt(SB_DH)
    row = lax.broadcasted_iota(jnp.int32, (CHUNK, CHUNK), 0)
    col = lax.broadcasted_iota(jnp.int32, (CHUNK, CHUNK), 1)

    def step(q, j, c, acc, mask):
        start = pl.multiple_of(j * CHUNK, CHUNK)
        k = k_ref[pl.ds(start, CHUNK), :]
        v = v_ref[pl.ds(start, CHUNK), :]
        z = lax.dot_general(q, k, _NT, preferred_element_type=F32) * scale
        sp = jnp.maximum(z, 0.0) + jnp.log(1.0 + jnp.exp(-jnp.abs(z)))
        log_beta = z - sp
        log_keep = -sp
        if mask is not None:
            log_keep = jnp.where(mask, log_keep, 0.0)
        hi = log_keep.astype(BF16)
        lo = (log_keep - hi.astype(F32)).astype(BF16)
        tt = tt_ref[...]
        sums = (jnp.dot(hi, tt, preferred_element_type=F32)
                + jnp.dot(lo, tt, preferred_element_type=F32))
        after = sums[:, :CHUNK] + c
        w = jnp.exp(log_beta + after)
        if mask is not None:
            w = jnp.where(mask, w, 0.0)
        acc = acc + jnp.dot(w.astype(BF16), v, preferred_element_type=F32)
        return c + sums[:, CHUNK:], acc

    def q_block(i, carry):
        start = pl.multiple_of(i * CHUNK, CHUNK)
        q = q_ref[pl.ds(start, CHUNK), :]
        zero = jnp.zeros((CHUNK, CHUNK), F32)
        diag_mask = jnp.logical_and(col < row, i * CHUNK + col >= N_PAD)
        c, acc = step(q, i, zero, zero, diag_mask)

        def inner(t, ca):
            return step(q, i - 1 - t, ca[0], ca[1], None)

        c, acc = lax.fori_loop(0, jnp.maximum(i - 1, 0), inner, (c, acc))
        c, acc = lax.cond(
            i > 0,
            lambda c_, a_: step(q, 0, c_, a_, col >= N_PAD),
            lambda c_, a_: (c_, a_),
            c, acc)
        o_ref[pl.ds(start, CHUNK), :] = _rms(acc, nw_ref[...]).astype(BF16)
        return carry

    lax.fori_loop(0, nblk, q_block, 0)


def _stick_breaking(rest, nw, b, tp):
    m = rest.shape[0]
    nblk = tp // CHUNK
    base = 2 * RET_WIDTH // SB_DH
    idx = jnp.arange(CHUNK)
    tri = (idx[:, None] > idx[None, :]).astype(BF16)
    tt = jnp.concatenate([tri, jnp.ones((CHUNK, CHUNK), BF16)], axis=1)
    return pl.pallas_call(
        functools.partial(_sb_kernel, nblk=nblk),
        grid=(b, SB_HEADS),
        in_specs=[
            pl.BlockSpec((tp, SB_DH), lambda i, h: (i, base + h)),
            pl.BlockSpec((tp, SB_DH), lambda i, h: (i, base + SB_HEADS + h)),
            pl.BlockSpec((tp, SB_DH), lambda i, h: (i, base + 2 * SB_HEADS + h)),
            pl.BlockSpec((CHUNK, 2 * CHUNK), lambda i, h: (0, 0)),
            pl.BlockSpec((1, SB_DH), lambda i, h: (0, h)),
        ],
        out_specs=pl.BlockSpec((tp, SB_DH), lambda i, h: (i, h)),
        out_shape=jax.ShapeDtypeStruct((m, SB_WIDTH), BF16),
        compiler_params=_params("parallel", "parallel"),
        name="stick_breaking",
    )(rest, rest, rest, tt, nw)


def _out_proj_kernel(r_ref, s_ref, w1_ref, w2_ref, o_ref):
    o_ref[...] = (jnp.dot(r_ref[...], w1_ref[...], preferred_element_type=F32)
                  + jnp.dot(s_ref[...], w2_ref[...], preferred_element_type=F32))


def _out_proj(ret_out, sb_out, w_out, tp):
    m = ret_out.shape[0]
    d = w_out.shape[1]
    bm = _largest_divisor(tp, 1056, BF16_SUBLANES)
    bn = 1024
    return pl.pallas_call(
        _out_proj_kernel,
        grid=(m // bm, d // bn),
        in_specs=[
            pl.BlockSpec((bm, RET_WIDTH), lambda i, j: (i, 0)),
            pl.BlockSpec((bm, SB_WIDTH), lambda i, j: (i, 0)),
            pl.BlockSpec((RET_WIDTH, bn), lambda i, j: (0, j)),
            pl.BlockSpec((SB_WIDTH, bn), lambda i, j: (1, j)),
        ],
        out_specs=pl.BlockSpec((bm, bn), lambda i, j: (i, j)),
        out_shape=jax.ShapeDtypeStruct((m, d), F32),
        compiler_params=_params("parallel", "arbitrary"),
        name="out_proj",
    )(ret_out, sb_out, w_out, w_out)


def _post_attn_kernel(a_ref, h_ref, w1_ref, w2_ref, h1_ref, hn_ref):
    h1 = h_ref[...] + _rms(a_ref[...], w1_ref[...])
    h1_ref[...] = h1
    hn_ref[...] = _rms(h1, w2_ref[...]).astype(BF16)


def _post_attn(a, h, w_post, w_pre):
    m, d = a.shape
    bm = _largest_divisor(m, 256, BF16_SUBLANES)
    blk = pl.BlockSpec((bm, d), lambda i: (i, 0))
    vec = pl.BlockSpec((1, d), lambda i: (0, 0))
    return pl.pallas_call(
        _post_attn_kernel,
        grid=(m // bm,),
        in_specs=[blk, blk, vec, vec],
        out_specs=[blk, blk],
        out_shape=[jax.ShapeDtypeStruct((m, d), F32),
                   jax.ShapeDtypeStruct((m, d), BF16)],
        compiler_params=_params("parallel"),
        name="post_attn",
    )(a, h, w_post, w_pre)


HALO = BF16_SUBLANES


def _ffn_up_kernel(x_ref, xh_ref, wg_ref, wu_ref, cw_ref, cb_ref, o_ref, g_ref):
    bm = x_ref.shape[0]
    x = x_ref[...]
    wg = wg_ref[...]
    g_ref[:HALO, :] = jnp.dot(xh_ref[...], wg, preferred_element_type=F32)
    gate = jnp.dot(x, wg, preferred_element_type=F32)
    g_ref[HALO:, :] = gate
    up = jnp.dot(x, wu_ref[...], preferred_element_type=F32)
    conv = cb_ref[...] + g_ref[pl.ds(HALO - 2, bm), :] * cw_ref[0:1, :]
    conv = conv + g_ref[pl.ds(HALO - 1, bm), :] * cw_ref[1:2, :]
    conv = conv + gate * cw_ref[2:3, :]
    o_ref[...] = (conv * jax.nn.sigmoid(conv) * up).astype(BF16)


def _ffn_up(hn, w_up, conv_w, conv_b, tp):
    m, d = hn.shape
    dff = w_up.shape[1] // 2
    bm = _largest_divisor(tp, 1056, BF16_SUBLANES)
    bn = 256
    halo_blocks = bm // HALO
    up_base = dff // bn
    return pl.pallas_call(
        _ffn_up_kernel,
        grid=(m // bm, dff // bn),
        in_specs=[
            pl.BlockSpec((bm, d), lambda i, j: (i, 0)),
            pl.BlockSpec((HALO, d), lambda i, j: (jnp.maximum(i * halo_blocks - 1, 0), 0)),
            pl.BlockSpec((d, bn), lambda i, j: (0, j)),
            pl.BlockSpec((d, bn), lambda i, j: (0, j + up_base)),
            pl.BlockSpec((CONV_W, bn), lambda i, j: (0, j)),
            pl.BlockSpec((1, bn), lambda i, j: (0, j)),
        ],
        out_specs=pl.BlockSpec((bm, bn), lambda i, j: (i, j)),
        out_shape=jax.ShapeDtypeStruct((m, dff), BF16),
        scratch_shapes=[pltpu.VMEM((bm + HALO, bn), F32)],
        compiler_params=_params("parallel", "arbitrary"),
        name="ffn_up",
    )(hn, hn, w_up, w_up, conv_w, conv_b)


def _ffn_down(act, w_down, tp):
    m, dff = act.shape
    d = w_down.shape[1]
    bm = _largest_divisor(tp, 528, BF16_SUBLANES)
    bn = 256
    return pl.pallas_call(
        _proj_plain_kernel,
        grid=(m // bm, d // bn),
        in_specs=[
            pl.BlockSpec((bm, dff), lambda i, j: (i, 0)),
            pl.BlockSpec((dff, bn), lambda i, j: (0, j)),
        ],
        out_specs=pl.BlockSpec((bm, bn), lambda i, j: (i, j)),
        out_shape=jax.ShapeDtypeStruct((m, d), F32),
        compiler_params=_params("parallel", "arbitrary"),
        name="ffn_down",
    )(act, w_down)


def _final_kernel(f_ref, h_ref, w_ref, o_ref):
    o_ref[0] = h_ref[...] + _rms(f_ref[...], w_ref[...])


def _final(f, h1, w, b, tp):
    m, d = f.shape
    nblk = tp // CHUNK
    seq = tp - CHUNK
    blk = pl.BlockSpec((CHUNK, d), lambda i, r: (i * nblk + r + 1, 0))
    return pl.pallas_call(
        _final_kernel,
        grid=(b, nblk - 1),
        in_specs=[blk, blk, pl.BlockSpec((1, d), lambda i, r: (0, 0))],
        out_specs=pl.BlockSpec((1, CHUNK, d), lambda i, r: (i, r, 0)),
        out_shape=jax.ShapeDtypeStruct((b, seq, d), F32),
        compiler_params=_params("parallel", "parallel"),
        name="final",
    )(f, h1, w)


def _rotary_tables(tp):
    half = RET_DK // 2
    pos = (jnp.arange(tp) - N_PAD).astype(F32)
    inv = ROPE_BASE ** (-jnp.arange(half, dtype=F32) / half)
    ang = pos[:, None] * inv[None, :]
    return jnp.cos(ang), jnp.sin(ang)


def kernel(x, meta_tokens, attn_pre_norm_w, w_in, ret_gn_w, sb_norm_w, w_out,
           attn_post_norm_w, ffn_pre_norm_w, w_up, conv_w, conv_b, w_down, ffn_post_norm_w):
    b, seq, d = x.shape
    assert seq % CHUNK == 0 and w_in.shape[0] == 1
    tp = seq + CHUNK
    m = b * tp
    dff = w_down.shape[1]

    h, hn = _prep(x, meta_tokens, attn_pre_norm_w)
    h = h.reshape(m, d)
    hn = hn.reshape(m, d)

    cos, sin = _rotary_tables(tp)
    qk, rest = _in_proj(hn, w_in[0].astype(BF16), cos, sin, tp)
    ret_out = _retention(qk, rest, ret_gn_w, b, tp)
    sb_out = _stick_breaking(rest, sb_norm_w, b, tp)
    a = _out_proj(ret_out, sb_out, w_out[0].astype(BF16), tp)
    h1, hn2 = _post_attn(a, h, attn_post_norm_w, ffn_pre_norm_w)

    act = _ffn_up(hn2, w_up[0].astype(BF16), conv_w[0], conv_b, tp)
    f = _ffn_down(act, w_down[0].astype(BF16), tp)
    return _final(f, h1, ffn_post_norm_w, b, tp)
```

```python
import functools
import math

import numpy as np
import jax
import jax.numpy as jnp
from jax import lax
from jax.experimental import pallas as pl
from jax.experimental.pallas import tpu as pltpu

N_META = 16
CHUNK = 128
N_PAD = CHUNK - N_META
RET_HEADS = 8
RET_DK = 256
RET_DV = 256
SB_HEADS = 16
SB_DH = 128
RET_WIDTH = RET_HEADS * RET_DV
RET_QK_WIDTH = RET_HEADS * RET_DK
SB_WIDTH = SB_HEADS * SB_DH
CONV_W = 3
EPS = 1e-6
ROPE_BASE = 10000.0

BF16_SUBLANES = 16
VMEM_LIMIT_BYTES = 56 * 1024 * 1024

F32 = jnp.float32
BF16 = jnp.bfloat16

_NT = (((1,), (1,)), ((), ()))
_TN = (((0,), (0,)), ((), ()))


def _params(*semantics):
    return pltpu.CompilerParams(dimension_semantics=semantics,
                                vmem_limit_bytes=VMEM_LIMIT_BYTES)


def _largest_divisor(n, target, multiple):
    best = None
    for d in range(multiple, min(n, target) + 1, multiple):
        if n % d == 0:
            best = d
    assert best is not None, (n, target, multiple)
    return best


def _rms(v, w):
    ms = jnp.mean(v * v, axis=-1, keepdims=True)
    return v * lax.rsqrt(ms + EPS) * w


def _prep_kernel(x_ref, meta_ref, w_ref, h_ref, hn_ref):
    r = pl.program_id(1)
    w = w_ref[...]
    d = h_ref.shape[-1]

    @pl.when(r == 0)
    def _():
        m = meta_ref[...]
        h_ref[0, :N_PAD, :] = jnp.zeros((N_PAD, d), F32)
        h_ref[0, N_PAD:, :] = m
        hn_ref[0, :N_PAD, :] = jnp.zeros((N_PAD, d), BF16)
        hn_ref[0, N_PAD:, :] = _rms(m, w).astype(BF16)

    @pl.when(r > 0)
    def _():
        v = x_ref[0]
        h_ref[0] = v
        hn_ref[0] = _rms(v, w).astype(BF16)


def _prep(x, meta, w):
    b, seq, d = x.shape
    nblk = seq // CHUNK + 1
    tp = nblk * CHUNK
    return pl.pallas_call(
        _prep_kernel,
        grid=(b, nblk),
        in_specs=[
            pl.BlockSpec((1, CHUNK, d), lambda i, r: (i, jnp.maximum(r - 1, 0), 0)),
            pl.BlockSpec((N_META, d), lambda i, r: (0, 0)),
            pl.BlockSpec((1, d), lambda i, r: (0, 0)),
        ],
        out_specs=[
            pl.BlockSpec((1, CHUNK, d), lambda i, r: (i, r, 0)),
            pl.BlockSpec((1, CHUNK, d), lambda i, r: (i, r, 0)),
        ],
        out_shape=[jax.ShapeDtypeStruct((b, tp, d), F32),
                   jax.ShapeDtypeStruct((b, tp, d), BF16)],
        compiler_params=_params("parallel", "arbitrary"),
        name="prep",
    )(x, meta, w)


def _proj_rot_kernel(x_ref, w_ref, cos_ref, sin_ref, o_ref, *, n_q_blocks):
    j = pl.program_id(1)
    acc = jnp.dot(x_ref[...], w_ref[...], preferred_element_type=F32)
    scale = jnp.where(j < n_q_blocks, RET_DK ** -0.5, 1.0).astype(F32)
    cos = cos_ref[...]
    sin = sin_ref[...]
    half = RET_DK // 2
    for h in range(o_ref.shape[1] // RET_DK):
        lo = h * RET_DK
        x1 = acc[:, lo:lo + half]
        x2 = acc[:, lo + half:lo + RET_DK]
        o_ref[:, lo:lo + half] = ((x1 * cos - x2 * sin) * scale).astype(BF16)
        o_ref[:, lo + half:lo + RET_DK] = ((x1 * sin + x2 * cos) * scale).astype(BF16)


def _proj_plain_kernel(x_ref, w_ref, o_ref):
    o_ref[...] = jnp.dot(x_ref[...], w_ref[...],
                         preferred_element_type=F32).astype(o_ref.dtype)


def _in_proj(hn, w_in, cos, sin, tp):
    m, d = hn.shape
    bm = _largest_divisor(tp, 1056, BF16_SUBLANES)
    bn = 1024
    pos_blocks = tp // bm
    n_rot = 2 * RET_QK_WIDTH
    n_rest = w_in.shape[1] - n_rot
    qk = pl.pallas_call(
        functools.partial(_proj_rot_kernel, n_q_blocks=RET_QK_WIDTH // bn),
        grid=(m // bm, n_rot // bn),
        in_specs=[
            pl.BlockSpec((bm, d), lambda i, j: (i, 0)),
            pl.BlockSpec((d, bn), lambda i, j: (0, j)),
            pl.BlockSpec((bm, RET_DK // 2), lambda i, j: (i % pos_blocks, 0)),
            pl.BlockSpec((bm, RET_DK // 2), lambda i, j: (i % pos_blocks, 0)),
        ],
        out_specs=pl.BlockSpec((bm, bn), lambda i, j: (i, j)),
        out_shape=jax.ShapeDtypeStruct((m, n_rot), BF16),
        compiler_params=_params("parallel", "arbitrary"),
        name="in_proj_rot",
    )(hn, w_in, cos, sin)
    rot_blocks = n_rot // bn
    rest = pl.pallas_call(
        _proj_plain_kernel,
        grid=(m // bm, n_rest // bn),
        in_specs=[
            pl.BlockSpec((bm, d), lambda i, j: (i, 0)),
            pl.BlockSpec((d, bn), lambda i, j: (0, j + rot_blocks)),
        ],
        out_specs=pl.BlockSpec((bm, bn), lambda i, j: (i, j)),
        out_shape=jax.ShapeDtypeStruct((m, n_rest), BF16),
        compiler_params=_params("parallel", "arbitrary"),
        name="in_proj_rest",
    )(hn, w_in)
    return qk, rest


def _ret_log_gamma():
    return np.log(1.0 - 2.0 ** (-5.0 - np.arange(RET_HEADS, dtype=np.float64)))


def _ret_kernel(q_ref, k_ref, v_ref, g_ref, intra_ref, qd_ref, kd_ref, gn_ref,
                o_ref, state_ref):
    c = pl.program_id(1)

    @pl.when(c == 0)
    def _():
        state_ref[...] = jnp.zeros(state_ref.shape, F32)

    chunk_decay = np.exp(CHUNK * _ret_log_gamma()).astype(np.float32)
    for h in range(RET_HEADS):
        sl = slice(h * RET_DK, (h + 1) * RET_DK)
        q = q_ref[:, sl]
        k = k_ref[:, sl]
        v = v_ref[:, sl]
        s = lax.dot_general(q, k, _NT, preferred_element_type=F32) * intra_ref[h]
        intra = jnp.dot(s.astype(BF16), v, preferred_element_type=F32)
        st = state_ref[h]
        qd = qd_ref[h]
        kd = kd_ref[h]
        cross = jnp.dot(q, st.astype(BF16), preferred_element_type=F32)
        cross = cross * jnp.concatenate([qd, qd], axis=1)
        kdk = (k.astype(F32) * jnp.concatenate([kd, kd], axis=1)).astype(BF16)
        upd = lax.dot_general(kdk, v, _TN, preferred_element_type=F32)
        state_ref[h] = st * float(chunk_decay[h]) + upd
        y = intra + cross
        yn = _rms(y, gn_ref[:, sl])
        g = g_ref[:, sl].astype(F32)
        o_ref[:, sl] = (g * jax.nn.sigmoid(g) * yn).astype(BF16)


def _retention(qk, rest, gn_w, b, tp):
    m = qk.shape[0]
    nblk = tp // CHUNK
    lg = jnp.log(1.0 - 2.0 ** (-5.0 - jnp.arange(RET_HEADS, dtype=F32)))
    idx = jnp.arange(CHUNK, dtype=F32)
    diff = idx[:, None] - idx[None, :]
    intra = jnp.where(diff[None] >= 0,
                      jnp.exp(jnp.maximum(diff, 0.0)[None] * lg[:, None, None]), 0.0)
    qd = jnp.exp((idx[None, :] + 1.0) * lg[:, None])
    kd = jnp.exp((CHUNK - 1.0 - idx[None, :]) * lg[:, None])
    qd = jnp.broadcast_to(qd[:, :, None], (RET_HEADS, CHUNK, CHUNK))
    kd = jnp.broadcast_to(kd[:, :, None], (RET_HEADS, CHUNK, CHUNK))
    row = lambda i, c: (i * nblk + c)
    table = pl.BlockSpec((RET_HEADS, CHUNK, CHUNK), lambda i, c: (0, 0, 0))
    return pl.pallas_call(
        _ret_kernel,
        grid=(b, nblk),
        in_specs=[
            pl.BlockSpec((CHUNK, RET_QK_WIDTH), lambda i, c: (row(i, c), 0)),
            pl.BlockSpec((CHUNK, RET_QK_WIDTH), lambda i, c: (row(i, c), 1)),
            pl.BlockSpec((CHUNK, RET_WIDTH), lambda i, c: (row(i, c), 0)),
            pl.BlockSpec((CHUNK, RET_WIDTH), lambda i, c: (row(i, c), 1)),
            table, table, table,
            pl.BlockSpec((1, RET_WIDTH), lambda i, c: (0, 0)),
        ],
        out_specs=pl.BlockSpec((CHUNK, RET_WIDTH), lambda i, c: (row(i, c), 0)),
        out_shape=jax.ShapeDtypeStruct((m, RET_WIDTH), BF16),
        scratch_shapes=[pltpu.VMEM((RET_HEADS, RET_DK, RET_DV), F32)],
        compiler_params=_params("parallel", "arbitrary"),
        name="retention",
    )(qk, qk, rest, rest, intra, qd, kd, gn_w)


SB_GROUP = 4
SB_UNDERFLOW = 110.0


def _sb_kernel(q_ref, k_ref, v_ref, tt_ref, nw_ref, o_ref, *, nblk):
    scale = 1.0 / math.sqrt(SB_DH)
    row = lax.broadcasted_iota(jnp.int32, (CHUNK, CHUNK), 0)
    col = lax.broadcasted_iota(jnp.int32, (CHUNK, CHUNK), 1)
    heads = [slice(g * SB_DH, (g + 1) * SB_DH) for g in range(SB_GROUP)]

    def step(qs, j, cs, accs, mask):
        start = pl.multiple_of(j * CHUNK, CHUNK)
        k = k_ref[pl.ds(start, CHUNK), :]
        v = v_ref[pl.ds(start, CHUNK), :]
        log_beta, split = [], []
        for g in range(SB_GROUP):
            z = lax.dot_general(qs[g], k[:, heads[g]], _NT,
                                preferred_element_type=F32) * scale
            sp = jnp.maximum(z, 0.0) + jnp.log(1.0 + jnp.exp(-jnp.abs(z)))
            log_beta.append(z - sp)
            log_keep = jnp.where(mask, -sp, 0.0)
            hi = log_keep.astype(BF16)
            lo = (log_keep - hi.astype(F32)).astype(BF16)
            split.append(jnp.concatenate([hi, lo], axis=1))
        sums = jnp.dot(jnp.concatenate(split, axis=0), tt_ref[...],
                       preferred_element_type=F32)
        new_cs, new_accs = [], []
        for g in range(SB_GROUP):
            s = sums[g * CHUNK:(g + 1) * CHUNK]
            after = s[:, :CHUNK] + cs[g]
            w = jnp.where(mask, jnp.exp(log_beta[g] + after), 0.0)
            new_accs.append(accs[g] + jnp.dot(w.astype(BF16), v[:, heads[g]],
                                              preferred_element_type=F32))
            new_cs.append(cs[g] + s[:, CHUNK:])
        return tuple(new_cs), tuple(new_accs)

    def least_decayed(cs):
        return jnp.max(functools.reduce(jnp.maximum, cs))

    def q_block(i, carry):
        start = pl.multiple_of(i * CHUNK, CHUNK)
        q = q_ref[pl.ds(start, CHUNK), :]
        qs = [q[:, heads[g]] for g in range(SB_GROUP)]
        zero = (jnp.zeros((CHUNK, CHUNK), F32),) * SB_GROUP
        diag_mask = jnp.logical_and(col < row, i * CHUNK + col >= N_PAD)
        cs, accs = step(qs, i, zero, zero, diag_mask)

        def cond(st):
            return jnp.logical_and(st[0] >= 0, st[1] > -SB_UNDERFLOW)

        def body(st):
            j = st[0]
            cs_, accs_ = step(qs, j, st[2:2 + SB_GROUP], st[2 + SB_GROUP:],
                              j * CHUNK + col >= N_PAD)
            return (j - 1, least_decayed(cs_)) + cs_ + accs_

        st = lax.while_loop(cond, body, (i - 1, least_decayed(cs)) + cs + accs)
        accs = st[2 + SB_GROUP:]
        for g in range(SB_GROUP):
            o_ref[pl.ds(start, CHUNK), heads[g]] = _rms(
                accs[g], nw_ref[:, heads[g]]).astype(BF16)
        return carry

    lax.fori_loop(0, nblk, q_block, 0)


def _stick_breaking(rest, nw, b, tp):
    m = rest.shape[0]
    nblk = tp // CHUNK
    width = SB_GROUP * SB_DH
    groups = SB_HEADS // SB_GROUP
    base = 2 * RET_WIDTH // width
    idx = jnp.arange(CHUNK)
    tri = (idx[:, None] > idx[None, :]).astype(BF16)
    tt = jnp.concatenate([tri, jnp.ones((CHUNK, CHUNK), BF16)], axis=1)
    tt = jnp.concatenate([tt, tt], axis=0)
    return pl.pallas_call(
        functools.partial(_sb_kernel, nblk=nblk),
        grid=(b, groups),
        in_specs=[
            pl.BlockSpec((tp, width), lambda i, h: (i, base + h)),
            pl.BlockSpec((tp, width), lambda i, h: (i, base + groups + h)),
            pl.BlockSpec((tp, width), lambda i, h: (i, base + 2 * groups + h)),
            pl.BlockSpec((2 * CHUNK, 2 * CHUNK), lambda i, h: (0, 0)),
            pl.BlockSpec((1, width), lambda i, h: (0, h)),
        ],
        out_specs=pl.BlockSpec((tp, width), lambda i, h: (i, h)),
        out_shape=jax.ShapeDtypeStruct((m, SB_WIDTH), BF16),
        compiler_params=_params("parallel", "parallel"),
        name="stick_breaking",
    )(rest, rest, rest, tt, nw)


def _out_proj_kernel(r_ref, s_ref, w1_ref, w2_ref, o_ref):
    o_ref[...] = (jnp.dot(r_ref[...], w1_ref[...], preferred_element_type=F32)
                  + jnp.dot(s_ref[...], w2_ref[...], preferred_element_type=F32))


def _out_proj(ret_out, sb_out, w_out, tp):
    m = ret_out.shape[0]
    d = w_out.shape[1]
    bm = _largest_divisor(tp, 1056, BF16_SUBLANES)
    bn = 1024
    return pl.pallas_call(
        _out_proj_kernel,
        grid=(m // bm, d // bn),
        in_specs=[
            pl.BlockSpec((bm, RET_WIDTH), lambda i, j: (i, 0)),
            pl.BlockSpec((bm, SB_WIDTH), lambda i, j: (i, 0)),
            pl.BlockSpec((RET_WIDTH, bn), lambda i, j: (0, j)),
            pl.BlockSpec((SB_WIDTH, bn), lambda i, j: (1, j)),
        ],
        out_specs=pl.BlockSpec((bm, bn), lambda i, j: (i, j)),
        out_shape=jax.ShapeDtypeStruct((m, d), F32),
        compiler_params=_params("parallel", "arbitrary"),
        name="out_proj",
    )(ret_out, sb_out, w_out, w_out)


def _post_attn_kernel(a_ref, h_ref, w1_ref, w2_ref, h1_ref, hn_ref):
    h1 = h_ref[...] + _rms(a_ref[...], w1_ref[...])
    h1_ref[...] = h1
    hn_ref[...] = _rms(h1, w2_ref[...]).astype(BF16)


def _post_attn(a, h, w_post, w_pre):
    m, d = a.shape
    bm = _largest_divisor(m, 256, BF16_SUBLANES)
    blk = pl.BlockSpec((bm, d), lambda i: (i, 0))
    vec = pl.BlockSpec((1, d), lambda i: (0, 0))
    return pl.pallas_call(
        _post_attn_kernel,
        grid=(m // bm,),
        in_specs=[blk, blk, vec, vec],
        out_specs=[blk, blk],
        out_shape=[jax.ShapeDtypeStruct((m, d), F32),
                   jax.ShapeDtypeStruct((m, d), BF16)],
        compiler_params=_params("parallel"),
        name="post_attn",
    )(a, h, w_post, w_pre)


HALO = BF16_SUBLANES


def _ffn_up_kernel(x_ref, xh_ref, wg_ref, wu_ref, cw_ref, cb_ref, o_ref, g_ref):
    bm = x_ref.shape[0]
    x = x_ref[...]
    wg = wg_ref[...]
    g_ref[:HALO, :] = jnp.dot(xh_ref[...], wg, preferred_element_type=F32)
    gate = jnp.dot(x, wg, preferred_element_type=F32)
    g_ref[HALO:, :] = gate
    up = jnp.dot(x, wu_ref[...], preferred_element_type=F32)
    conv = cb_ref[...] + g_ref[pl.ds(HALO - 2, bm), :] * cw_ref[0:1, :]
    conv = conv + g_ref[pl.ds(HALO - 1, bm), :] * cw_ref[1:2, :]
    conv = conv + gate * cw_ref[2:3, :]
    o_ref[...] = (conv * jax.nn.sigmoid(conv) * up).astype(BF16)


def _ffn_up(hn, w_up, conv_w, conv_b, tp):
    m, d = hn.shape
    dff = w_up.shape[1] // 2
    bm = _largest_divisor(tp, 1056, BF16_SUBLANES)
    bn = 256
    halo_blocks = bm // HALO
    up_base = dff // bn
    return pl.pallas_call(
        _ffn_up_kernel,
        grid=(m // bm, dff // bn),
        in_specs=[
            pl.BlockSpec((bm, d), lambda i, j: (i, 0)),
            pl.BlockSpec((HALO, d), lambda i, j: (jnp.maximum(i * halo_blocks - 1, 0), 0)),
            pl.BlockSpec((d, bn), lambda i, j: (0, j)),
            pl.BlockSpec((d, bn), lambda i, j: (0, j + up_base)),
            pl.BlockSpec((CONV_W, bn), lambda i, j: (0, j)),
            pl.BlockSpec((1, bn), lambda i, j: (0, j)),
        ],
        out_specs=pl.BlockSpec((bm, bn), lambda i, j: (i, j)),
        out_shape=jax.ShapeDtypeStruct((m, dff), BF16),
        scratch_shapes=[pltpu.VMEM((bm + HALO, bn), F32)],
        compiler_params=_params("parallel", "arbitrary"),
        name="ffn_up",
    )(hn, hn, w_up, w_up, conv_w, conv_b)


def _ffn_down(act, w_down, tp):
    m, dff = act.shape
    d = w_down.shape[1]
    bm = _largest_divisor(tp, 528, BF16_SUBLANES)
    bn = 256
    return pl.pallas_call(
        _proj_plain_kernel,
        grid=(m // bm, d // bn),
        in_specs=[
            pl.BlockSpec((bm, dff), lambda i, j: (i, 0)),
            pl.BlockSpec((dff, bn), lambda i, j: (0, j)),
        ],
        out_specs=pl.BlockSpec((bm, bn), lambda i, j: (i, j)),
        out_shape=jax.ShapeDtypeStruct((m, d), F32),
        compiler_params=_params("parallel", "arbitrary"),
        name="ffn_down",
    )(act, w_down)


def _final_kernel(f_ref, h_ref, w_ref, o_ref):
    o_ref[0] = h_ref[...] + _rms(f_ref[...], w_ref[...])


def _final(f, h1, w, b, tp):
    m, d = f.shape
    nblk = tp // CHUNK
    seq = tp - CHUNK
    blk = pl.BlockSpec((CHUNK, d), lambda i, r: (i * nblk + r + 1, 0))
    return pl.pallas_call(
        _final_kernel,
        grid=(b, nblk - 1),
        in_specs=[blk, blk, pl.BlockSpec((1, d), lambda i, r: (0, 0))],
        out_specs=pl.BlockSpec((1, CHUNK, d), lambda i, r: (i, r, 0)),
        out_shape=jax.ShapeDtypeStruct((b, seq, d), F32),
        compiler_params=_params("parallel", "parallel"),
        name="final",
    )(f, h1, w)


def _rotary_tables(tp):
    half = RET_DK // 2
    pos = (jnp.arange(tp) - N_PAD).astype(F32)
    inv = ROPE_BASE ** (-jnp.arange(half, dtype=F32) / half)
    ang = pos[:, None] * inv[None, :]
    return jnp.cos(ang), jnp.sin(ang)


def kernel(x, meta_tokens, attn_pre_norm_w, w_in, ret_gn_w, sb_norm_w, w_out,
           attn_post_norm_w, ffn_pre_norm_w, w_up, conv_w, conv_b, w_down, ffn_post_norm_w):
    b, seq, d = x.shape
    assert seq % CHUNK == 0 and w_in.shape[0] == 1
    tp = seq + CHUNK
    m = b * tp
    dff = w_down.shape[1]

    h, hn = _prep(x, meta_tokens, attn_pre_norm_w)
    h = h.reshape(m, d)
    hn = hn.reshape(m, d)

    cos, sin = _rotary_tables(tp)
    qk, rest = _in_proj(hn, w_in[0].astype(BF16), cos, sin, tp)
    ret_out = _retention(qk, rest, ret_gn_w, b, tp)
    sb_out = _stick_breaking(rest, sb_norm_w, b, tp)
    a = _out_proj(ret_out, sb_out, w_out[0].astype(BF16), tp)
    h1, hn2 = _post_attn(a, h, attn_post_norm_w, ffn_pre_norm_w)

    act = _ffn_up(hn2, w_up[0].astype(BF16), conv_w[0], conv_b, tp)
    f = _ffn_down(act, w_down[0].astype(BF16), tp)
    return _final(f, h1, ffn_post_norm_w, b, tp)
```

```python
import functools
import math

import numpy as np
import jax
import jax.numpy as jnp
from jax import lax
from jax.experimental import pallas as pl
from jax.experimental.pallas import tpu as pltpu

N_META = 16
CHUNK = 128
N_PAD = CHUNK - N_META
RET_HEADS = 8
RET_DK = 256
RET_DV = 256
SB_HEADS = 16
SB_DH = 128
RET_WIDTH = RET_HEADS * RET_DV
RET_QK_WIDTH = RET_HEADS * RET_DK
SB_WIDTH = SB_HEADS * SB_DH
CONV_W = 3
EPS = 1e-6
ROPE_BASE = 10000.0

BF16_SUBLANES = 16
VMEM_LIMIT_BYTES = 56 * 1024 * 1024

F32 = jnp.float32
BF16 = jnp.bfloat16

_NT = (((1,), (1,)), ((), ()))
_TN = (((0,), (0,)), ((), ()))


def _params(*semantics):
    return pltpu.CompilerParams(dimension_semantics=semantics,
                                vmem_limit_bytes=VMEM_LIMIT_BYTES)


def _largest_divisor(n, target, multiple):
    best = None
    for d in range(multiple, min(n, target) + 1, multiple):
        if n % d == 0:
            best = d
    assert best is not None, (n, target, multiple)
    return best


def _rms(v, w):
    ms = jnp.mean(v * v, axis=-1, keepdims=True)
    return v * lax.rsqrt(ms + EPS) * w


def _prep_kernel(x_ref, meta_ref, w_ref, h_ref, hn_ref):
    r = pl.program_id(1)
    w = w_ref[...]
    d = h_ref.shape[-1]

    @pl.when(r == 0)
    def _():
        m = meta_ref[...]
        h_ref[0, :N_PAD, :] = jnp.zeros((N_PAD, d), F32)
        h_ref[0, N_PAD:, :] = m
        hn_ref[0, :N_PAD, :] = jnp.zeros((N_PAD, d), BF16)
        hn_ref[0, N_PAD:, :] = _rms(m, w).astype(BF16)

    @pl.when(r > 0)
    def _():
        v = x_ref[0]
        h_ref[0] = v
        hn_ref[0] = _rms(v, w).astype(BF16)


def _prep(x, meta, w):
    b, seq, d = x.shape
    nblk = seq // CHUNK + 1
    tp = nblk * CHUNK
    return pl.pallas_call(
        _prep_kernel,
        grid=(b, nblk),
        in_specs=[
            pl.BlockSpec((1, CHUNK, d), lambda i, r: (i, jnp.maximum(r - 1, 0), 0)),
            pl.BlockSpec((N_META, d), lambda i, r: (0, 0)),
            pl.BlockSpec((1, d), lambda i, r: (0, 0)),
        ],
        out_specs=[
            pl.BlockSpec((1, CHUNK, d), lambda i, r: (i, r, 0)),
            pl.BlockSpec((1, CHUNK, d), lambda i, r: (i, r, 0)),
        ],
        out_shape=[jax.ShapeDtypeStruct((b, tp, d), F32),
                   jax.ShapeDtypeStruct((b, tp, d), BF16)],
        compiler_params=_params("parallel", "arbitrary"),
        name="prep",
    )(x, meta, w)


def _proj_rot_kernel(x_ref, w_ref, cos_ref, sin_ref, o_ref, *, n_q_blocks):
    j = pl.program_id(1)
    acc = jnp.dot(x_ref[...], w_ref[...], preferred_element_type=F32)
    scale = jnp.where(j < n_q_blocks, RET_DK ** -0.5, 1.0).astype(F32)
    cos = cos_ref[...]
    sin = sin_ref[...]
    half = RET_DK // 2
    for h in range(o_ref.shape[1] // RET_DK):
        lo = h * RET_DK
        x1 = acc[:, lo:lo + half]
        x2 = acc[:, lo + half:lo + RET_DK]
        o_ref[:, lo:lo + half] = ((x1 * cos - x2 * sin) * scale).astype(BF16)
        o_ref[:, lo + half:lo + RET_DK] = ((x1 * sin + x2 * cos) * scale).astype(BF16)


def _proj_plain_kernel(x_ref, w_ref, o_ref):
    o_ref[...] = jnp.dot(x_ref[...], w_ref[...],
                         preferred_element_type=F32).astype(o_ref.dtype)


def _in_proj(hn, w_in, cos, sin, tp):
    m, d = hn.shape
    bm = _largest_divisor(tp, 1056, BF16_SUBLANES)
    bn = 1024
    pos_blocks = tp // bm
    n_rot = 2 * RET_QK_WIDTH
    n_rest = w_in.shape[1] - n_rot
    qk = pl.pallas_call(
        functools.partial(_proj_rot_kernel, n_q_blocks=RET_QK_WIDTH // bn),
        grid=(m // bm, n_rot // bn),
        in_specs=[
            pl.BlockSpec((bm, d), lambda i, j: (i, 0)),
            pl.BlockSpec((d, bn), lambda i, j: (0, j)),
            pl.BlockSpec((bm, RET_DK // 2), lambda i, j: (i % pos_blocks, 0)),
            pl.BlockSpec((bm, RET_DK // 2), lambda i, j: (i % pos_blocks, 0)),
        ],
        out_specs=pl.BlockSpec((bm, bn), lambda i, j: (i, j)),
        out_shape=jax.ShapeDtypeStruct((m, n_rot), BF16),
        compiler_params=_params("parallel", "arbitrary"),
        name="in_proj_rot",
    )(hn, w_in, cos, sin)
    rot_blocks = n_rot // bn
    rest = pl.pallas_call(
        _proj_plain_kernel,
        grid=(m // bm, n_rest // bn),
        in_specs=[
            pl.BlockSpec((bm, d), lambda i, j: (i, 0)),
            pl.BlockSpec((d, bn), lambda i, j: (0, j + rot_blocks)),
        ],
        out_specs=pl.BlockSpec((bm, bn), lambda i, j: (i, j)),
        out_shape=jax.ShapeDtypeStruct((m, n_rest), BF16),
        compiler_params=_params("parallel", "arbitrary"),
        name="in_proj_rest",
    )(hn, w_in)
    return qk, rest


def _ret_log_gamma():
    return np.log(1.0 - 2.0 ** (-5.0 - np.arange(RET_HEADS, dtype=np.float64)))


def _ret_kernel(q_ref, k_ref, v_ref, g_ref, intra_ref, qd_ref, kd_ref, gn_ref,
                o_ref, state_ref):
    c = pl.program_id(1)

    @pl.when(c == 0)
    def _():
        state_ref[...] = jnp.zeros(state_ref.shape, F32)

    chunk_decay = np.exp(CHUNK * _ret_log_gamma()).astype(np.float32)
    for h in range(RET_HEADS):
        sl = slice(h * RET_DK, (h + 1) * RET_DK)
        q = q_ref[:, sl]
        k = k_ref[:, sl]
        v = v_ref[:, sl]
        s = lax.dot_general(q, k, _NT, preferred_element_type=F32) * intra_ref[h]
        intra = jnp.dot(s.astype(BF16), v, preferred_element_type=F32)
        st = state_ref[h]
        qd = qd_ref[h]
        kd = kd_ref[h]
        cross = jnp.dot(q, st.astype(BF16), preferred_element_type=F32)
        cross = cross * jnp.concatenate([qd, qd], axis=1)
        kdk = (k.astype(F32) * jnp.concatenate([kd, kd], axis=1)).astype(BF16)
        upd = lax.dot_general(kdk, v, _TN, preferred_element_type=F32)
        state_ref[h] = st * float(chunk_decay[h]) + upd
        y = intra + cross
        yn = _rms(y, gn_ref[:, sl])
        g = g_ref[:, sl].astype(F32)
        o_ref[:, sl] = (g * jax.nn.sigmoid(g) * yn).astype(BF16)


def _retention(qk, rest, gn_w, b, tp):
    m = qk.shape[0]
    nblk = tp // CHUNK
    lg = jnp.log(1.0 - 2.0 ** (-5.0 - jnp.arange(RET_HEADS, dtype=F32)))
    idx = jnp.arange(CHUNK, dtype=F32)
    diff = idx[:, None] - idx[None, :]
    intra = jnp.where(diff[None] >= 0,
                      jnp.exp(jnp.maximum(diff, 0.0)[None] * lg[:, None, None]), 0.0)
    qd = jnp.exp((idx[None, :] + 1.0) * lg[:, None])
    kd = jnp.exp((CHUNK - 1.0 - idx[None, :]) * lg[:, None])
    qd = jnp.broadcast_to(qd[:, :, None], (RET_HEADS, CHUNK, CHUNK))
    kd = jnp.broadcast_to(kd[:, :, None], (RET_HEADS, CHUNK, CHUNK))
    row = lambda i, c: (i * nblk + c)
    table = pl.BlockSpec((RET_HEADS, CHUNK, CHUNK), lambda i, c: (0, 0, 0))
    return pl.pallas_call(
        _ret_kernel,
        grid=(b, nblk),
        in_specs=[
            pl.BlockSpec((CHUNK, RET_QK_WIDTH), lambda i, c: (row(i, c), 0)),
            pl.BlockSpec((CHUNK, RET_QK_WIDTH), lambda i, c: (row(i, c), 1)),
            pl.BlockSpec((CHUNK, RET_WIDTH), lambda i, c: (row(i, c), 0)),
            pl.BlockSpec((CHUNK, RET_WIDTH), lambda i, c: (row(i, c), 1)),
            table, table, table,
            pl.BlockSpec((1, RET_WIDTH), lambda i, c: (0, 0)),
        ],
        out_specs=pl.BlockSpec((CHUNK, RET_WIDTH), lambda i, c: (row(i, c), 0)),
        out_shape=jax.ShapeDtypeStruct((m, RET_WIDTH), BF16),
        scratch_shapes=[pltpu.VMEM((RET_HEADS, RET_DK, RET_DV), F32)],
        compiler_params=_params("parallel", "arbitrary"),
        name="retention",
    )(qk, qk, rest, rest, intra, qd, kd, gn_w)


SB_GROUP = 4
SB_UNDERFLOW = 110.0


def _sb_kernel(q_ref, k_ref, v_ref, tt_ref, nw_ref, o_ref, *, nblk):
    scale = 1.0 / math.sqrt(SB_DH)
    row = lax.broadcasted_iota(jnp.int32, (CHUNK, CHUNK), 0)
    col = lax.broadcasted_iota(jnp.int32, (CHUNK, CHUNK), 1)
    heads = [slice(g * SB_DH, (g + 1) * SB_DH) for g in range(SB_GROUP)]

    def step(qs, j, cs, accs, mask):
        start = pl.multiple_of(j * CHUNK, CHUNK)
        k = k_ref[pl.ds(start, CHUNK), :]
        v = v_ref[pl.ds(start, CHUNK), :]
        log_beta, split = [], []
        for g in range(SB_GROUP):
            z = lax.dot_general(qs[g], k[:, heads[g]], _NT,
                                preferred_element_type=F32) * scale
            sp = jnp.maximum(z, 0.0) + jnp.log(1.0 + jnp.exp(-jnp.abs(z)))
            log_beta.append(z - sp)
            log_keep = jnp.where(mask, -sp, 0.0)
            hi = log_keep.astype(BF16)
            lo = (log_keep - hi.astype(F32)).astype(BF16)
            split.append(jnp.concatenate([hi, lo], axis=1))
        sums = jnp.dot(jnp.concatenate(split, axis=0), tt_ref[...],
                       preferred_element_type=F32)
        new_cs, new_accs = [], []
        for g in range(SB_GROUP):
            s = sums[g * CHUNK:(g + 1) * CHUNK]
            after = s[:, :CHUNK] + cs[g]
            w = jnp.where(mask, jnp.exp(log_beta[g] + after), 0.0)
            new_accs.append(accs[g] + jnp.dot(w.astype(BF16), v[:, heads[g]],
                                              preferred_element_type=F32))
            new_cs.append(cs[g] + s[:, CHUNK:])
        return tuple(new_cs), tuple(new_accs)

    def least_decayed(cs):
        return jnp.max(functools.reduce(jnp.maximum, cs))

    def q_block(i, carry):
        start = pl.multiple_of(i * CHUNK, CHUNK)
        q = q_ref[pl.ds(start, CHUNK), :]
        qs = [q[:, heads[g]] for g in range(SB_GROUP)]
        zero = (jnp.zeros((CHUNK, CHUNK), F32),) * SB_GROUP
        diag_mask = jnp.logical_and(col < row, i * CHUNK + col >= N_PAD)
        cs, accs = step(qs, i, zero, zero, diag_mask)

        def cond(st):
            return jnp.logical_and(st[0] >= 0, st[1] > -SB_UNDERFLOW)

        def body(st):
            j = st[0]
            cs_, accs_ = step(qs, j, st[2:2 + SB_GROUP], st[2 + SB_GROUP:],
                              j * CHUNK + col >= N_PAD)
            return (j - 1, least_decayed(cs_)) + cs_ + accs_

        st = lax.while_loop(cond, body, (i - 1, least_decayed(cs)) + cs + accs)
        accs = st[2 + SB_GROUP:]
        for g in range(SB_GROUP):
            o_ref[pl.ds(start, CHUNK), heads[g]] = _rms(
                accs[g], nw_ref[:, heads[g]]).astype(BF16)
        return carry

    lax.fori_loop(0, nblk, q_block, 0)


def _stick_breaking(rest, nw, b, tp):
    m = rest.shape[0]
    nblk = tp // CHUNK
    width = SB_GROUP * SB_DH
    groups = SB_HEADS // SB_GROUP
    base = 2 * RET_WIDTH // width
    idx = jnp.arange(CHUNK)
    tri = (idx[:, None] > idx[None, :]).astype(BF16)
    tt = jnp.concatenate([tri, jnp.ones((CHUNK, CHUNK), BF16)], axis=1)
    tt = jnp.concatenate([tt, tt], axis=0)
    return pl.pallas_call(
        functools.partial(_sb_kernel, nblk=nblk),
        grid=(b, groups),
        in_specs=[
            pl.BlockSpec((tp, width), lambda i, h: (i, base + h)),
            pl.BlockSpec((tp, width), lambda i, h: (i, base + groups + h)),
            pl.BlockSpec((tp, width), lambda i, h: (i, base + 2 * groups + h)),
            pl.BlockSpec((2 * CHUNK, 2 * CHUNK), lambda i, h: (0, 0)),
            pl.BlockSpec((1, width), lambda i, h: (0, h)),
        ],
        out_specs=pl.BlockSpec((tp, width), lambda i, h: (i, h)),
        out_shape=jax.ShapeDtypeStruct((m, SB_WIDTH), BF16),
        compiler_params=_params("parallel", "parallel"),
        name="stick_breaking",
    )(rest, rest, rest, tt, nw)


def _out_proj_kernel(r_ref, s_ref, w1_ref, w2_ref, o_ref):
    o_ref[...] = (jnp.dot(r_ref[...], w1_ref[...], preferred_element_type=F32)
                  + jnp.dot(s_ref[...], w2_ref[...], preferred_element_type=F32))


def _out_proj(ret_out, sb_out, w_out, tp):
    m = ret_out.shape[0]
    d = w_out.shape[1]
    bm = _largest_divisor(tp, 1056, BF16_SUBLANES)
    bn = 1024
    return pl.pallas_call(
        _out_proj_kernel,
        grid=(m // bm, d // bn),
        in_specs=[
            pl.BlockSpec((bm, RET_WIDTH), lambda i, j: (i, 0)),
            pl.BlockSpec((bm, SB_WIDTH), lambda i, j: (i, 0)),
            pl.BlockSpec((RET_WIDTH, bn), lambda i, j: (0, j)),
            pl.BlockSpec((SB_WIDTH, bn), lambda i, j: (1, j)),
        ],
        out_specs=pl.BlockSpec((bm, bn), lambda i, j: (i, j)),
        out_shape=jax.ShapeDtypeStruct((m, d), F32),
        compiler_params=_params("parallel", "arbitrary"),
        name="out_proj",
    )(ret_out, sb_out, w_out, w_out)


def _post_attn_kernel(a_ref, h_ref, w1_ref, w2_ref, h1_ref, hn_ref):
    h1 = h_ref[...] + _rms(a_ref[...], w1_ref[...])
    h1_ref[...] = h1
    hn_ref[...] = _rms(h1, w2_ref[...]).astype(BF16)


def _post_attn(a, h, w_post, w_pre):
    m, d = a.shape
    bm = _largest_divisor(m, 256, BF16_SUBLANES)
    blk = pl.BlockSpec((bm, d), lambda i: (i, 0))
    vec = pl.BlockSpec((1, d), lambda i: (0, 0))
    return pl.pallas_call(
        _post_attn_kernel,
        grid=(m // bm,),
        in_specs=[blk, blk, vec, vec],
        out_specs=[blk, blk],
        out_shape=[jax.ShapeDtypeStruct((m, d), F32),
                   jax.ShapeDtypeStruct((m, d), BF16)],
        compiler_params=_params("parallel"),
        name="post_attn",
    )(a, h, w_post, w_pre)


HALO = BF16_SUBLANES


FFN_SUB_ROWS = 352


def _ffn_up_kernel(x_ref, xh_ref, wg_ref, wu_ref, cw_ref, cb_ref, o_ref, g_ref):
    bm = x_ref.shape[0]
    sub = FFN_SUB_ROWS if bm % FFN_SUB_ROWS == 0 else bm
    wg = wg_ref[...]
    wu = wu_ref[...]
    g_ref[:HALO, :] = jnp.dot(xh_ref[...], wg, preferred_element_type=F32)
    for r in range(bm // sub):
        x = x_ref[r * sub:(r + 1) * sub, :]
        lo = HALO + r * sub
        g_ref[lo:lo + sub, :] = jnp.dot(x, wg, preferred_element_type=F32)
        up = jnp.dot(x, wu, preferred_element_type=F32)
        conv = cb_ref[...] + g_ref[lo - 2:lo - 2 + sub, :] * cw_ref[0:1, :]
        conv = conv + g_ref[lo - 1:lo - 1 + sub, :] * cw_ref[1:2, :]
        conv = conv + g_ref[lo:lo + sub, :] * cw_ref[2:3, :]
        o_ref[r * sub:(r + 1) * sub, :] = (conv * jax.nn.sigmoid(conv) * up).astype(BF16)


def _ffn_up(hn, w_up, conv_w, conv_b, tp):
    m, d = hn.shape
    dff = w_up.shape[1] // 2
    bm = _largest_divisor(tp, 2112, BF16_SUBLANES)
    bn = 256
    halo_blocks = bm // HALO
    up_base = dff // bn
    return pl.pallas_call(
        _ffn_up_kernel,
        grid=(m // bm, dff // bn),
        in_specs=[
            pl.BlockSpec((bm, d), lambda i, j: (i, 0), pipeline_mode=pl.Buffered(1)),
            pl.BlockSpec((HALO, d), lambda i, j: (jnp.maximum(i * halo_blocks - 1, 0), 0)),
            pl.BlockSpec((d, bn), lambda i, j: (0, j)),
            pl.BlockSpec((d, bn), lambda i, j: (0, j + up_base)),
            pl.BlockSpec((CONV_W, bn), lambda i, j: (0, j)),
            pl.BlockSpec((1, bn), lambda i, j: (0, j)),
        ],
        out_specs=pl.BlockSpec((bm, bn), lambda i, j: (i, j)),
        out_shape=jax.ShapeDtypeStruct((m, dff), BF16),
        scratch_shapes=[pltpu.VMEM((bm + HALO, bn), F32)],
        compiler_params=_params("parallel", "arbitrary"),
        name="ffn_up",
    )(hn, hn, w_up, w_up, conv_w, conv_b)


def _ffn_down(act, w_down, tp):
    m, dff = act.shape
    d = w_down.shape[1]
    bm = _largest_divisor(tp, 528, BF16_SUBLANES)
    bn = 512
    return pl.pallas_call(
        _proj_plain_kernel,
        grid=(m // bm, d // bn),
        in_specs=[
            pl.BlockSpec((bm, dff), lambda i, j: (i, 0)),
            pl.BlockSpec((dff, bn), lambda i, j: (0, j)),
        ],
        out_specs=pl.BlockSpec((bm, bn), lambda i, j: (i, j)),
        out_shape=jax.ShapeDtypeStruct((m, d), F32),
        compiler_params=_params("parallel", "arbitrary"),
        name="ffn_down",
    )(act, w_down)


def _final_kernel(f_ref, h_ref, w_ref, o_ref):
    o_ref[0] = h_ref[...] + _rms(f_ref[...], w_ref[...])


def _final(f, h1, w, b, tp):
    m, d = f.shape
    nblk = tp // CHUNK
    seq = tp - CHUNK
    blk = pl.BlockSpec((CHUNK, d), lambda i, r: (i * nblk + r + 1, 0))
    return pl.pallas_call(
        _final_kernel,
        grid=(b, nblk - 1),
        in_specs=[blk, blk, pl.BlockSpec((1, d), lambda i, r: (0, 0))],
        out_specs=pl.BlockSpec((1, CHUNK, d), lambda i, r: (i, r, 0)),
        out_shape=jax.ShapeDtypeStruct((b, seq, d), F32),
        compiler_params=_params("parallel", "parallel"),
        name="final",
    )(f, h1, w)


def _rotary_tables(tp):
    half = RET_DK // 2
    pos = (jnp.arange(tp) - N_PAD).astype(F32)
    inv = ROPE_BASE ** (-jnp.arange(half, dtype=F32) / half)
    ang = pos[:, None] * inv[None, :]
    return jnp.cos(ang), jnp.sin(ang)


def kernel(x, meta_tokens, attn_pre_norm_w, w_in, ret_gn_w, sb_norm_w, w_out,
           attn_post_norm_w, ffn_pre_norm_w, w_up, conv_w, conv_b, w_down, ffn_post_norm_w):
    b, seq, d = x.shape
    assert seq % CHUNK == 0 and w_in.shape[0] == 1
    tp = seq + CHUNK
    m = b * tp
    dff = w_down.shape[1]

    h, hn = _prep(x, meta_tokens, attn_pre_norm_w)
    h = h.reshape(m, d)
    hn = hn.reshape(m, d)

    cos, sin = _rotary_tables(tp)
    qk, rest = _in_proj(hn, w_in[0].astype(BF16), cos, sin, tp)
    ret_out = _retention(qk, rest, ret_gn_w, b, tp)
    sb_out = _stick_breaking(rest, sb_norm_w, b, tp)
    a = _out_proj(ret_out, sb_out, w_out[0].astype(BF16), tp)
    h1, hn2 = _post_attn(a, h, attn_post_norm_w, ffn_pre_norm_w)

    act = _ffn_up(hn2, w_up[0].astype(BF16), conv_w[0], conv_b, tp)
    f = _ffn_down(act, w_down[0].astype(BF16), tp)
    return _final(f, h1, ffn_post_norm_w, b, tp)
```

```python
import functools
import math

import numpy as np
import jax
import jax.numpy as jnp
from jax import lax
from jax.experimental import pallas as pl
from jax.experimental.pallas import tpu as pltpu

N_META = 16
CHUNK = 128
N_PAD = CHUNK - N_META
RET_HEADS = 8
RET_DK = 256
RET_DV = 256
SB_HEADS = 16
SB_DH = 128
RET_WIDTH = RET_HEADS * RET_DV
RET_QK_WIDTH = RET_HEADS * RET_DK
SB_WIDTH = SB_HEADS * SB_DH
CONV_W = 3
EPS = 1e-6
ROPE_BASE = 10000.0

BF16_SUBLANES = 16
VMEM_LIMIT_BYTES = 56 * 1024 * 1024

F32 = jnp.float32
BF16 = jnp.bfloat16

_NT = (((1,), (1,)), ((), ()))
_TN = (((0,), (0,)), ((), ()))


def _params(*semantics):
    return pltpu.CompilerParams(dimension_semantics=semantics,
                                vmem_limit_bytes=VMEM_LIMIT_BYTES)


def _largest_divisor(n, target, multiple):
    best = None
    for d in range(multiple, min(n, target) + 1, multiple):
        if n % d == 0:
            best = d
    assert best is not None, (n, target, multiple)
    return best


def _rms(v, w):
    ms = jnp.mean(v * v, axis=-1, keepdims=True)
    return v * lax.rsqrt(ms + EPS) * w


def _padded_rows(x_ref, meta_ref, r):
    d = x_ref.shape[-1]
    first = jnp.concatenate([jnp.zeros((N_PAD, d), F32), meta_ref[...]], axis=0)
    return jnp.where(r == 0, first, x_ref[0])


def _x_block_spec(d):
    return pl.BlockSpec((1, CHUNK, d), lambda i, r: (i, jnp.maximum(r - 1, 0), 0))


def _prep_kernel(x_ref, meta_ref, w_ref, hn_ref):
    h = _padded_rows(x_ref, meta_ref, pl.program_id(1))
    hn_ref[0] = _rms(h, w_ref[...]).astype(BF16)


def _prep(x, meta, w):
    b, seq, d = x.shape
    nblk = seq // CHUNK + 1
    tp = nblk * CHUNK
    return pl.pallas_call(
        _prep_kernel,
        grid=(b, nblk),
        in_specs=[
            _x_block_spec(d),
            pl.BlockSpec((N_META, d), lambda i, r: (0, 0)),
            pl.BlockSpec((1, d), lambda i, r: (0, 0)),
        ],
        out_specs=pl.BlockSpec((1, CHUNK, d), lambda i, r: (i, r, 0)),
        out_shape=jax.ShapeDtypeStruct((b, tp, d), BF16),
        compiler_params=_params("parallel", "arbitrary"),
        name="prep",
    )(x, meta, w)


def _proj_rot_kernel(x_ref, w_ref, cos_ref, sin_ref, o_ref, *, n_q_blocks):
    j = pl.program_id(1)
    acc = jnp.dot(x_ref[...], w_ref[...], preferred_element_type=F32)
    scale = jnp.where(j < n_q_blocks, RET_DK ** -0.5, 1.0).astype(F32)
    cos = cos_ref[...]
    sin = sin_ref[...]
    half = RET_DK // 2
    for h in range(o_ref.shape[1] // RET_DK):
        lo = h * RET_DK
        x1 = acc[:, lo:lo + half]
        x2 = acc[:, lo + half:lo + RET_DK]
        o_ref[:, lo:lo + half] = ((x1 * cos - x2 * sin) * scale).astype(BF16)
        o_ref[:, lo + half:lo + RET_DK] = ((x1 * sin + x2 * cos) * scale).astype(BF16)


def _proj_plain_kernel(x_ref, w_ref, o_ref):
    o_ref[...] = jnp.dot(x_ref[...], w_ref[...],
                         preferred_element_type=F32).astype(o_ref.dtype)


def _in_proj(hn, w_in, cos, sin, tp):
    m, d = hn.shape
    bm = _largest_divisor(tp, 1056, BF16_SUBLANES)
    bn = 1024
    pos_blocks = tp // bm
    n_rot = 2 * RET_QK_WIDTH
    n_rest = w_in.shape[1] - n_rot
    qk = pl.pallas_call(
        functools.partial(_proj_rot_kernel, n_q_blocks=RET_QK_WIDTH // bn),
        grid=(m // bm, n_rot // bn),
        in_specs=[
            pl.BlockSpec((bm, d), lambda i, j: (i, 0)),
            pl.BlockSpec((d, bn), lambda i, j: (0, j)),
            pl.BlockSpec((bm, RET_DK // 2), lambda i, j: (i % pos_blocks, 0)),
            pl.BlockSpec((bm, RET_DK // 2), lambda i, j: (i % pos_blocks, 0)),
        ],
        out_specs=pl.BlockSpec((bm, bn), lambda i, j: (i, j)),
        out_shape=jax.ShapeDtypeStruct((m, n_rot), BF16),
        compiler_params=_params("parallel", "arbitrary"),
        name="in_proj_rot",
    )(hn, w_in, cos, sin)
    rot_blocks = n_rot // bn
    rest = pl.pallas_call(
        _proj_plain_kernel,
        grid=(m // bm, n_rest // bn),
        in_specs=[
            pl.BlockSpec((bm, d), lambda i, j: (i, 0)),
            pl.BlockSpec((d, bn), lambda i, j: (0, j + rot_blocks)),
        ],
        out_specs=pl.BlockSpec((bm, bn), lambda i, j: (i, j)),
        out_shape=jax.ShapeDtypeStruct((m, n_rest), BF16),
        compiler_params=_params("parallel", "arbitrary"),
        name="in_proj_rest",
    )(hn, w_in)
    return qk, rest


def _ret_log_gamma():
    return np.log(1.0 - 2.0 ** (-5.0 - np.arange(RET_HEADS, dtype=np.float64)))


def _ret_kernel(q_ref, k_ref, v_ref, g_ref, intra_ref, qd_ref, kd_ref, gn_ref,
                o_ref, state_ref):
    c = pl.program_id(1)

    @pl.when(c == 0)
    def _():
        state_ref[...] = jnp.zeros(state_ref.shape, F32)

    chunk_decay = np.exp(CHUNK * _ret_log_gamma()).astype(np.float32)
    for h in range(RET_HEADS):
        sl = slice(h * RET_DK, (h + 1) * RET_DK)
        q = q_ref[:, sl]
        k = k_ref[:, sl]
        v = v_ref[:, sl]
        s = lax.dot_general(q, k, _NT, preferred_element_type=F32) * intra_ref[h]
        intra = jnp.dot(s.astype(BF16), v, preferred_element_type=F32)
        st = state_ref[h]
        qd = qd_ref[h]
        kd = kd_ref[h]
        cross = jnp.dot(q, st.astype(BF16), preferred_element_type=F32)
        cross = cross * jnp.concatenate([qd, qd], axis=1)
        kdk = (k.astype(F32) * jnp.concatenate([kd, kd], axis=1)).astype(BF16)
        upd = lax.dot_general(kdk, v, _TN, preferred_element_type=F32)
        state_ref[h] = st * float(chunk_decay[h]) + upd
        y = intra + cross
        yn = _rms(y, gn_ref[:, sl])
        g = g_ref[:, sl].astype(F32)
        o_ref[:, sl] = (g * jax.nn.sigmoid(g) * yn).astype(BF16)


def _retention(qk, rest, gn_w, b, tp):
    m = qk.shape[0]
    nblk = tp // CHUNK
    lg = jnp.log(1.0 - 2.0 ** (-5.0 - jnp.arange(RET_HEADS, dtype=F32)))
    idx = jnp.arange(CHUNK, dtype=F32)
    diff = idx[:, None] - idx[None, :]
    intra = jnp.where(diff[None] >= 0,
                      jnp.exp(jnp.maximum(diff, 0.0)[None] * lg[:, None, None]), 0.0)
    qd = jnp.exp((idx[None, :] + 1.0) * lg[:, None])
    kd = jnp.exp((CHUNK - 1.0 - idx[None, :]) * lg[:, None])
    qd = jnp.broadcast_to(qd[:, :, None], (RET_HEADS, CHUNK, CHUNK))
    kd = jnp.broadcast_to(kd[:, :, None], (RET_HEADS, CHUNK, CHUNK))
    row = lambda i, c: (i * nblk + c)
    table = pl.BlockSpec((RET_HEADS, CHUNK, CHUNK), lambda i, c: (0, 0, 0))
    return pl.pallas_call(
        _ret_kernel,
        grid=(b, nblk),
        in_specs=[
            pl.BlockSpec((CHUNK, RET_QK_WIDTH), lambda i, c: (row(i, c), 0)),
            pl.BlockSpec((CHUNK, RET_QK_WIDTH), lambda i, c: (row(i, c), 1)),
            pl.BlockSpec((CHUNK, RET_WIDTH), lambda i, c: (row(i, c), 0)),
            pl.BlockSpec((CHUNK, RET_WIDTH), lambda i, c: (row(i, c), 1)),
            table, table, table,
            pl.BlockSpec((1, RET_WIDTH), lambda i, c: (0, 0)),
        ],
        out_specs=pl.BlockSpec((CHUNK, RET_WIDTH), lambda i, c: (row(i, c), 0)),
        out_shape=jax.ShapeDtypeStruct((m, RET_WIDTH), BF16),
        scratch_shapes=[pltpu.VMEM((RET_HEADS, RET_DK, RET_DV), F32)],
        compiler_params=_params("parallel", "arbitrary"),
        name="retention",
    )(qk, qk, rest, rest, intra, qd, kd, gn_w)


SB_GROUP = 8
SB_UNDERFLOW = 110.0


def _sb_kernel(q_ref, k_ref, v_ref, tt_ref, nw_ref, o_ref, *, nblk):
    scale = 1.0 / math.sqrt(SB_DH)
    row = lax.broadcasted_iota(jnp.int32, (CHUNK, CHUNK), 0)
    col = lax.broadcasted_iota(jnp.int32, (CHUNK, CHUNK), 1)
    heads = [slice(g * SB_DH, (g + 1) * SB_DH) for g in range(SB_GROUP)]

    def step(qs, j, cs, accs, mask):
        start = pl.multiple_of(j * CHUNK, CHUNK)
        k = k_ref[pl.ds(start, CHUNK), :]
        v = v_ref[pl.ds(start, CHUNK), :]
        log_beta, split = [], []
        for g in range(SB_GROUP):
            z = lax.dot_general(qs[g], k[:, heads[g]], _NT,
                                preferred_element_type=F32) * scale
            sp = jnp.maximum(z, 0.0) + jnp.log(1.0 + jnp.exp(-jnp.abs(z)))
            log_beta.append(z - sp)
            log_keep = jnp.where(mask, -sp, 0.0)
            hi = log_keep.astype(BF16)
            lo = (log_keep - hi.astype(F32)).astype(BF16)
            split.append(jnp.concatenate([hi, lo], axis=1))
        sums = jnp.dot(jnp.concatenate(split, axis=0), tt_ref[...],
                       preferred_element_type=F32)
        new_cs, new_accs = [], []
        for g in range(SB_GROUP):
            s = sums[g * CHUNK:(g + 1) * CHUNK]
            after = s[:, :CHUNK] + cs[g]
            w = jnp.where(mask, jnp.exp(log_beta[g] + after), 0.0)
            new_accs.append(accs[g] + jnp.dot(w.astype(BF16), v[:, heads[g]],
                                              preferred_element_type=F32))
            new_cs.append(cs[g] + s[:, CHUNK:])
        return tuple(new_cs), tuple(new_accs)

    def least_decayed(cs):
        return jnp.max(functools.reduce(jnp.maximum, cs))

    def q_block(i, carry):
        start = pl.multiple_of(i * CHUNK, CHUNK)
        q = q_ref[pl.ds(start, CHUNK), :]
        qs = [q[:, heads[g]] for g in range(SB_GROUP)]
        zero = (jnp.zeros((CHUNK, CHUNK), F32),) * SB_GROUP
        diag_mask = jnp.logical_and(col < row, i * CHUNK + col >= N_PAD)
        cs, accs = step(qs, i, zero, zero, diag_mask)

        def cond(st):
            return jnp.logical_and(st[0] >= 0, st[1] > -SB_UNDERFLOW)

        def body(st):
            j = st[0]
            cs_, accs_ = step(qs, j, st[2:2 + SB_GROUP], st[2 + SB_GROUP:],
                              j * CHUNK + col >= N_PAD)
            return (j - 1, least_decayed(cs_)) + cs_ + accs_

        st = lax.while_loop(cond, body, (i - 1, least_decayed(cs)) + cs + accs)
        accs = st[2 + SB_GROUP:]
        for g in range(SB_GROUP):
            o_ref[pl.ds(start, CHUNK), heads[g]] = _rms(
                accs[g], nw_ref[:, heads[g]]).astype(BF16)
        return carry

    lax.fori_loop(0, nblk, q_block, 0)


def _stick_breaking(rest, nw, b, tp):
    m = rest.shape[0]
    nblk = tp // CHUNK
    width = SB_GROUP * SB_DH
    groups = SB_HEADS // SB_GROUP
    base = 2 * RET_WIDTH // width
    idx = jnp.arange(CHUNK)
    tri = (idx[:, None] > idx[None, :]).astype(BF16)
    tt = jnp.concatenate([tri, jnp.ones((CHUNK, CHUNK), BF16)], axis=1)
    tt = jnp.concatenate([tt, tt], axis=0)
    return pl.pallas_call(
        functools.partial(_sb_kernel, nblk=nblk),
        grid=(b, groups),
        in_specs=[
            pl.BlockSpec((tp, width), lambda i, h: (i, base + h),
                         pipeline_mode=pl.Buffered(1)),
            pl.BlockSpec((tp, width), lambda i, h: (i, base + groups + h),
                         pipeline_mode=pl.Buffered(1)),
            pl.BlockSpec((tp, width), lambda i, h: (i, base + 2 * groups + h),
                         pipeline_mode=pl.Buffered(1)),
            pl.BlockSpec((2 * CHUNK, 2 * CHUNK), lambda i, h: (0, 0)),
            pl.BlockSpec((1, width), lambda i, h: (0, h)),
        ],
        out_specs=pl.BlockSpec((tp, width), lambda i, h: (i, h)),
        out_shape=jax.ShapeDtypeStruct((m, SB_WIDTH), BF16),
        compiler_params=_params("parallel", "parallel"),
        name="stick_breaking",
    )(rest, rest, rest, tt, nw)


def _out_proj_kernel(r_ref, s_ref, w1_ref, w2_ref, o_ref):
    o_ref[...] = (jnp.dot(r_ref[...], w1_ref[...], preferred_element_type=F32)
                  + jnp.dot(s_ref[...], w2_ref[...], preferred_element_type=F32)
                  ).astype(o_ref.dtype)


def _out_proj(ret_out, sb_out, w_out, tp):
    m = ret_out.shape[0]
    d = w_out.shape[1]
    bm = _largest_divisor(tp, 1056, BF16_SUBLANES)
    bn = 1024
    return pl.pallas_call(
        _out_proj_kernel,
        grid=(m // bm, d // bn),
        in_specs=[
            pl.BlockSpec((bm, RET_WIDTH), lambda i, j: (i, 0)),
            pl.BlockSpec((bm, SB_WIDTH), lambda i, j: (i, 0)),
            pl.BlockSpec((RET_WIDTH, bn), lambda i, j: (0, j)),
            pl.BlockSpec((SB_WIDTH, bn), lambda i, j: (1, j)),
        ],
        out_specs=pl.BlockSpec((bm, bn), lambda i, j: (i, j)),
        out_shape=jax.ShapeDtypeStruct((m, d), BF16),
        compiler_params=_params("parallel", "arbitrary"),
        name="out_proj",
    )(ret_out, sb_out, w_out, w_out)


def _post_attn_kernel(a_ref, x_ref, meta_ref, w1_ref, w2_ref, h1_ref, hn_ref):
    h = _padded_rows(x_ref, meta_ref, pl.program_id(1))
    h1 = h + _rms(a_ref[...].astype(F32), w1_ref[...])
    h1_ref[...] = h1
    hn_ref[...] = _rms(h1, w2_ref[...]).astype(BF16)


def _post_attn(a, x, meta, w_post, w_pre):
    m, d = a.shape
    b = x.shape[0]
    nblk = m // (b * CHUNK)
    blk = pl.BlockSpec((CHUNK, d), lambda i, r: (i * nblk + r, 0))
    vec = pl.BlockSpec((1, d), lambda i, r: (0, 0))
    return pl.pallas_call(
        _post_attn_kernel,
        grid=(b, nblk),
        in_specs=[blk, _x_block_spec(d),
                  pl.BlockSpec((N_META, d), lambda i, r: (0, 0)), vec, vec],
        out_specs=[blk, blk],
        out_shape=[jax.ShapeDtypeStruct((m, d), F32),
                   jax.ShapeDtypeStruct((m, d), BF16)],
        compiler_params=_params("parallel", "arbitrary"),
        name="post_attn",
    )(a, x, meta, w_post, w_pre)


HALO = BF16_SUBLANES


FFN_SUB_ROWS = 352


def _ffn_up_kernel(x_ref, xh_ref, wg_ref, wu_ref, cw_ref, cb_ref, o_ref, g_ref):
    bm = x_ref.shape[0]
    sub = FFN_SUB_ROWS if bm % FFN_SUB_ROWS == 0 else bm
    wg = wg_ref[...]
    wu = wu_ref[...]
    g_ref[:HALO, :] = jnp.dot(xh_ref[...], wg, preferred_element_type=F32)
    for r in range(bm // sub):
        x = x_ref[r * sub:(r + 1) * sub, :]
        lo = HALO + r * sub
        g_ref[lo:lo + sub, :] = jnp.dot(x, wg, preferred_element_type=F32)
        up = jnp.dot(x, wu, preferred_element_type=F32)
        conv = cb_ref[...] + g_ref[lo - 2:lo - 2 + sub, :] * cw_ref[0:1, :]
        conv = conv + g_ref[lo - 1:lo - 1 + sub, :] * cw_ref[1:2, :]
        conv = conv + g_ref[lo:lo + sub, :] * cw_ref[2:3, :]
        o_ref[r * sub:(r + 1) * sub, :] = (conv * jax.nn.sigmoid(conv) * up).astype(BF16)


def _ffn_up(hn, w_up, conv_w, conv_b, tp):
    m, d = hn.shape
    dff = w_up.shape[1] // 2
    bm = _largest_divisor(tp, 2112, BF16_SUBLANES)
    bn = 256
    halo_blocks = bm // HALO
    up_base = dff // bn
    return pl.pallas_call(
        _ffn_up_kernel,
        grid=(m // bm, dff // bn),
        in_specs=[
            pl.BlockSpec((bm, d), lambda i, j: (i, 0), pipeline_mode=pl.Buffered(1)),
            pl.BlockSpec((HALO, d), lambda i, j: (jnp.maximum(i * halo_blocks - 1, 0), 0)),
            pl.BlockSpec((d, bn), lambda i, j: (0, j)),
            pl.BlockSpec((d, bn), lambda i, j: (0, j + up_base)),
            pl.BlockSpec((CONV_W, bn), lambda i, j: (0, j)),
            pl.BlockSpec((1, bn), lambda i, j: (0, j)),
        ],
        out_specs=pl.BlockSpec((bm, bn), lambda i, j: (i, j)),
        out_shape=jax.ShapeDtypeStruct((m, dff), BF16),
        scratch_shapes=[pltpu.VMEM((bm + HALO, bn), F32)],
        compiler_params=_params("parallel", "arbitrary"),
        name="ffn_up",
    )(hn, hn, w_up, w_up, conv_w, conv_b)


def _ffn_down(act, w_down, tp):
    m, dff = act.shape
    d = w_down.shape[1]
    bm = _largest_divisor(tp, 528, BF16_SUBLANES)
    bn = 512
    return pl.pallas_call(
        _proj_plain_kernel,
        grid=(m // bm, d // bn),
        in_specs=[
            pl.BlockSpec((bm, dff), lambda i, j: (i, 0)),
            pl.BlockSpec((dff, bn), lambda i, j: (0, j)),
        ],
        out_specs=pl.BlockSpec((bm, bn), lambda i, j: (i, j)),
        out_shape=jax.ShapeDtypeStruct((m, d), BF16),
        compiler_params=_params("parallel", "arbitrary"),
        name="ffn_down",
    )(act, w_down)


def _final_kernel(f_ref, h_ref, w_ref, o_ref):
    o_ref[0] = h_ref[...] + _rms(f_ref[...].astype(F32), w_ref[...])


def _final(f, h1, w, b, tp):
    m, d = f.shape
    nblk = tp // CHUNK
    seq = tp - CHUNK
    blk = pl.BlockSpec((CHUNK, d), lambda i, r: (i * nblk + r + 1, 0))
    return pl.pallas_call(
        _final_kernel,
        grid=(b, nblk - 1),
        in_specs=[blk, blk, pl.BlockSpec((1, d), lambda i, r: (0, 0))],
        out_specs=pl.BlockSpec((1, CHUNK, d), lambda i, r: (i, r, 0)),
        out_shape=jax.ShapeDtypeStruct((b, seq, d), F32),
        compiler_params=_params("parallel", "parallel"),
        name="final",
    )(f, h1, w)


def _rotary_tables(tp):
    half = RET_DK // 2
    pos = (jnp.arange(tp) - N_PAD).astype(F32)
    inv = ROPE_BASE ** (-jnp.arange(half, dtype=F32) / half)
    ang = pos[:, None] * inv[None, :]
    return jnp.cos(ang), jnp.sin(ang)


def kernel(x, meta_tokens, attn_pre_norm_w, w_in, ret_gn_w, sb_norm_w, w_out,
           attn_post_norm_w, ffn_pre_norm_w, w_up, conv_w, conv_b, w_down, ffn_post_norm_w):
    b, seq, d = x.shape
    assert seq % CHUNK == 0 and w_in.shape[0] == 1
    tp = seq + CHUNK
    m = b * tp

    hn = _prep(x, meta_tokens, attn_pre_norm_w).reshape(m, d)

    cos, sin = _rotary_tables(tp)
    qk, rest = _in_proj(hn, w_in[0].astype(BF16), cos, sin, tp)
    ret_out = _retention(qk, rest, ret_gn_w, b, tp)
    sb_out = _stick_breaking(rest, sb_norm_w, b, tp)
    a = _out_proj(ret_out, sb_out, w_out[0].astype(BF16), tp)
    h1, hn2 = _post_attn(a, x, meta_tokens, attn_post_norm_w, ffn_pre_norm_w)

    act = _ffn_up(hn2, w_up[0].astype(BF16), conv_w[0], conv_b, tp)
    f = _ffn_down(act, w_down[0].astype(BF16), tp)
    return _final(f, h1, ffn_post_norm_w, b, tp)
```

```python
import functools
import math

import numpy as np
import jax
import jax.numpy as jnp
from jax import lax
from jax.experimental import pallas as pl
from jax.experimental.pallas import tpu as pltpu

N_META = 16
CHUNK = 128
N_PAD = CHUNK - N_META
RET_HEADS = 8
RET_DK = 256
RET_DV = 256
SB_HEADS = 16
SB_DH = 128
RET_WIDTH = RET_HEADS * RET_DV
RET_QK_WIDTH = RET_HEADS * RET_DK
SB_WIDTH = SB_HEADS * SB_DH
CONV_W = 3
EPS = 1e-6
ROPE_BASE = 10000.0

BF16_SUBLANES = 16
VMEM_LIMIT_BYTES = 58 * 1024 * 1024

F32 = jnp.float32
BF16 = jnp.bfloat16

_NT = (((1,), (1,)), ((), ()))
_TN = (((0,), (0,)), ((), ()))


def _params(*semantics):
    return pltpu.CompilerParams(dimension_semantics=semantics,
                                vmem_limit_bytes=VMEM_LIMIT_BYTES)


def _largest_divisor(n, target, multiple):
    best = None
    for d in range(multiple, min(n, target) + 1, multiple):
        if n % d == 0:
            best = d
    assert best is not None, (n, target, multiple)
    return best


def _rms(v, w):
    ms = jnp.mean(v * v, axis=-1, keepdims=True)
    return v * lax.rsqrt(ms + EPS) * w


def _padded_rows(x_ref, meta_ref, r):
    d = x_ref.shape[-1]
    first = jnp.concatenate([jnp.zeros((N_PAD, d), F32), meta_ref[...]], axis=0)
    return jnp.where(r == 0, first, x_ref[0])


def _x_block_spec(d):
    return pl.BlockSpec((1, CHUNK, d), lambda i, r: (i, jnp.maximum(r - 1, 0), 0))


def _prep_kernel(x_ref, meta_ref, w_ref, hn_ref):
    h = _padded_rows(x_ref, meta_ref, pl.program_id(1))
    hn_ref[0] = _rms(h, w_ref[...]).astype(BF16)


def _prep(x, meta, w):
    b, seq, d = x.shape
    nblk = seq // CHUNK + 1
    tp = nblk * CHUNK
    return pl.pallas_call(
        _prep_kernel,
        grid=(b, nblk),
        in_specs=[
            _x_block_spec(d),
            pl.BlockSpec((N_META, d), lambda i, r: (0, 0)),
            pl.BlockSpec((1, d), lambda i, r: (0, 0)),
        ],
        out_specs=pl.BlockSpec((1, CHUNK, d), lambda i, r: (i, r, 0)),
        out_shape=jax.ShapeDtypeStruct((b, tp, d), BF16),
        compiler_params=_params("parallel", "arbitrary"),
        name="prep",
    )(x, meta, w)


CAST_BLOCK_BYTES = 4 * 1024 * 1024


def _cast_plan(numel, steps):
    unit = BF16_SUBLANES * 128
    if numel % unit:
        return None
    q = numel // unit
    for nb in range(min(steps, q), 0, -1):
        if q % nb == 0:
            cols = q // nb * 128
            return (nb, cols) if BF16_SUBLANES * cols * 4 <= CAST_BLOCK_BYTES else None
    return None


def _with_casts(kernel_fn, n_in, n_out, weights, steps, nj):
    plans, in_specs, out_specs, out_shapes, operands = [], [], [], [], []
    start = 0
    for w in weights:
        plan = _cast_plan(w.size, steps - start)
        plans.append(plan)
        if plan is None:
            continue
        nb, cols = plan
        spec = pl.BlockSpec(
            (BF16_SUBLANES, cols),
            lambda i, j, start=start, nb=nb: (jnp.clip(i * nj + j - start, 0, nb - 1), 0))
        in_specs.append(spec)
        out_specs.append(spec)
        out_shapes.append(jax.ShapeDtypeStruct((nb * BF16_SUBLANES, cols), BF16))
        operands.append(w.reshape(nb * BF16_SUBLANES, cols))
        start += nb
    n_cast = len(operands)

    def wrapped(*refs):
        ins, rest = refs[:n_in], refs[n_in:]
        cast_in, rest = rest[:n_cast], rest[n_cast:]
        outs, cast_out = rest[:n_out], rest[n_out:]
        kernel_fn(*ins, *outs)
        for ci, co in zip(cast_in, cast_out):
            co[...] = ci[...].astype(BF16)

    def finish(cast_results):
        it = iter(cast_results)
        return [w.astype(BF16) if p is None else next(it).reshape(w.shape)
                for w, p in zip(weights, plans)]

    return wrapped, in_specs, out_specs, out_shapes, operands, finish


def _proj_rot_kernel(x_ref, w_ref, cos_ref, sin_ref, o_ref, *, n_q_blocks):
    j = pl.program_id(1)
    acc = jnp.dot(x_ref[...], w_ref[...], preferred_element_type=F32)
    scale = jnp.where(j < n_q_blocks, RET_DK ** -0.5, 1.0).astype(F32)
    cos = cos_ref[...]
    sin = sin_ref[...]
    half = RET_DK // 2
    for h in range(o_ref.shape[1] // RET_DK):
        lo = h * RET_DK
        x1 = acc[:, lo:lo + half]
        x2 = acc[:, lo + half:lo + RET_DK]
        o_ref[:, lo:lo + half] = ((x1 * cos - x2 * sin) * scale).astype(BF16)
        o_ref[:, lo + half:lo + RET_DK] = ((x1 * sin + x2 * cos) * scale).astype(BF16)


def _proj_plain_kernel(x_ref, w_ref, o_ref):
    o_ref[...] = jnp.dot(x_ref[...], w_ref[...],
                         preferred_element_type=F32).astype(o_ref.dtype)


def _in_proj(hn, w_in, cos, sin, tp, later_weights_rot, later_weights_rest):
    m, d = hn.shape
    bm = _largest_divisor(tp, 1056, BF16_SUBLANES)
    bn = 1024
    pos_blocks = tp // bm
    n_rot = 2 * RET_QK_WIDTH
    n_rest = w_in.shape[1] - n_rot

    grid = (m // bm, n_rot // bn)
    body, c_in, c_out, c_shape, c_ops, finish_rot = _with_casts(
        functools.partial(_proj_rot_kernel, n_q_blocks=RET_QK_WIDTH // bn),
        4, 1, later_weights_rot, grid[0] * grid[1], grid[1])
    qk, *cast_rot = pl.pallas_call(
        body,
        grid=grid,
        in_specs=[
            pl.BlockSpec((bm, d), lambda i, j: (i, 0)),
            pl.BlockSpec((d, bn), lambda i, j: (0, j)),
            pl.BlockSpec((bm, RET_DK // 2), lambda i, j: (i % pos_blocks, 0)),
            pl.BlockSpec((bm, RET_DK // 2), lambda i, j: (i % pos_blocks, 0)),
        ] + c_in,
        out_specs=[pl.BlockSpec((bm, bn), lambda i, j: (i, j))] + c_out,
        out_shape=[jax.ShapeDtypeStruct((m, n_rot), BF16)] + c_shape,
        compiler_params=_params("arbitrary", "arbitrary"),
        name="in_proj_rot",
    )(hn, w_in, cos, sin, *c_ops)

    rot_blocks = n_rot // bn
    grid = (m // bm, n_rest // bn)
    body, c_in, c_out, c_shape, c_ops, finish_rest = _with_casts(
        _proj_plain_kernel, 2, 1, later_weights_rest, grid[0] * grid[1], grid[1])
    rest, *cast_rest = pl.pallas_call(
        body,
        grid=grid,
        in_specs=[
            pl.BlockSpec((bm, d), lambda i, j: (i, 0)),
            pl.BlockSpec((d, bn), lambda i, j: (0, j + rot_blocks)),
        ] + c_in,
        out_specs=[pl.BlockSpec((bm, bn), lambda i, j: (i, j))] + c_out,
        out_shape=[jax.ShapeDtypeStruct((m, n_rest), BF16)] + c_shape,
        compiler_params=_params("arbitrary", "arbitrary"),
        name="in_proj_rest",
    )(hn, w_in, *c_ops)
    return qk, rest, finish_rot(cast_rot), finish_rest(cast_rest)


def _ret_log_gamma():
    return np.log(1.0 - 2.0 ** (-5.0 - np.arange(RET_HEADS, dtype=np.float64)))


def _ret_kernel(q_ref, k_ref, v_ref, g_ref, intra_ref, qd_ref, kd_ref, gn_ref,
                o_ref, state_ref):
    c = pl.program_id(1)

    @pl.when(c == 0)
    def _():
        state_ref[...] = jnp.zeros(state_ref.shape, F32)

    chunk_decay = np.exp(CHUNK * _ret_log_gamma()).astype(np.float32)
    for h in range(RET_HEADS):
        sl = slice(h * RET_DK, (h + 1) * RET_DK)
        q = q_ref[:, sl]
        k = k_ref[:, sl]
        v = v_ref[:, sl]
        s = lax.dot_general(q, k, _NT, preferred_element_type=F32) * intra_ref[h]
        intra = jnp.dot(s.astype(BF16), v, preferred_element_type=F32)
        st = state_ref[h]
        qd = qd_ref[h]
        kd = kd_ref[h]
        cross = jnp.dot(q, st.astype(BF16), preferred_element_type=F32)
        cross = cross * jnp.concatenate([qd, qd], axis=1)
        kdk = (k.astype(F32) * jnp.concatenate([kd, kd], axis=1)).astype(BF16)
        upd = lax.dot_general(kdk, v, _TN, preferred_element_type=F32)
        state_ref[h] = st * float(chunk_decay[h]) + upd
        y = intra + cross
        yn = _rms(y, gn_ref[:, sl])
        g = g_ref[:, sl].astype(F32)
        o_ref[:, sl] = (g * jax.nn.sigmoid(g) * yn).astype(BF16)


def _retention(qk, rest, gn_w, b, tp):
    m = qk.shape[0]
    nblk = tp // CHUNK
    lg = jnp.log(1.0 - 2.0 ** (-5.0 - jnp.arange(RET_HEADS, dtype=F32)))
    idx = jnp.arange(CHUNK, dtype=F32)
    diff = idx[:, None] - idx[None, :]
    intra = jnp.where(diff[None] >= 0,
                      jnp.exp(jnp.maximum(diff, 0.0)[None] * lg[:, None, None]), 0.0)
    qd = jnp.exp((idx[None, :] + 1.0) * lg[:, None])
    kd = jnp.exp((CHUNK - 1.0 - idx[None, :]) * lg[:, None])
    qd = jnp.broadcast_to(qd[:, :, None], (RET_HEADS, CHUNK, CHUNK))
    kd = jnp.broadcast_to(kd[:, :, None], (RET_HEADS, CHUNK, CHUNK))
    row = lambda i, c: (i * nblk + c)
    table = pl.BlockSpec((RET_HEADS, CHUNK, CHUNK), lambda i, c: (0, 0, 0))
    return pl.pallas_call(
        _ret_kernel,
        grid=(b, nblk),
        in_specs=[
            pl.BlockSpec((CHUNK, RET_QK_WIDTH), lambda i, c: (row(i, c), 0)),
            pl.BlockSpec((CHUNK, RET_QK_WIDTH), lambda i, c: (row(i, c), 1)),
            pl.BlockSpec((CHUNK, RET_WIDTH), lambda i, c: (row(i, c), 0)),
            pl.BlockSpec((CHUNK, RET_WIDTH), lambda i, c: (row(i, c), 1)),
            table, table, table,
            pl.BlockSpec((1, RET_WIDTH), lambda i, c: (0, 0)),
        ],
        out_specs=pl.BlockSpec((CHUNK, RET_WIDTH), lambda i, c: (row(i, c), 0)),
        out_shape=jax.ShapeDtypeStruct((m, RET_WIDTH), BF16),
        scratch_shapes=[pltpu.VMEM((RET_HEADS, RET_DK, RET_DV), F32)],
        compiler_params=_params("parallel", "arbitrary"),
        name="retention",
    )(qk, qk, rest, rest, intra, qd, kd, gn_w)


SB_GROUP = 8
SB_UNDERFLOW = 110.0


def _sb_kernel(q_ref, k_ref, v_ref, tt_ref, nw_ref, o_ref, *, nblk):
    scale = 1.0 / math.sqrt(SB_DH)
    row = lax.broadcasted_iota(jnp.int32, (CHUNK, CHUNK), 0)
    col = lax.broadcasted_iota(jnp.int32, (CHUNK, CHUNK), 1)
    heads = [slice(g * SB_DH, (g + 1) * SB_DH) for g in range(SB_GROUP)]

    def step(qs, j, cs, accs, mask):
        start = pl.multiple_of(j * CHUNK, CHUNK)
        k = k_ref[pl.ds(start, CHUNK), :]
        v = v_ref[pl.ds(start, CHUNK), :]
        log_beta, split = [], []
        for g in range(SB_GROUP):
            z = lax.dot_general(qs[g], k[:, heads[g]], _NT,
                                preferred_element_type=F32) * scale
            sp = jnp.maximum(z, 0.0) + jnp.log(1.0 + jnp.exp(-jnp.abs(z)))
            log_beta.append(z - sp)
            log_keep = jnp.where(mask, -sp, 0.0)
            hi = log_keep.astype(BF16)
            lo = (log_keep - hi.astype(F32)).astype(BF16)
            split.append(jnp.concatenate([hi, lo], axis=1))
        sums = jnp.dot(jnp.concatenate(split, axis=0), tt_ref[...],
                       preferred_element_type=F32)
        new_cs, new_accs = [], []
        for g in range(SB_GROUP):
            s = sums[g * CHUNK:(g + 1) * CHUNK]
            after = s[:, :CHUNK] + cs[g]
            w = jnp.where(mask, jnp.exp(log_beta[g] + after), 0.0)
            new_accs.append(accs[g] + jnp.dot(w.astype(BF16), v[:, heads[g]],
                                              preferred_element_type=F32))
            new_cs.append(cs[g] + s[:, CHUNK:])
        return tuple(new_cs), tuple(new_accs)

    def least_decayed(cs):
        return jnp.max(functools.reduce(jnp.maximum, cs))

    def q_block(i, carry):
        start = pl.multiple_of(i * CHUNK, CHUNK)
        q = q_ref[pl.ds(start, CHUNK), :]
        qs = [q[:, heads[g]] for g in range(SB_GROUP)]
        zero = (jnp.zeros((CHUNK, CHUNK), F32),) * SB_GROUP
        diag_mask = jnp.logical_and(col < row, i * CHUNK + col >= N_PAD)
        cs, accs = step(qs, i, zero, zero, diag_mask)

        def cond(st):
            return jnp.logical_and(st[0] >= 0, st[1] > -SB_UNDERFLOW)

        def body(st):
            j = st[0]
            cs_, accs_ = step(qs, j, st[2:2 + SB_GROUP], st[2 + SB_GROUP:],
                              j * CHUNK + col >= N_PAD)
            return (j - 1, least_decayed(cs_)) + cs_ + accs_

        st = lax.while_loop(cond, body, (i - 1, least_decayed(cs)) + cs + accs)
        accs = st[2 + SB_GROUP:]
        for g in range(SB_GROUP):
            o_ref[pl.ds(start, CHUNK), heads[g]] = _rms(
                accs[g], nw_ref[:, heads[g]]).astype(BF16)
        return carry

    lax.fori_loop(0, nblk, q_block, 0)


def _stick_breaking(rest, nw, b, tp):
    m = rest.shape[0]
    nblk = tp // CHUNK
    width = SB_GROUP * SB_DH
    groups = SB_HEADS // SB_GROUP
    base = 2 * RET_WIDTH // width
    idx = jnp.arange(CHUNK)
    tri = (idx[:, None] > idx[None, :]).astype(BF16)
    tt = jnp.concatenate([tri, jnp.ones((CHUNK, CHUNK), BF16)], axis=1)
    tt = jnp.concatenate([tt, tt], axis=0)
    return pl.pallas_call(
        functools.partial(_sb_kernel, nblk=nblk),
        grid=(b, groups),
        in_specs=[
            pl.BlockSpec((tp, width), lambda i, h: (i, base + h),
                         pipeline_mode=pl.Buffered(1)),
            pl.BlockSpec((tp, width), lambda i, h: (i, base + groups + h),
                         pipeline_mode=pl.Buffered(1)),
            pl.BlockSpec((tp, width), lambda i, h: (i, base + 2 * groups + h),
                         pipeline_mode=pl.Buffered(1)),
            pl.BlockSpec((2 * CHUNK, 2 * CHUNK), lambda i, h: (0, 0)),
            pl.BlockSpec((1, width), lambda i, h: (0, h)),
        ],
        out_specs=pl.BlockSpec((tp, width), lambda i, h: (i, h)),
        out_shape=jax.ShapeDtypeStruct((m, SB_WIDTH), BF16),
        compiler_params=_params("parallel", "parallel"),
        name="stick_breaking",
    )(rest, rest, rest, tt, nw)


def _out_proj_kernel(r_ref, s_ref, w1_ref, w2_ref, o_ref):
    o_ref[...] = (jnp.dot(r_ref[...], w1_ref[...], preferred_element_type=F32)
                  + jnp.dot(s_ref[...], w2_ref[...], preferred_element_type=F32)
                  ).astype(o_ref.dtype)


def _out_proj(ret_out, sb_out, w_out, tp):
    m = ret_out.shape[0]
    d = w_out.shape[1]
    bm = _largest_divisor(tp, 1056, BF16_SUBLANES)
    bn = 1024
    return pl.pallas_call(
        _out_proj_kernel,
        grid=(m // bm, d // bn),
        in_specs=[
            pl.BlockSpec((bm, RET_WIDTH), lambda i, j: (i, 0)),
            pl.BlockSpec((bm, SB_WIDTH), lambda i, j: (i, 0)),
            pl.BlockSpec((RET_WIDTH, bn), lambda i, j: (0, j)),
            pl.BlockSpec((SB_WIDTH, bn), lambda i, j: (1, j)),
        ],
        out_specs=pl.BlockSpec((bm, bn), lambda i, j: (i, j)),
        out_shape=jax.ShapeDtypeStruct((m, d), BF16),
        compiler_params=_params("parallel", "arbitrary"),
        name="out_proj",
    )(ret_out, sb_out, w_out, w_out)


def _post_attn_kernel(a_ref, x_ref, meta_ref, w1_ref, w2_ref, h1_ref, hn_ref):
    h = _padded_rows(x_ref, meta_ref, pl.program_id(1))
    h1 = h + _rms(a_ref[...].astype(F32), w1_ref[...])
    h1_ref[...] = h1
    hn_ref[...] = _rms(h1, w2_ref[...]).astype(BF16)


def _post_attn(a, x, meta, w_post, w_pre):
    m, d = a.shape
    b = x.shape[0]
    nblk = m // (b * CHUNK)
    blk = pl.BlockSpec((CHUNK, d), lambda i, r: (i * nblk + r, 0))
    vec = pl.BlockSpec((1, d), lambda i, r: (0, 0))
    return pl.pallas_call(
        _post_attn_kernel,
        grid=(b, nblk),
        in_specs=[blk, _x_block_spec(d),
                  pl.BlockSpec((N_META, d), lambda i, r: (0, 0)), vec, vec],
        out_specs=[blk, blk],
        out_shape=[jax.ShapeDtypeStruct((m, d), F32),
                   jax.ShapeDtypeStruct((m, d), BF16)],
        compiler_params=_params("parallel", "arbitrary"),
        name="post_attn",
    )(a, x, meta, w_post, w_pre)


HALO = BF16_SUBLANES


FFN_SUB_ROWS = 352


def _ffn_up_kernel(x_ref, xh_ref, wg_ref, wu_ref, cw_ref, cb_ref, o_ref, g_ref):
    bm = x_ref.shape[0]
    sub = FFN_SUB_ROWS if bm % FFN_SUB_ROWS == 0 else bm
    wg = wg_ref[...]
    wu = wu_ref[...]
    g_ref[:HALO, :] = jnp.dot(xh_ref[...], wg, preferred_element_type=F32)
    for r in range(bm // sub):
        x = x_ref[r * sub:(r + 1) * sub, :]
        lo = HALO + r * sub
        g_ref[lo:lo + sub, :] = jnp.dot(x, wg, preferred_element_type=F32)
        up = jnp.dot(x, wu, preferred_element_type=F32)
        conv = cb_ref[...] + g_ref[lo - 2:lo - 2 + sub, :] * cw_ref[0:1, :]
        conv = conv + g_ref[lo - 1:lo - 1 + sub, :] * cw_ref[1:2, :]
        conv = conv + g_ref[lo:lo + sub, :] * cw_ref[2:3, :]
        o_ref[r * sub:(r + 1) * sub, :] = (conv * jax.nn.sigmoid(conv) * up).astype(BF16)


def _ffn_up(hn, w_up, conv_w, conv_b, tp):
    m, d = hn.shape
    dff = w_up.shape[1] // 2
    bm = _largest_divisor(tp, 2112, BF16_SUBLANES)
    bn = 256
    halo_blocks = bm // HALO
    up_base = dff // bn
    return pl.pallas_call(
        _ffn_up_kernel,
        grid=(m // bm, dff // bn),
        in_specs=[
            pl.BlockSpec((bm, d), lambda i, j: (i, 0), pipeline_mode=pl.Buffered(1)),
            pl.BlockSpec((HALO, d), lambda i, j: (jnp.maximum(i * halo_blocks - 1, 0), 0)),
            pl.BlockSpec((d, bn), lambda i, j: (0, j)),
            pl.BlockSpec((d, bn), lambda i, j: (0, j + up_base)),
            pl.BlockSpec((CONV_W, bn), lambda i, j: (0, j)),
            pl.BlockSpec((1, bn), lambda i, j: (0, j)),
        ],
        out_specs=pl.BlockSpec((bm, bn), lambda i, j: (i, j)),
        out_shape=jax.ShapeDtypeStruct((m, dff), BF16),
        scratch_shapes=[pltpu.VMEM((bm + HALO, bn), F32)],
        compiler_params=_params("parallel", "arbitrary"),
        name="ffn_up",
    )(hn, hn, w_up, w_up, conv_w, conv_b)


def _ffn_down(act, w_down, tp):
    m, dff = act.shape
    d = w_down.shape[1]
    bm = _largest_divisor(tp, 528, BF16_SUBLANES)
    bn = 512
    return pl.pallas_call(
        _proj_plain_kernel,
        grid=(m // bm, d // bn),
        in_specs=[
            pl.BlockSpec((bm, dff), lambda i, j: (i, 0)),
            pl.BlockSpec((dff, bn), lambda i, j: (0, j)),
        ],
        out_specs=pl.BlockSpec((bm, bn), lambda i, j: (i, j)),
        out_shape=jax.ShapeDtypeStruct((m, d), BF16),
        compiler_params=_params("parallel", "arbitrary"),
        name="ffn_down",
    )(act, w_down)


def _final_kernel(f_ref, h_ref, w_ref, o_ref):
    o_ref[0] = h_ref[...] + _rms(f_ref[...].astype(F32), w_ref[...])


def _final(f, h1, w, b, tp):
    m, d = f.shape
    nblk = tp // CHUNK
    seq = tp - CHUNK
    blk = pl.BlockSpec((CHUNK, d), lambda i, r: (i * nblk + r + 1, 0))
    return pl.pallas_call(
        _final_kernel,
        grid=(b, nblk - 1),
        in_specs=[blk, blk, pl.BlockSpec((1, d), lambda i, r: (0, 0))],
        out_specs=pl.BlockSpec((1, CHUNK, d), lambda i, r: (i, r, 0)),
        out_shape=jax.ShapeDtypeStruct((b, seq, d), F32),
        compiler_params=_params("parallel", "parallel"),
        name="final",
    )(f, h1, w)


def _rotary_tables(tp):
    half = RET_DK // 2
    pos = (jnp.arange(tp) - N_PAD).astype(F32)
    inv = ROPE_BASE ** (-jnp.arange(half, dtype=F32) / half)
    ang = pos[:, None] * inv[None, :]
    return jnp.cos(ang), jnp.sin(ang)


def kernel(x, meta_tokens, attn_pre_norm_w, w_in, ret_gn_w, sb_norm_w, w_out,
           attn_post_norm_w, ffn_pre_norm_w, w_up, conv_w, conv_b, w_down, ffn_post_norm_w):
    b, seq, d = x.shape
    assert seq % CHUNK == 0 and w_in.shape[0] == 1
    tp = seq + CHUNK
    m = b * tp

    hn = _prep(x, meta_tokens, attn_pre_norm_w).reshape(m, d)

    cos, sin = _rotary_tables(tp)
    qk, rest, (w_down_bf,), (w_up_bf,) = _in_proj(
        hn, w_in[0].astype(BF16), cos, sin, tp, [w_down[0]], [w_up[0]])
    ret_out = _retention(qk, rest, ret_gn_w, b, tp)
    sb_out = _stick_breaking(rest, sb_norm_w, b, tp)
    a = _out_proj(ret_out, sb_out, w_out[0].astype(BF16), tp)
    h1, hn2 = _post_attn(a, x, meta_tokens, attn_post_norm_w, ffn_pre_norm_w)

    act = _ffn_up(hn2, w_up_bf, conv_w[0], conv_b, tp)
    f = _ffn_down(act, w_down_bf, tp)
    return _final(f, h1, ffn_post_norm_w, b, tp)
```

```python
import functools
import math

import numpy as np
import jax
import jax.numpy as jnp
from jax import lax
from jax.experimental import pallas as pl
from jax.experimental.pallas import tpu as pltpu

N_META = 16
CHUNK = 128
N_PAD = CHUNK - N_META
RET_HEADS = 8
RET_DK = 256
RET_DV = 256
SB_HEADS = 16
SB_DH = 128
RET_WIDTH = RET_HEADS * RET_DV
RET_QK_WIDTH = RET_HEADS * RET_DK
SB_WIDTH = SB_HEADS * SB_DH
CONV_W = 3
EPS = 1e-6
ROPE_BASE = 10000.0

BF16_SUBLANES = 16
VMEM_LIMIT_BYTES = 58 * 1024 * 1024

F32 = jnp.float32
BF16 = jnp.bfloat16

_NT = (((1,), (1,)), ((), ()))
_TN = (((0,), (0,)), ((), ()))


def _params(*semantics):
    return pltpu.CompilerParams(dimension_semantics=semantics,
                                vmem_limit_bytes=VMEM_LIMIT_BYTES)


def _largest_divisor(n, target, multiple):
    best = None
    for d in range(multiple, min(n, target) + 1, multiple):
        if n % d == 0:
            best = d
    assert best is not None, (n, target, multiple)
    return best


def _rms(v, w):
    ms = jnp.mean(v * v, axis=-1, keepdims=True)
    return v * lax.rsqrt(ms + EPS) * w


def _padded_rows(x_ref, meta_ref, r):
    d = x_ref.shape[-1]
    first = jnp.concatenate([jnp.zeros((N_PAD, d), F32), meta_ref[...]], axis=0)
    return jnp.where(r == 0, first, x_ref[0])


def _x_block_spec(d):
    return pl.BlockSpec((1, CHUNK, d), lambda i, r: (i, jnp.maximum(r - 1, 0), 0))


def _prep_kernel(x_ref, meta_ref, w_ref, hn_ref):
    h = _padded_rows(x_ref, meta_ref, pl.program_id(1))
    hn_ref[0] = _rms(h, w_ref[...]).astype(BF16)


def _prep(x, meta, w):
    b, seq, d = x.shape
    nblk = seq // CHUNK + 1
    tp = nblk * CHUNK
    return pl.pallas_call(
        _prep_kernel,
        grid=(b, nblk),
        in_specs=[
            _x_block_spec(d),
            pl.BlockSpec((N_META, d), lambda i, r: (0, 0)),
            pl.BlockSpec((1, d), lambda i, r: (0, 0)),
        ],
        out_specs=pl.BlockSpec((1, CHUNK, d), lambda i, r: (i, r, 0)),
        out_shape=jax.ShapeDtypeStruct((b, tp, d), BF16),
        compiler_params=_params("parallel", "arbitrary"),
        name="prep",
    )(x, meta, w)


CAST_BLOCK_BYTES = 3 * 1024 * 1024


def _cast_plan(rows, cols, steps):
    best = None
    for br in range(BF16_SUBLANES, rows + 1, BF16_SUBLANES):
        if rows % br:
            continue
        for bc in range(128, cols + 1, 128):
            if cols % bc or br * bc * 4 > CAST_BLOCK_BYTES:
                continue
            if (rows // br) * (cols // bc) <= steps and (best is None or br * bc < best[0] * best[1]):
                best = (br, bc)
    return best


def _with_casts(kernel_fn, n_in, n_out, weights, steps, nj):
    plans, in_specs, out_specs, out_shapes, operands = [], [], [], [], []
    start = 0
    for w in weights:
        _, rows, cols = w.shape
        plan = _cast_plan(rows, cols, steps - start)
        plans.append(plan)
        if plan is None:
            continue
        br, bc = plan
        ncb = cols // bc
        nb = (rows // br) * ncb

        def block(i, j, start=start, nb=nb, ncb=ncb):
            t = jnp.clip(i * nj + j - start, 0, nb - 1)
            return t // ncb, t % ncb

        in_specs.append(pl.BlockSpec((None, br, bc), lambda i, j, block=block: (0,) + block(i, j)))
        out_specs.append(pl.BlockSpec((br, bc), block))
        out_shapes.append(jax.ShapeDtypeStruct((rows, cols), BF16))
        operands.append(w)
        start += nb
    n_cast = len(operands)

    def wrapped(*refs):
        ins, rest = refs[:n_in], refs[n_in:]
        cast_in, rest = rest[:n_cast], rest[n_cast:]
        outs, cast_out = rest[:n_out], rest[n_out:]
        kernel_fn(*ins, *outs)
        for ci, co in zip(cast_in, cast_out):
            co[...] = ci[...].astype(BF16)

    def finish(cast_results):
        it = iter(cast_results)
        return [w[0].astype(BF16) if p is None else next(it)
                for w, p in zip(weights, plans)]

    return wrapped, in_specs, out_specs, out_shapes, operands, finish


def _proj_rot_kernel(x_ref, w_ref, cos_ref, sin_ref, o_ref, *, n_q_blocks):
    j = pl.program_id(1)
    acc = jnp.dot(x_ref[...], w_ref[...], preferred_element_type=F32)
    scale = jnp.where(j < n_q_blocks, RET_DK ** -0.5, 1.0).astype(F32)
    cos = cos_ref[...]
    sin = sin_ref[...]
    half = RET_DK // 2
    for h in range(o_ref.shape[1] // RET_DK):
        lo = h * RET_DK
        x1 = acc[:, lo:lo + half]
        x2 = acc[:, lo + half:lo + RET_DK]
        o_ref[:, lo:lo + half] = ((x1 * cos - x2 * sin) * scale).astype(BF16)
        o_ref[:, lo + half:lo + RET_DK] = ((x1 * sin + x2 * cos) * scale).astype(BF16)


def _proj_plain_kernel(x_ref, w_ref, o_ref):
    o_ref[...] = jnp.dot(x_ref[...], w_ref[...],
                         preferred_element_type=F32).astype(o_ref.dtype)


def _in_proj(hn, w_in, cos, sin, tp, later_weights_rot, later_weights_rest):
    m, d = hn.shape
    bm = _largest_divisor(tp, 1056, BF16_SUBLANES)
    bn = 1024
    pos_blocks = tp // bm
    n_rot = 2 * RET_QK_WIDTH
    n_rest = w_in.shape[1] - n_rot

    grid = (m // bm, n_rot // bn)
    body, c_in, c_out, c_shape, c_ops, finish_rot = _with_casts(
        functools.partial(_proj_rot_kernel, n_q_blocks=RET_QK_WIDTH // bn),
        4, 1, later_weights_rot, grid[0] * grid[1], grid[1])
    qk, *cast_rot = pl.pallas_call(
        body,
        grid=grid,
        in_specs=[
            pl.BlockSpec((bm, d), lambda i, j: (i, 0)),
            pl.BlockSpec((d, bn), lambda i, j: (0, j)),
            pl.BlockSpec((bm, RET_DK // 2), lambda i, j: (i % pos_blocks, 0)),
            pl.BlockSpec((bm, RET_DK // 2), lambda i, j: (i % pos_blocks, 0)),
        ] + c_in,
        out_specs=[pl.BlockSpec((bm, bn), lambda i, j: (i, j))] + c_out,
        out_shape=[jax.ShapeDtypeStruct((m, n_rot), BF16)] + c_shape,
        compiler_params=_params("arbitrary", "arbitrary"),
        name="in_proj_rot",
    )(hn, w_in, cos, sin, *c_ops)

    rot_blocks = n_rot // bn
    grid = (m // bm, n_rest // bn)
    body, c_in, c_out, c_shape, c_ops, finish_rest = _with_casts(
        _proj_plain_kernel, 2, 1, later_weights_rest, grid[0] * grid[1], grid[1])
    rest, *cast_rest = pl.pallas_call(
        body,
        grid=grid,
        in_specs=[
            pl.BlockSpec((bm, d), lambda i, j: (i, 0)),
            pl.BlockSpec((d, bn), lambda i, j: (0, j + rot_blocks)),
        ] + c_in,
        out_specs=[pl.BlockSpec((bm, bn), lambda i, j: (i, j))] + c_out,
        out_shape=[jax.ShapeDtypeStruct((m, n_rest), BF16)] + c_shape,
        compiler_params=_params("arbitrary", "arbitrary"),
        name="in_proj_rest",
    )(hn, w_in, *c_ops)
    return qk, rest, finish_rot(cast_rot), finish_rest(cast_rest)


def _ret_log_gamma():
    return np.log(1.0 - 2.0 ** (-5.0 - np.arange(RET_HEADS, dtype=np.float64)))


def _ret_kernel(q_ref, k_ref, v_ref, g_ref, intra_ref, qd_ref, kd_ref, gn_ref,
                o_ref, state_ref):
    c = pl.program_id(1)

    @pl.when(c == 0)
    def _():
        state_ref[...] = jnp.zeros(state_ref.shape, F32)

    chunk_decay = np.exp(CHUNK * _ret_log_gamma()).astype(np.float32)
    for h in range(RET_HEADS):
        sl = slice(h * RET_DK, (h + 1) * RET_DK)
        q = q_ref[:, sl]
        k = k_ref[:, sl]
        v = v_ref[:, sl]
        s = lax.dot_general(q, k, _NT, preferred_element_type=F32) * intra_ref[h]
        intra = jnp.dot(s.astype(BF16), v, preferred_element_type=F32)
        st = state_ref[h]
        qd = qd_ref[h]
        kd = kd_ref[h]
        cross = jnp.dot(q, st.astype(BF16), preferred_element_type=F32)
        cross = cross * jnp.concatenate([qd, qd], axis=1)
        kdk = (k.astype(F32) * jnp.concatenate([kd, kd], axis=1)).astype(BF16)
        upd = lax.dot_general(kdk, v, _TN, preferred_element_type=F32)
        state_ref[h] = st * float(chunk_decay[h]) + upd
        y = intra + cross
        yn = _rms(y, gn_ref[:, sl])
        g = g_ref[:, sl].astype(F32)
        o_ref[:, sl] = (g * jax.nn.sigmoid(g) * yn).astype(BF16)


def _retention(qk, rest, gn_w, b, tp):
    m = qk.shape[0]
    nblk = tp // CHUNK
    lg = jnp.log(1.0 - 2.0 ** (-5.0 - jnp.arange(RET_HEADS, dtype=F32)))
    idx = jnp.arange(CHUNK, dtype=F32)
    diff = idx[:, None] - idx[None, :]
    intra = jnp.where(diff[None] >= 0,
                      jnp.exp(jnp.maximum(diff, 0.0)[None] * lg[:, None, None]), 0.0)
    qd = jnp.exp((idx[None, :] + 1.0) * lg[:, None])
    kd = jnp.exp((CHUNK - 1.0 - idx[None, :]) * lg[:, None])
    qd = jnp.broadcast_to(qd[:, :, None], (RET_HEADS, CHUNK, CHUNK))
    kd = jnp.broadcast_to(kd[:, :, None], (RET_HEADS, CHUNK, CHUNK))
    row = lambda i, c: (i * nblk + c)
    table = pl.BlockSpec((RET_HEADS, CHUNK, CHUNK), lambda i, c: (0, 0, 0))
    return pl.pallas_call(
        _ret_kernel,
        grid=(b, nblk),
        in_specs=[
            pl.BlockSpec((CHUNK, RET_QK_WIDTH), lambda i, c: (row(i, c), 0)),
            pl.BlockSpec((CHUNK, RET_QK_WIDTH), lambda i, c: (row(i, c), 1)),
            pl.BlockSpec((CHUNK, RET_WIDTH), lambda i, c: (row(i, c), 0)),
            pl.BlockSpec((CHUNK, RET_WIDTH), lambda i, c: (row(i, c), 1)),
            table, table, table,
            pl.BlockSpec((1, RET_WIDTH), lambda i, c: (0, 0)),
        ],
        out_specs=pl.BlockSpec((CHUNK, RET_WIDTH), lambda i, c: (row(i, c), 0)),
        out_shape=jax.ShapeDtypeStruct((m, RET_WIDTH), BF16),
        scratch_shapes=[pltpu.VMEM((RET_HEADS, RET_DK, RET_DV), F32)],
        compiler_params=_params("parallel", "arbitrary"),
        name="retention",
    )(qk, qk, rest, rest, intra, qd, kd, gn_w)


SB_GROUP = 8
SB_UNDERFLOW = 110.0


def _sb_kernel(q_ref, k_ref, v_ref, tt_ref, nw_ref, o_ref, *, nblk):
    scale = 1.0 / math.sqrt(SB_DH)
    row = lax.broadcasted_iota(jnp.int32, (CHUNK, CHUNK), 0)
    col = lax.broadcasted_iota(jnp.int32, (CHUNK, CHUNK), 1)
    heads = [slice(g * SB_DH, (g + 1) * SB_DH) for g in range(SB_GROUP)]

    def step(qs, j, cs, accs, mask):
        start = pl.multiple_of(j * CHUNK, CHUNK)
        k = k_ref[pl.ds(start, CHUNK), :]
        v = v_ref[pl.ds(start, CHUNK), :]
        log_beta, split = [], []
        for g in range(SB_GROUP):
            z = lax.dot_general(qs[g], k[:, heads[g]], _NT,
                                preferred_element_type=F32) * scale
            sp = jnp.maximum(z, 0.0) + jnp.log(1.0 + jnp.exp(-jnp.abs(z)))
            log_beta.append(z - sp)
            log_keep = jnp.where(mask, -sp, 0.0)
            hi = log_keep.astype(BF16)
            lo = (log_keep - hi.astype(F32)).astype(BF16)
            split.append(jnp.concatenate([hi, lo], axis=1))
        sums = jnp.dot(jnp.concatenate(split, axis=0), tt_ref[...],
                       preferred_element_type=F32)
        new_cs, new_accs = [], []
        for g in range(SB_GROUP):
            s = sums[g * CHUNK:(g + 1) * CHUNK]
            after = s[:, :CHUNK] + cs[g]
            w = jnp.where(mask, jnp.exp(log_beta[g] + after), 0.0)
            new_accs.append(accs[g] + jnp.dot(w.astype(BF16), v[:, heads[g]],
                                              preferred_element_type=F32))
            new_cs.append(cs[g] + s[:, CHUNK:])
        return tuple(new_cs), tuple(new_accs)

    def least_decayed(cs):
        return jnp.max(functools.reduce(jnp.maximum, cs))

    def q_block(i, carry):
        start = pl.multiple_of(i * CHUNK, CHUNK)
        q = q_ref[pl.ds(start, CHUNK), :]
        qs = [q[:, heads[g]] for g in range(SB_GROUP)]
        zero = (jnp.zeros((CHUNK, CHUNK), F32),) * SB_GROUP
        diag_mask = jnp.logical_and(col < row, i * CHUNK + col >= N_PAD)
        cs, accs = step(qs, i, zero, zero, diag_mask)

        def cond(st):
            return jnp.logical_and(st[0] >= 0, st[1] > -SB_UNDERFLOW)

        def body(st):
            j = st[0]
            cs_, accs_ = step(qs, j, st[2:2 + SB_GROUP], st[2 + SB_GROUP:],
                              j * CHUNK + col >= N_PAD)
            return (j - 1, least_decayed(cs_)) + cs_ + accs_

        st = lax.while_loop(cond, body, (i - 1, least_decayed(cs)) + cs + accs)
        accs = st[2 + SB_GROUP:]
        for g in range(SB_GROUP):
            o_ref[pl.ds(start, CHUNK), heads[g]] = _rms(
                accs[g], nw_ref[:, heads[g]]).astype(BF16)
        return carry

    lax.fori_loop(0, nblk, q_block, 0)


def _stick_breaking(rest, nw, b, tp):
    m = rest.shape[0]
    nblk = tp // CHUNK
    width = SB_GROUP * SB_DH
    groups = SB_HEADS // SB_GROUP
    base = 2 * RET_WIDTH // width
    idx = jnp.arange(CHUNK)
    tri = (idx[:, None] > idx[None, :]).astype(BF16)
    tt = jnp.concatenate([tri, jnp.ones((CHUNK, CHUNK), BF16)], axis=1)
    tt = jnp.concatenate([tt, tt], axis=0)
    return pl.pallas_call(
        functools.partial(_sb_kernel, nblk=nblk),
        grid=(b, groups),
        in_specs=[
            pl.BlockSpec((tp, width), lambda i, h: (i, base + h),
                         pipeline_mode=pl.Buffered(1)),
            pl.BlockSpec((tp, width), lambda i, h: (i, base + groups + h),
                         pipeline_mode=pl.Buffered(1)),
            pl.BlockSpec((tp, width), lambda i, h: (i, base + 2 * groups + h),
                         pipeline_mode=pl.Buffered(1)),
            pl.BlockSpec((2 * CHUNK, 2 * CHUNK), lambda i, h: (0, 0)),
            pl.BlockSpec((1, width), lambda i, h: (0, h)),
        ],
        out_specs=pl.BlockSpec((tp, width), lambda i, h: (i, h)),
        out_shape=jax.ShapeDtypeStruct((m, SB_WIDTH), BF16),
        compiler_params=_params("parallel", "parallel"),
        name="stick_breaking",
    )(rest, rest, rest, tt, nw)


def _out_proj_kernel(r_ref, s_ref, w1_ref, w2_ref, o_ref):
    o_ref[...] = (jnp.dot(r_ref[...], w1_ref[...], preferred_element_type=F32)
                  + jnp.dot(s_ref[...], w2_ref[...], preferred_element_type=F32)
                  ).astype(o_ref.dtype)


def _out_proj(ret_out, sb_out, w_out, tp):
    m = ret_out.shape[0]
    d = w_out.shape[1]
    bm = _largest_divisor(tp, 1056, BF16_SUBLANES)
    bn = 1024
    return pl.pallas_call(
        _out_proj_kernel,
        grid=(m // bm, d // bn),
        in_specs=[
            pl.BlockSpec((bm, RET_WIDTH), lambda i, j: (i, 0)),
            pl.BlockSpec((bm, SB_WIDTH), lambda i, j: (i, 0)),
            pl.BlockSpec((RET_WIDTH, bn), lambda i, j: (0, j)),
            pl.BlockSpec((SB_WIDTH, bn), lambda i, j: (1, j)),
        ],
        out_specs=pl.BlockSpec((bm, bn), lambda i, j: (i, j)),
        out_shape=jax.ShapeDtypeStruct((m, d), BF16),
        compiler_params=_params("parallel", "arbitrary"),
        name="out_proj",
    )(ret_out, sb_out, w_out, w_out)


def _post_attn_kernel(a_ref, x_ref, meta_ref, w1_ref, w2_ref, h1_ref, hn_ref):
    h = _padded_rows(x_ref, meta_ref, pl.program_id(1))
    h1 = h + _rms(a_ref[...].astype(F32), w1_ref[...])
    h1_ref[...] = h1
    hn_ref[...] = _rms(h1, w2_ref[...]).astype(BF16)


def _post_attn(a, x, meta, w_post, w_pre):
    m, d = a.shape
    b = x.shape[0]
    nblk = m // (b * CHUNK)
    blk = pl.BlockSpec((CHUNK, d), lambda i, r: (i * nblk + r, 0))
    vec = pl.BlockSpec((1, d), lambda i, r: (0, 0))
    return pl.pallas_call(
        _post_attn_kernel,
        grid=(b, nblk),
        in_specs=[blk, _x_block_spec(d),
                  pl.BlockSpec((N_META, d), lambda i, r: (0, 0)), vec, vec],
        out_specs=[blk, blk],
        out_shape=[jax.ShapeDtypeStruct((m, d), F32),
                   jax.ShapeDtypeStruct((m, d), BF16)],
        compiler_params=_params("parallel", "arbitrary"),
        name="post_attn",
    )(a, x, meta, w_post, w_pre)


HALO = BF16_SUBLANES


FFN_SUB_ROWS = 352


def _ffn_up_kernel(x_ref, xh_ref, wg_ref, wu_ref, cw_ref, cb_ref, o_ref, g_ref):
    bm = x_ref.shape[0]
    sub = FFN_SUB_ROWS if bm % FFN_SUB_ROWS == 0 else bm
    wg = wg_ref[...]
    wu = wu_ref[...]
    g_ref[:HALO, :] = jnp.dot(xh_ref[...], wg, preferred_element_type=F32)
    for r in range(bm // sub):
        x = x_ref[r * sub:(r + 1) * sub, :]
        lo = HALO + r * sub
        g_ref[lo:lo + sub, :] = jnp.dot(x, wg, preferred_element_type=F32)
        up = jnp.dot(x, wu, preferred_element_type=F32)
        conv = cb_ref[...] + g_ref[lo - 2:lo - 2 + sub, :] * cw_ref[0:1, :]
        conv = conv + g_ref[lo - 1:lo - 1 + sub, :] * cw_ref[1:2, :]
        conv = conv + g_ref[lo:lo + sub, :] * cw_ref[2:3, :]
        o_ref[r * sub:(r + 1) * sub, :] = (conv * jax.nn.sigmoid(conv) * up).astype(BF16)


def _ffn_up(hn, w_up, conv_w, conv_b, tp):
    m, d = hn.shape
    dff = w_up.shape[1] // 2
    bm = _largest_divisor(tp, 2112, BF16_SUBLANES)
    bn = 256
    halo_blocks = bm // HALO
    up_base = dff // bn
    return pl.pallas_call(
        _ffn_up_kernel,
        grid=(m // bm, dff // bn),
        in_specs=[
            pl.BlockSpec((bm, d), lambda i, j: (i, 0), pipeline_mode=pl.Buffered(1)),
            pl.BlockSpec((HALO, d), lambda i, j: (jnp.maximum(i * halo_blocks - 1, 0), 0)),
            pl.BlockSpec((d, bn), lambda i, j: (0, j)),
            pl.BlockSpec((d, bn), lambda i, j: (0, j + up_base)),
            pl.BlockSpec((CONV_W, bn), lambda i, j: (0, j)),
            pl.BlockSpec((1, bn), lambda i, j: (0, j)),
        ],
        out_specs=pl.BlockSpec((bm, bn), lambda i, j: (i, j)),
        out_shape=jax.ShapeDtypeStruct((m, dff), BF16),
        scratch_shapes=[pltpu.VMEM((bm + HALO, bn), F32)],
        compiler_params=_params("parallel", "arbitrary"),
        name="ffn_up",
    )(hn, hn, w_up, w_up, conv_w, conv_b)


def _ffn_down(act, w_down, tp):
    m, dff = act.shape
    d = w_down.shape[1]
    bm = _largest_divisor(tp, 528, BF16_SUBLANES)
    bn = 512
    return pl.pallas_call(
        _proj_plain_kernel,
        grid=(m // bm, d // bn),
        in_specs=[
            pl.BlockSpec((bm, dff), lambda i, j: (i, 0)),
            pl.BlockSpec((dff, bn), lambda i, j: (0, j)),
        ],
        out_specs=pl.BlockSpec((bm, bn), lambda i, j: (i, j)),
        out_shape=jax.ShapeDtypeStruct((m, d), BF16),
        compiler_params=_params("parallel", "arbitrary"),
        name="ffn_down",
    )(act, w_down)


def _final_kernel(f_ref, h_ref, w_ref, o_ref):
    o_ref[0] = h_ref[...] + _rms(f_ref[...].astype(F32), w_ref[...])


def _final(f, h1, w, b, tp):
    m, d = f.shape
    nblk = tp // CHUNK
    seq = tp - CHUNK
    blk = pl.BlockSpec((CHUNK, d), lambda i, r: (i * nblk + r + 1, 0))
    return pl.pallas_call(
        _final_kernel,
        grid=(b, nblk - 1),
        in_specs=[blk, blk, pl.BlockSpec((1, d), lambda i, r: (0, 0))],
        out_specs=pl.BlockSpec((1, CHUNK, d), lambda i, r: (i, r, 0)),
        out_shape=jax.ShapeDtypeStruct((b, seq, d), F32),
        compiler_params=_params("parallel", "parallel"),
        name="final",
    )(f, h1, w)


def _rotary_tables(tp):
    half = RET_DK // 2
    pos = (jnp.arange(tp) - N_PAD).astype(F32)
    inv = ROPE_BASE ** (-jnp.arange(half, dtype=F32) / half)
    ang = pos[:, None] * inv[None, :]
    return jnp.cos(ang), jnp.sin(ang)


def kernel(x, meta_tokens, attn_pre_norm_w, w_in, ret_gn_w, sb_norm_w, w_out,
           attn_post_norm_w, ffn_pre_norm_w, w_up, conv_w, conv_b, w_down, ffn_post_norm_w):
    b, seq, d = x.shape
    assert seq % CHUNK == 0 and w_in.shape[0] == 1
    tp = seq + CHUNK
    m = b * tp

    hn = _prep(x, meta_tokens, attn_pre_norm_w).reshape(m, d)

    cos, sin = _rotary_tables(tp)
    qk, rest, (w_down_bf,), (w_up_bf,) = _in_proj(
        hn, w_in[0].astype(BF16), cos, sin, tp, [w_down], [w_up])
    ret_out = _retention(qk, rest, ret_gn_w, b, tp)
    sb_out = _stick_breaking(rest, sb_norm_w, b, tp)
    a = _out_proj(ret_out, sb_out, w_out[0].astype(BF16), tp)
    h1, hn2 = _post_attn(a, x, meta_tokens, attn_post_norm_w, ffn_pre_norm_w)

    act = _ffn_up(hn2, w_up_bf, conv_w[0], conv_b, tp)
    f = _ffn_down(act, w_down_bf, tp)
    return _final(f, h1, ffn_post_norm_w, b, tp)
```

```python
import functools
import math

import numpy as np
import jax
import jax.numpy as jnp
from jax import lax
from jax.experimental import pallas as pl
from jax.experimental.pallas import tpu as pltpu

N_META = 16
CHUNK = 128
N_PAD = CHUNK - N_META
RET_HEADS = 8
RET_DK = 256
RET_DV = 256
SB_HEADS = 16
SB_DH = 128
RET_WIDTH = RET_HEADS * RET_DV
RET_QK_WIDTH = RET_HEADS * RET_DK
SB_WIDTH = SB_HEADS * SB_DH
CONV_W = 3
EPS = 1e-6
ROPE_BASE = 10000.0

BF16_SUBLANES = 16
VMEM_LIMIT_BYTES = 58 * 1024 * 1024

F32 = jnp.float32
BF16 = jnp.bfloat16

_NT = (((1,), (1,)), ((), ()))
_TN = (((0,), (0,)), ((), ()))


def _params(*semantics):
    return pltpu.CompilerParams(dimension_semantics=semantics,
                                vmem_limit_bytes=VMEM_LIMIT_BYTES)


def _largest_divisor(n, target, multiple):
    best = None
    for d in range(multiple, min(n, target) + 1, multiple):
        if n % d == 0:
            best = d
    assert best is not None, (n, target, multiple)
    return best


def _rms(v, w):
    ms = jnp.mean(v * v, axis=-1, keepdims=True)
    return v * lax.rsqrt(ms + EPS) * w


def _padded_rows(x_ref, meta_ref, r):
    d = x_ref.shape[-1]
    first = jnp.concatenate([jnp.zeros((N_PAD, d), F32), meta_ref[...]], axis=0)
    return jnp.where(r == 0, first, x_ref[0])


def _x_block_spec(d):
    return pl.BlockSpec((1, CHUNK, d), lambda i, r: (i, jnp.maximum(r - 1, 0), 0))


def _prep_kernel(x_ref, meta_ref, w_ref, hn_ref):
    h = _padded_rows(x_ref, meta_ref, pl.program_id(1))
    hn_ref[0] = _rms(h, w_ref[...]).astype(BF16)


def _prep(x, meta, w):
    b, seq, d = x.shape
    nblk = seq // CHUNK + 1
    tp = nblk * CHUNK
    return pl.pallas_call(
        _prep_kernel,
        grid=(b, nblk),
        in_specs=[
            _x_block_spec(d),
            pl.BlockSpec((N_META, d), lambda i, r: (0, 0)),
            pl.BlockSpec((1, d), lambda i, r: (0, 0)),
        ],
        out_specs=pl.BlockSpec((1, CHUNK, d), lambda i, r: (i, r, 0)),
        out_shape=jax.ShapeDtypeStruct((b, tp, d), BF16),
        compiler_params=_params("parallel", "arbitrary"),
        name="prep",
    )(x, meta, w)


CAST_BLOCK_BYTES = 3 * 1024 * 1024


def _cast_plan(rows, cols, steps):
    best = None
    for br in range(BF16_SUBLANES, rows + 1, BF16_SUBLANES):
        if rows % br:
            continue
        for bc in range(128, cols + 1, 128):
            if cols % bc or br * bc * 4 > CAST_BLOCK_BYTES:
                continue
            if (rows // br) * (cols // bc) <= steps and (best is None or br * bc < best[0] * best[1]):
                best = (br, bc)
    return best


def _with_casts(kernel_fn, n_in, n_out, weights, steps, nj):
    plans, in_specs, out_specs, out_shapes, operands = [], [], [], [], []
    start = 0
    for w in weights:
        _, rows, cols = w.shape
        plan = _cast_plan(rows, cols, steps - start)
        plans.append(plan)
        if plan is None:
            continue
        br, bc = plan
        ncb = cols // bc
        nb = (rows // br) * ncb

        def block(i, j, start=start, nb=nb, ncb=ncb):
            t = jnp.clip(i * nj + j - start, 0, nb - 1)
            return t // ncb, t % ncb

        in_specs.append(pl.BlockSpec((None, br, bc), lambda i, j, block=block: (0,) + block(i, j)))
        out_specs.append(pl.BlockSpec((br, bc), block))
        out_shapes.append(jax.ShapeDtypeStruct((rows, cols), BF16))
        operands.append(w)
        start += nb
    n_cast = len(operands)

    def wrapped(*refs):
        ins, rest = refs[:n_in], refs[n_in:]
        cast_in, rest = rest[:n_cast], rest[n_cast:]
        outs, cast_out = rest[:n_out], rest[n_out:]
        kernel_fn(*ins, *outs)
        for ci, co in zip(cast_in, cast_out):
            co[...] = ci[...].astype(BF16)

    def finish(cast_results):
        it = iter(cast_results)
        return [w[0].astype(BF16) if p is None else next(it)
                for w, p in zip(weights, plans)]

    return wrapped, in_specs, out_specs, out_shapes, operands, finish


def _proj_rot_kernel(x_ref, w_ref, cos_ref, sin_ref, o_ref, *, n_q_blocks):
    j = pl.program_id(1)
    acc = jnp.dot(x_ref[...], w_ref[...], preferred_element_type=F32)
    scale = jnp.where(j < n_q_blocks, RET_DK ** -0.5, 1.0).astype(F32)
    cos = cos_ref[...]
    sin = sin_ref[...]
    half = RET_DK // 2
    for h in range(o_ref.shape[1] // RET_DK):
        lo = h * RET_DK
        x1 = acc[:, lo:lo + half]
        x2 = acc[:, lo + half:lo + RET_DK]
        o_ref[:, lo:lo + half] = ((x1 * cos - x2 * sin) * scale).astype(BF16)
        o_ref[:, lo + half:lo + RET_DK] = ((x1 * sin + x2 * cos) * scale).astype(BF16)


def _proj_plain_kernel(x_ref, w_ref, o_ref):
    o_ref[...] = jnp.dot(x_ref[...], w_ref[...],
                         preferred_element_type=F32).astype(o_ref.dtype)


def _in_proj(hn, w_in, cos, sin, tp, later_weights_rot, later_weights_rest):
    m, d = hn.shape
    bm = _largest_divisor(tp, 1056, BF16_SUBLANES)
    bn = 1024
    pos_blocks = tp // bm
    n_rot = 2 * RET_QK_WIDTH
    n_rest = w_in.shape[1] - n_rot

    grid = (m // bm, n_rot // bn)
    body, c_in, c_out, c_shape, c_ops, finish_rot = _with_casts(
        functools.partial(_proj_rot_kernel, n_q_blocks=RET_QK_WIDTH // bn),
        4, 1, later_weights_rot, grid[0] * grid[1], grid[1])
    qk, *cast_rot = pl.pallas_call(
        body,
        grid=grid,
        in_specs=[
            pl.BlockSpec((bm, d), lambda i, j: (i, 0)),
            pl.BlockSpec((d, bn), lambda i, j: (0, j)),
            pl.BlockSpec((bm, RET_DK // 2), lambda i, j: (i % pos_blocks, 0)),
            pl.BlockSpec((bm, RET_DK // 2), lambda i, j: (i % pos_blocks, 0)),
        ] + c_in,
        out_specs=[pl.BlockSpec((bm, bn), lambda i, j: (i, j))] + c_out,
        out_shape=[jax.ShapeDtypeStruct((m, n_rot), BF16)] + c_shape,
        compiler_params=_params("arbitrary", "arbitrary"),
        name="in_proj_rot",
    )(hn, w_in, cos, sin, *c_ops)

    rot_blocks = n_rot // bn
    grid = (m // bm, n_rest // bn)
    body, c_in, c_out, c_shape, c_ops, finish_rest = _with_casts(
        _proj_plain_kernel, 2, 1, later_weights_rest, grid[0] * grid[1], grid[1])
    rest, *cast_rest = pl.pallas_call(
        body,
        grid=grid,
        in_specs=[
            pl.BlockSpec((bm, d), lambda i, j: (i, 0)),
            pl.BlockSpec((d, bn), lambda i, j: (0, j + rot_blocks)),
        ] + c_in,
        out_specs=[pl.BlockSpec((bm, bn), lambda i, j: (i, j))] + c_out,
        out_shape=[jax.ShapeDtypeStruct((m, n_rest), BF16)] + c_shape,
        compiler_params=_params("arbitrary", "arbitrary"),
        name="in_proj_rest",
    )(hn, w_in, *c_ops)
    return qk, rest, finish_rot(cast_rot), finish_rest(cast_rest)


def _ret_log_gamma():
    return np.log(1.0 - 2.0 ** (-5.0 - np.arange(RET_HEADS, dtype=np.float64)))


def _ret_kernel(q_ref, k_ref, v_ref, g_ref, intra_ref, qd_ref, kd_ref, gn_ref,
                o_ref, state_ref):
    c = pl.program_id(1)

    @pl.when(c == 0)
    def _():
        state_ref[...] = jnp.zeros(state_ref.shape, F32)

    chunk_decay = np.exp(CHUNK * _ret_log_gamma()).astype(np.float32)
    for h in range(RET_HEADS):
        sl = slice(h * RET_DK, (h + 1) * RET_DK)
        q = q_ref[:, sl]
        k = k_ref[:, sl]
        v = v_ref[:, sl]
        s = lax.dot_general(q, k, _NT, preferred_element_type=F32) * intra_ref[h]
        intra = jnp.dot(s.astype(BF16), v, preferred_element_type=F32)
        st = state_ref[h]
        qd = qd_ref[h]
        kd = kd_ref[h]
        cross = jnp.dot(q, st.astype(BF16), preferred_element_type=F32)
        cross = cross * jnp.concatenate([qd, qd], axis=1)
        kdk = (k.astype(F32) * jnp.concatenate([kd, kd], axis=1)).astype(BF16)
        upd = lax.dot_general(kdk, v, _TN, preferred_element_type=F32)
        state_ref[h] = st * float(chunk_decay[h]) + upd
        y = intra + cross
        yn = _rms(y, gn_ref[:, sl])
        g = g_ref[:, sl].astype(F32)
        o_ref[:, sl] = (g * jax.nn.sigmoid(g) * yn).astype(BF16)


def _retention(qk, rest, gn_w, b, tp):
    m = qk.shape[0]
    nblk = tp // CHUNK
    lg = jnp.log(1.0 - 2.0 ** (-5.0 - jnp.arange(RET_HEADS, dtype=F32)))
    idx = jnp.arange(CHUNK, dtype=F32)
    diff = idx[:, None] - idx[None, :]
    intra = jnp.where(diff[None] >= 0,
                      jnp.exp(jnp.maximum(diff, 0.0)[None] * lg[:, None, None]), 0.0)
    qd = jnp.exp((idx[None, :] + 1.0) * lg[:, None])
    kd = jnp.exp((CHUNK - 1.0 - idx[None, :]) * lg[:, None])
    qd = jnp.broadcast_to(qd[:, :, None], (RET_HEADS, CHUNK, CHUNK))
    kd = jnp.broadcast_to(kd[:, :, None], (RET_HEADS, CHUNK, CHUNK))
    row = lambda i, c: (i * nblk + c)
    table = pl.BlockSpec((RET_HEADS, CHUNK, CHUNK), lambda i, c: (0, 0, 0))
    return pl.pallas_call(
        _ret_kernel,
        grid=(b, nblk),
        in_specs=[
            pl.BlockSpec((CHUNK, RET_QK_WIDTH), lambda i, c: (row(i, c), 0)),
            pl.BlockSpec((CHUNK, RET_QK_WIDTH), lambda i, c: (row(i, c), 1)),
            pl.BlockSpec((CHUNK, RET_WIDTH), lambda i, c: (row(i, c), 0)),
            pl.BlockSpec((CHUNK, RET_WIDTH), lambda i, c: (row(i, c), 1)),
            table, table, table,
            pl.BlockSpec((1, RET_WIDTH), lambda i, c: (0, 0)),
        ],
        out_specs=pl.BlockSpec((CHUNK, RET_WIDTH), lambda i, c: (row(i, c), 0)),
        out_shape=jax.ShapeDtypeStruct((m, RET_WIDTH), BF16),
        scratch_shapes=[pltpu.VMEM((RET_HEADS, RET_DK, RET_DV), F32)],
        compiler_params=_params("parallel", "arbitrary"),
        name="retention",
    )(qk, qk, rest, rest, intra, qd, kd, gn_w)


SB_GROUP = 8
SB_UNDERFLOW = 110.0
SB_FIRST_TILES = 3
LOG2E = 1.4426950408889634


def _sb_kernel(q_ref, k_ref, v_ref, tt_ref, nw_ref, o_ref, *, nblk):
    scale = LOG2E / math.sqrt(SB_DH)
    row = lax.broadcasted_iota(jnp.int32, (CHUNK, CHUNK), 0)
    col = lax.broadcasted_iota(jnp.int32, (CHUNK, CHUNK), 1)
    heads = [slice(g * SB_DH, (g + 1) * SB_DH) for g in range(SB_GROUP)]

    def step(qs, tiles, cs, accs):
        vs, masks, log_beta, split = [], [], [], []
        for j, mask in tiles:
            start = pl.multiple_of(jnp.maximum(j, 0) * CHUNK, CHUNK)
            k = k_ref[pl.ds(start, CHUNK), :]
            vs.append(v_ref[pl.ds(start, CHUNK), :])
            masks.append(mask)
            for g in range(SB_GROUP):
                zs = lax.dot_general(qs[g], k[:, heads[g]], _NT,
                                     preferred_element_type=F32) * scale
                sp = jnp.maximum(zs, 0.0) + jnp.log2(1.0 + jnp.exp2(-jnp.abs(zs)))
                log_beta.append(zs - sp)
                log_keep = jnp.where(mask, -sp, 0.0)
                hi = log_keep.astype(BF16)
                lo = (log_keep - hi.astype(F32)).astype(BF16)
                split.append(jnp.concatenate([hi, lo], axis=1))
        sums = jnp.dot(jnp.concatenate(split, axis=0), tt_ref[...],
                       preferred_element_type=F32)
        cs, accs = list(cs), list(accs)
        for t in range(len(tiles)):
            for g in range(SB_GROUP):
                n = t * SB_GROUP + g
                s = sums[n * CHUNK:(n + 1) * CHUNK]
                after = s[:, :CHUNK] + cs[g]
                w = jnp.where(masks[t], jnp.exp2(log_beta[n] + after), 0.0)
                accs[g] = accs[g] + jnp.dot(w.astype(BF16), vs[t][:, heads[g]],
                                            preferred_element_type=F32)
                cs[g] = cs[g] + s[:, CHUNK:]
        return tuple(cs), tuple(accs)

    def least_decayed(cs):
        return jnp.max(functools.reduce(jnp.maximum, cs))

    def valid_keys(j):
        return j * CHUNK + col >= N_PAD

    def q_block(i, carry):
        start = pl.multiple_of(i * CHUNK, CHUNK)
        q = q_ref[pl.ds(start, CHUNK), :]
        qs = [q[:, heads[g]] for g in range(SB_GROUP)]
        zero = (jnp.zeros((CHUNK, CHUNK), F32),) * SB_GROUP
        first = [(i, jnp.logical_and(col < row, valid_keys(i)))]
        first += [(i - t, valid_keys(i - t)) for t in range(1, SB_FIRST_TILES)]
        cs, accs = step(qs, first, zero, zero)

        def cond(st):
            return jnp.logical_and(st[0] >= 0, st[1] > -SB_UNDERFLOW * LOG2E)

        def body(st):
            j = st[0]
            cs_, accs_ = step(qs, [(j, valid_keys(j))], st[2:2 + SB_GROUP],
                              st[2 + SB_GROUP:])
            return (j - 1, least_decayed(cs_)) + cs_ + accs_

        st = lax.while_loop(cond, body,
                            (i - SB_FIRST_TILES, least_decayed(cs)) + cs + accs)
        accs = st[2 + SB_GROUP:]
        for g in range(SB_GROUP):
            o_ref[pl.ds(start, CHUNK), heads[g]] = _rms(
                accs[g], nw_ref[:, heads[g]]).astype(BF16)
        return carry

    lax.fori_loop(0, nblk, q_block, 0)


def _stick_breaking(rest, nw, b, tp):
    m = rest.shape[0]
    nblk = tp // CHUNK
    width = SB_GROUP * SB_DH
    groups = SB_HEADS // SB_GROUP
    base = 2 * RET_WIDTH // width
    idx = jnp.arange(CHUNK)
    tri = (idx[:, None] > idx[None, :]).astype(BF16)
    tt = jnp.concatenate([tri, jnp.ones((CHUNK, CHUNK), BF16)], axis=1)
    tt = jnp.concatenate([tt, tt], axis=0)
    return pl.pallas_call(
        functools.partial(_sb_kernel, nblk=nblk),
        grid=(b, groups),
        in_specs=[
            pl.BlockSpec((tp, width), lambda i, h: (i, base + h),
                         pipeline_mode=pl.Buffered(1)),
            pl.BlockSpec((tp, width), lambda i, h: (i, base + groups + h),
                         pipeline_mode=pl.Buffered(1)),
            pl.BlockSpec((tp, width), lambda i, h: (i, base + 2 * groups + h),
                         pipeline_mode=pl.Buffered(1)),
            pl.BlockSpec((2 * CHUNK, 2 * CHUNK), lambda i, h: (0, 0)),
            pl.BlockSpec((1, width), lambda i, h: (0, h)),
        ],
        out_specs=pl.BlockSpec((tp, width), lambda i, h: (i, h)),
        out_shape=jax.ShapeDtypeStruct((m, SB_WIDTH), BF16),
        compiler_params=_params("parallel", "parallel"),
        name="stick_breaking",
    )(rest, rest, rest, tt, nw)


def _out_proj_kernel(r_ref, s_ref, w1_ref, w2_ref, o_ref):
    o_ref[...] = (jnp.dot(r_ref[...], w1_ref[...], preferred_element_type=F32)
                  + jnp.dot(s_ref[...], w2_ref[...], preferred_element_type=F32)
                  ).astype(o_ref.dtype)


def _out_proj(ret_out, sb_out, w_out, tp):
    m = ret_out.shape[0]
    d = w_out.shape[1]
    bm = _largest_divisor(tp, 1056, BF16_SUBLANES)
    bn = 1024
    return pl.pallas_call(
        _out_proj_kernel,
        grid=(m // bm, d // bn),
        in_specs=[
            pl.BlockSpec((bm, RET_WIDTH), lambda i, j: (i, 0)),
            pl.BlockSpec((bm, SB_WIDTH), lambda i, j: (i, 0)),
            pl.BlockSpec((RET_WIDTH, bn), lambda i, j: (0, j)),
            pl.BlockSpec((SB_WIDTH, bn), lambda i, j: (1, j)),
        ],
        out_specs=pl.BlockSpec((bm, bn), lambda i, j: (i, j)),
        out_shape=jax.ShapeDtypeStruct((m, d), BF16),
        compiler_params=_params("parallel", "arbitrary"),
        name="out_proj",
    )(ret_out, sb_out, w_out, w_out)


def _post_attn_kernel(a_ref, x_ref, meta_ref, w1_ref, w2_ref, h1_ref, hn_ref):
    h = _padded_rows(x_ref, meta_ref, pl.program_id(1))
    h1 = h + _rms(a_ref[...].astype(F32), w1_ref[...])
    h1_ref[...] = h1
    hn_ref[...] = _rms(h1, w2_ref[...]).astype(BF16)


def _post_attn(a, x, meta, w_post, w_pre):
    m, d = a.shape
    b = x.shape[0]
    nblk = m // (b * CHUNK)
    blk = pl.BlockSpec((CHUNK, d), lambda i, r: (i * nblk + r, 0))
    vec = pl.BlockSpec((1, d), lambda i, r: (0, 0))
    return pl.pallas_call(
        _post_attn_kernel,
        grid=(b, nblk),
        in_specs=[blk, _x_block_spec(d),
                  pl.BlockSpec((N_META, d), lambda i, r: (0, 0)), vec, vec],
        out_specs=[blk, blk],
        out_shape=[jax.ShapeDtypeStruct((m, d), F32),
                   jax.ShapeDtypeStruct((m, d), BF16)],
        compiler_params=_params("parallel", "arbitrary"),
        name="post_attn",
    )(a, x, meta, w_post, w_pre)


HALO = BF16_SUBLANES


FFN_SUB_ROWS = 352


def _ffn_up_kernel(x_ref, xh_ref, wg_ref, wu_ref, cw_ref, cb_ref, o_ref, g_ref):
    bm = x_ref.shape[0]
    sub = FFN_SUB_ROWS if bm % FFN_SUB_ROWS == 0 else bm
    wg = wg_ref[...]
    wu = wu_ref[...]
    g_ref[:HALO, :] = jnp.dot(xh_ref[...], wg, preferred_element_type=F32)
    for r in range(bm // sub):
        x = x_ref[r * sub:(r + 1) * sub, :]
        lo = HALO + r * sub
        g_ref[lo:lo + sub, :] = jnp.dot(x, wg, preferred_element_type=F32)
        up = jnp.dot(x, wu, preferred_element_type=F32)
        ext = g_ref[lo - 8:lo + sub, :]
        prev1 = pltpu.roll(ext, 1, axis=0)
        prev2 = pltpu.roll(prev1, 1, axis=0)
        conv = cb_ref[...] + prev2[8:] * cw_ref[0:1, :]
        conv = conv + prev1[8:] * cw_ref[1:2, :]
        conv = conv + ext[8:] * cw_ref[2:3, :]
        o_ref[r * sub:(r + 1) * sub, :] = (conv * jax.nn.sigmoid(conv) * up).astype(BF16)


def _ffn_up(hn, w_up, conv_w, conv_b, tp):
    m, d = hn.shape
    dff = w_up.shape[1] // 2
    bm = _largest_divisor(tp, 2112, BF16_SUBLANES)
    bn = 256
    halo_blocks = bm // HALO
    up_base = dff // bn
    return pl.pallas_call(
        _ffn_up_kernel,
        grid=(m // bm, dff // bn),
        in_specs=[
            pl.BlockSpec((bm, d), lambda i, j: (i, 0), pipeline_mode=pl.Buffered(1)),
            pl.BlockSpec((HALO, d), lambda i, j: (jnp.maximum(i * halo_blocks - 1, 0), 0)),
            pl.BlockSpec((d, bn), lambda i, j: (0, j)),
            pl.BlockSpec((d, bn), lambda i, j: (0, j + up_base)),
            pl.BlockSpec((CONV_W, bn), lambda i, j: (0, j)),
            pl.BlockSpec((1, bn), lambda i, j: (0, j)),
        ],
        out_specs=pl.BlockSpec((bm, bn), lambda i, j: (i, j)),
        out_shape=jax.ShapeDtypeStruct((m, dff), BF16),
        scratch_shapes=[pltpu.VMEM((bm + HALO, bn), F32)],
        compiler_params=_params("parallel", "arbitrary"),
        name="ffn_up",
    )(hn, hn, w_up, w_up, conv_w, conv_b)


def _ffn_down(act, w_down, tp):
    m, dff = act.shape
    d = w_down.shape[1]
    bm = _largest_divisor(tp, 528, BF16_SUBLANES)
    bn = 512
    return pl.pallas_call(
        _proj_plain_kernel,
        grid=(m // bm, d // bn),
        in_specs=[
            pl.BlockSpec((bm, dff), lambda i, j: (i, 0)),
            pl.BlockSpec((dff, bn), lambda i, j: (0, j)),
        ],
        out_specs=pl.BlockSpec((bm, bn), lambda i, j: (i, j)),
        out_shape=jax.ShapeDtypeStruct((m, d), BF16),
        compiler_params=_params("parallel", "arbitrary"),
        name="ffn_down",
    )(act, w_down)


def _final_kernel(f_ref, h_ref, w_ref, o_ref):
    o_ref[0] = h_ref[...] + _rms(f_ref[...].astype(F32), w_ref[...])


def _final(f, h1, w, b, tp):
    m, d = f.shape
    nblk = tp // CHUNK
    seq = tp - CHUNK
    blk = pl.BlockSpec((CHUNK, d), lambda i, r: (i * nblk + r + 1, 0))
    return pl.pallas_call(
        _final_kernel,
        grid=(b, nblk - 1),
        in_specs=[blk, blk, pl.BlockSpec((1, d), lambda i, r: (0, 0))],
        out_specs=pl.BlockSpec((1, CHUNK, d), lambda i, r: (i, r, 0)),
        out_shape=jax.ShapeDtypeStruct((b, seq, d), F32),
        compiler_params=_params("parallel", "parallel"),
        name="final",
    )(f, h1, w)


def _rotary_tables(tp):
    half = RET_DK // 2
    pos = (jnp.arange(tp) - N_PAD).astype(F32)
    inv = ROPE_BASE ** (-jnp.arange(half, dtype=F32) / half)
    ang = pos[:, None] * inv[None, :]
    return jnp.cos(ang), jnp.sin(ang)


def kernel(x, meta_tokens, attn_pre_norm_w, w_in, ret_gn_w, sb_norm_w, w_out,
           attn_post_norm_w, ffn_pre_norm_w, w_up, conv_w, conv_b, w_down, ffn_post_norm_w):
    b, seq, d = x.shape
    assert seq % CHUNK == 0 and w_in.shape[0] == 1
    tp = seq + CHUNK
    m = b * tp

    hn = _prep(x, meta_tokens, attn_pre_norm_w).reshape(m, d)

    cos, sin = _rotary_tables(tp)
    qk, rest, (w_down_bf,), (w_up_bf,) = _in_proj(
        hn, w_in[0].astype(BF16), cos, sin, tp, [w_down], [w_up])
    ret_out = _retention(qk, rest, ret_gn_w, b, tp)
    sb_out = _stick_breaking(rest, sb_norm_w, b, tp)
    a = _out_proj(ret_out, sb_out, w_out[0].astype(BF16), tp)
    h1, hn2 = _post_attn(a, x, meta_tokens, attn_post_norm_w, ffn_pre_norm_w)

    act = _ffn_up(hn2, w_up_bf, conv_w[0], conv_b, tp)
    f = _ffn_down(act, w_down_bf, tp)
    return _final(f, h1, ffn_post_norm_w, b, tp)
```

```python
import functools
import math

import numpy as np
import jax
import jax.numpy as jnp
from jax import lax
from jax.experimental import pallas as pl
from jax.experimental.pallas import tpu as pltpu

N_META = 16
CHUNK = 128
N_PAD = CHUNK - N_META
RET_HEADS = 8
RET_DK = 256
RET_DV = 256
SB_HEADS = 16
SB_DH = 128
RET_WIDTH = RET_HEADS * RET_DV
RET_QK_WIDTH = RET_HEADS * RET_DK
SB_WIDTH = SB_HEADS * SB_DH
CONV_W = 3
EPS = 1e-6
ROPE_BASE = 10000.0

BF16_SUBLANES = 16
VMEM_LIMIT_BYTES = 58 * 1024 * 1024

F32 = jnp.float32
BF16 = jnp.bfloat16

_NT = (((1,), (1,)), ((), ()))
_TN = (((0,), (0,)), ((), ()))


def _params(*semantics):
    return pltpu.CompilerParams(dimension_semantics=semantics,
                                vmem_limit_bytes=VMEM_LIMIT_BYTES)


def _largest_divisor(n, target, multiple):
    best = None
    for d in range(multiple, min(n, target) + 1, multiple):
        if n % d == 0:
            best = d
    assert best is not None, (n, target, multiple)
    return best


def _rms(v, w):
    ms = jnp.mean(v * v, axis=-1, keepdims=True)
    return v * lax.rsqrt(ms + EPS) * w


ROW_BATCH = 2


def _padded_rows(x_ref, meta_ref, r):
    d = x_ref.shape[-1]
    first = jnp.concatenate([jnp.zeros((N_PAD, d), F32), meta_ref[...]], axis=0)
    return jnp.where(r == 0, first[None], x_ref[...])


def _row_spec(d, shift=0):
    return pl.BlockSpec((ROW_BATCH, CHUNK, d),
                        lambda i, r: (i, jnp.maximum(r + shift, 0), 0))


def _prep_kernel(x_ref, meta_ref, w_ref, hn_ref):
    h = _padded_rows(x_ref, meta_ref, pl.program_id(1))
    hn_ref[...] = _rms(h, w_ref[...]).astype(BF16)


def _prep(x, meta, w):
    b, seq, d = x.shape
    nblk = seq // CHUNK + 1
    tp = nblk * CHUNK
    return pl.pallas_call(
        _prep_kernel,
        grid=(b // ROW_BATCH, nblk),
        in_specs=[
            _row_spec(d, -1),
            pl.BlockSpec((N_META, d), lambda i, r: (0, 0)),
            pl.BlockSpec((1, d), lambda i, r: (0, 0)),
        ],
        out_specs=_row_spec(d),
        out_shape=jax.ShapeDtypeStruct((b, tp, d), BF16),
        compiler_params=_params("parallel", "arbitrary"),
        name="prep",
    )(x, meta, w)


CAST_BLOCK_BYTES = 3 * 1024 * 1024


def _cast_plan(rows, cols, steps):
    best = None
    for br in range(BF16_SUBLANES, rows + 1, BF16_SUBLANES):
        if rows % br:
            continue
        for bc in range(128, cols + 1, 128):
            if cols % bc or br * bc * 4 > CAST_BLOCK_BYTES:
                continue
            if (rows // br) * (cols // bc) <= steps and (best is None or br * bc < best[0] * best[1]):
                best = (br, bc)
    return best


def _with_casts(kernel_fn, n_in, n_out, weights, steps, nj):
    plans, in_specs, out_specs, out_shapes, operands = [], [], [], [], []
    start = 0
    for w in weights:
        _, rows, cols = w.shape
        plan = _cast_plan(rows, cols, steps - start)
        plans.append(plan)
        if plan is None:
            continue
        br, bc = plan
        ncb = cols // bc
        nb = (rows // br) * ncb

        def block(i, j, start=start, nb=nb, ncb=ncb):
            t = jnp.clip(i * nj + j - start, 0, nb - 1)
            return t // ncb, t % ncb

        in_specs.append(pl.BlockSpec((None, br, bc), lambda i, j, block=block: (0,) + block(i, j)))
        out_specs.append(pl.BlockSpec((br, bc), block))
        out_shapes.append(jax.ShapeDtypeStruct((rows, cols), BF16))
        operands.append(w)
        start += nb
    n_cast = len(operands)

    def wrapped(*refs):
        ins, rest = refs[:n_in], refs[n_in:]
        cast_in, rest = rest[:n_cast], rest[n_cast:]
        outs, rest = rest[:n_out], rest[n_out:]
        cast_out, scratch = rest[:n_cast], rest[n_cast:]
        kernel_fn(*ins, *outs, *scratch)
        for ci, co in zip(cast_in, cast_out):
            co[...] = ci[...].astype(BF16)

    def finish(cast_results):
        it = iter(cast_results)
        return [w[0].astype(BF16) if p is None else next(it)
                for w, p in zip(weights, plans)]

    return wrapped, in_specs, out_specs, out_shapes, operands, finish


def _proj_rot_kernel(x_ref, w_ref, cos_ref, sin_ref, o_ref, *, n_q_blocks):
    j = pl.program_id(1)
    acc = jnp.dot(x_ref[...], w_ref[...], preferred_element_type=F32)
    scale = jnp.where(j < n_q_blocks, RET_DK ** -0.5, 1.0).astype(F32)
    cos = cos_ref[...]
    sin = sin_ref[...]
    half = RET_DK // 2
    for h in range(o_ref.shape[1] // RET_DK):
        lo = h * RET_DK
        x1 = acc[:, lo:lo + half]
        x2 = acc[:, lo + half:lo + RET_DK]
        o_ref[:, lo:lo + half] = ((x1 * cos - x2 * sin) * scale).astype(BF16)
        o_ref[:, lo + half:lo + RET_DK] = ((x1 * sin + x2 * cos) * scale).astype(BF16)


def _proj_plain_kernel(x_ref, w_ref, o_ref):
    o_ref[...] = jnp.dot(x_ref[...], w_ref[...],
                         preferred_element_type=F32).astype(o_ref.dtype)


def _in_proj(hn, w_in, cos, sin, tp, later_weights_rot, later_weights_rest):
    m, d = hn.shape
    bm = _largest_divisor(tp, 1056, BF16_SUBLANES)
    bn = 1024
    pos_blocks = tp // bm
    n_rot = 2 * RET_QK_WIDTH
    n_rest = w_in.shape[1] - n_rot

    grid = (m // bm, n_rot // bn)
    body, c_in, c_out, c_shape, c_ops, finish_rot = _with_casts(
        functools.partial(_proj_rot_kernel, n_q_blocks=RET_QK_WIDTH // bn),
        4, 1, later_weights_rot, grid[0] * grid[1], grid[1])
    qk, *cast_rot = pl.pallas_call(
        body,
        grid=grid,
        in_specs=[
            pl.BlockSpec((bm, d), lambda i, j: (i, 0)),
            pl.BlockSpec((d, bn), lambda i, j: (0, j)),
            pl.BlockSpec((bm, RET_DK // 2), lambda i, j: (i % pos_blocks, 0)),
            pl.BlockSpec((bm, RET_DK // 2), lambda i, j: (i % pos_blocks, 0)),
        ] + c_in,
        out_specs=[pl.BlockSpec((bm, bn), lambda i, j: (i, j))] + c_out,
        out_shape=[jax.ShapeDtypeStruct((m, n_rot), BF16)] + c_shape,
        compiler_params=_params("arbitrary", "arbitrary"),
        name="in_proj_rot",
    )(hn, w_in, cos, sin, *c_ops)

    rot_blocks = n_rot // bn
    grid = (m // bm, n_rest // bn)
    body, c_in, c_out, c_shape, c_ops, finish_rest = _with_casts(
        _proj_plain_kernel, 2, 1, later_weights_rest, grid[0] * grid[1], grid[1])
    rest, *cast_rest = pl.pallas_call(
        body,
        grid=grid,
        in_specs=[
            pl.BlockSpec((bm, d), lambda i, j: (i, 0)),
            pl.BlockSpec((d, bn), lambda i, j: (0, j + rot_blocks)),
        ] + c_in,
        out_specs=[pl.BlockSpec((bm, bn), lambda i, j: (i, j))] + c_out,
        out_shape=[jax.ShapeDtypeStruct((m, n_rest), BF16)] + c_shape,
        compiler_params=_params("arbitrary", "arbitrary"),
        name="in_proj_rest",
    )(hn, w_in, *c_ops)
    return qk, rest, finish_rot(cast_rot), finish_rest(cast_rest)


def _ret_log_gamma():
    return np.log(1.0 - 2.0 ** (-5.0 - np.arange(RET_HEADS, dtype=np.float64)))


def _ret_kernel(q_ref, k_ref, v_ref, g_ref, intra_ref, qd_ref, kd_ref, gn_ref,
                o_ref, state_ref):
    c = pl.program_id(1)

    @pl.when(c == 0)
    def _():
        state_ref[...] = jnp.zeros(state_ref.shape, F32)

    chunk_decay = np.exp(CHUNK * _ret_log_gamma()).astype(np.float32)
    for h in range(RET_HEADS):
        sl = slice(h * RET_DK, (h + 1) * RET_DK)
        q = q_ref[:, sl]
        k = k_ref[:, sl]
        v = v_ref[:, sl]
        s = lax.dot_general(q, k, _NT, preferred_element_type=F32) * intra_ref[h]
        intra = jnp.dot(s.astype(BF16), v, preferred_element_type=F32)
        st = state_ref[h]
        qd = qd_ref[h]
        kd = kd_ref[h]
        cross = jnp.dot(q, st.astype(BF16), preferred_element_type=F32)
        cross = cross * jnp.concatenate([qd, qd], axis=1)
        kdk = (k.astype(F32) * jnp.concatenate([kd, kd], axis=1)).astype(BF16)
        upd = lax.dot_general(kdk, v, _TN, preferred_element_type=F32)
        state_ref[h] = st * float(chunk_decay[h]) + upd
        y = intra + cross
        yn = _rms(y, gn_ref[:, sl])
        g = g_ref[:, sl].astype(F32)
        o_ref[:, sl] = (g * jax.nn.sigmoid(g) * yn).astype(BF16)


def _retention(qk, rest, gn_w, b, tp, later_weights):
    m = qk.shape[0]
    nblk = tp // CHUNK
    lg = jnp.log(1.0 - 2.0 ** (-5.0 - jnp.arange(RET_HEADS, dtype=F32)))
    idx = jnp.arange(CHUNK, dtype=F32)
    diff = idx[:, None] - idx[None, :]
    intra = jnp.where(diff[None] >= 0,
                      jnp.exp(jnp.maximum(diff, 0.0)[None] * lg[:, None, None]), 0.0)
    qd = jnp.exp((idx[None, :] + 1.0) * lg[:, None])
    kd = jnp.exp((CHUNK - 1.0 - idx[None, :]) * lg[:, None])
    qd = jnp.broadcast_to(qd[:, :, None], (RET_HEADS, CHUNK, CHUNK))
    kd = jnp.broadcast_to(kd[:, :, None], (RET_HEADS, CHUNK, CHUNK))
    row = lambda i, c: (i * nblk + c)
    table = pl.BlockSpec((RET_HEADS, CHUNK, CHUNK), lambda i, c: (0, 0, 0))
    body, c_in, c_out, c_shape, c_ops, finish = _with_casts(
        _ret_kernel, 8, 1, later_weights, b * nblk, nblk)
    out, *cast = pl.pallas_call(
        body,
        grid=(b, nblk),
        in_specs=[
            pl.BlockSpec((CHUNK, RET_QK_WIDTH), lambda i, c: (row(i, c), 0)),
            pl.BlockSpec((CHUNK, RET_QK_WIDTH), lambda i, c: (row(i, c), 1)),
            pl.BlockSpec((CHUNK, RET_WIDTH), lambda i, c: (row(i, c), 0)),
            pl.BlockSpec((CHUNK, RET_WIDTH), lambda i, c: (row(i, c), 1)),
            table, table, table,
            pl.BlockSpec((1, RET_WIDTH), lambda i, c: (0, 0)),
        ] + c_in,
        out_specs=[pl.BlockSpec((CHUNK, RET_WIDTH), lambda i, c: (row(i, c), 0))] + c_out,
        out_shape=[jax.ShapeDtypeStruct((m, RET_WIDTH), BF16)] + c_shape,
        scratch_shapes=[pltpu.VMEM((RET_HEADS, RET_DK, RET_DV), F32)],
        compiler_params=_params("arbitrary", "arbitrary"),
        name="retention",
    )(qk, qk, rest, rest, intra, qd, kd, gn_w, *c_ops)
    return out, finish(cast)


SB_GROUP = 8
SB_UNDERFLOW = 110.0
SB_FIRST_TILES = 3
LOG2E = 1.4426950408889634


def _sb_kernel(q_ref, k_ref, v_ref, tt_ref, nw_ref, o_ref, *, nblk):
    scale = LOG2E / math.sqrt(SB_DH)
    row = lax.broadcasted_iota(jnp.int32, (CHUNK, CHUNK), 0)
    col = lax.broadcasted_iota(jnp.int32, (CHUNK, CHUNK), 1)
    heads = [slice(g * SB_DH, (g + 1) * SB_DH) for g in range(SB_GROUP)]

    def step(qs, tiles, cs, accs):
        vs, masks, log_beta, split = [], [], [], []
        for j, mask in tiles:
            start = pl.multiple_of(jnp.maximum(j, 0) * CHUNK, CHUNK)
            k = k_ref[pl.ds(start, CHUNK), :]
            vs.append(v_ref[pl.ds(start, CHUNK), :])
            masks.append(mask)
            for g in range(SB_GROUP):
                zs = lax.dot_general(qs[g], k[:, heads[g]], _NT,
                                     preferred_element_type=F32) * scale
                sp = jnp.maximum(zs, 0.0) + jnp.log2(1.0 + jnp.exp2(-jnp.abs(zs)))
                log_beta.append(zs - sp)
                log_keep = jnp.where(mask, -sp, 0.0)
                hi = log_keep.astype(BF16)
                lo = (log_keep - hi.astype(F32)).astype(BF16)
                split.append(jnp.concatenate([hi, lo], axis=1))
        sums = jnp.dot(jnp.concatenate(split, axis=0), tt_ref[...],
                       preferred_element_type=F32)
        cs, accs = list(cs), list(accs)
        for t in range(len(tiles)):
            for g in range(SB_GROUP):
                n = t * SB_GROUP + g
                s = sums[n * CHUNK:(n + 1) * CHUNK]
                after = s[:, :CHUNK] + cs[g]
                w = jnp.where(masks[t], jnp.exp2(log_beta[n] + after), 0.0)
                accs[g] = accs[g] + jnp.dot(w.astype(BF16), vs[t][:, heads[g]],
                                            preferred_element_type=F32)
                cs[g] = cs[g] + s[:, CHUNK:]
        return tuple(cs), tuple(accs)

    def least_decayed(cs):
        return jnp.max(functools.reduce(jnp.maximum, cs))

    def valid_keys(j):
        return j * CHUNK + col >= N_PAD

    def q_block(i, carry):
        start = pl.multiple_of(i * CHUNK, CHUNK)
        q = q_ref[pl.ds(start, CHUNK), :]
        qs = [q[:, heads[g]] for g in range(SB_GROUP)]
        zero = (jnp.zeros((CHUNK, CHUNK), F32),) * SB_GROUP
        first = [(i, jnp.logical_and(col < row, valid_keys(i)))]
        first += [(i - t, valid_keys(i - t)) for t in range(1, SB_FIRST_TILES)]
        cs, accs = step(qs, first, zero, zero)

        def cond(st):
            return jnp.logical_and(st[0] >= 0, st[1] > -SB_UNDERFLOW * LOG2E)

        def body(st):
            j = st[0]
            cs_, accs_ = step(qs, [(j, valid_keys(j))], st[2:2 + SB_GROUP],
                              st[2 + SB_GROUP:])
            return (j - 1, least_decayed(cs_)) + cs_ + accs_

        st = lax.while_loop(cond, body,
                            (i - SB_FIRST_TILES, least_decayed(cs)) + cs + accs)
        accs = st[2 + SB_GROUP:]
        for g in range(SB_GROUP):
            o_ref[pl.ds(start, CHUNK), heads[g]] = _rms(
                accs[g], nw_ref[:, heads[g]]).astype(BF16)
        return carry

    lax.fori_loop(0, nblk, q_block, 0)


def _stick_breaking(rest, nw, b, tp):
    m = rest.shape[0]
    nblk = tp // CHUNK
    width = SB_GROUP * SB_DH
    groups = SB_HEADS // SB_GROUP
    base = 2 * RET_WIDTH // width
    idx = jnp.arange(CHUNK)
    tri = (idx[:, None] > idx[None, :]).astype(BF16)
    tt = jnp.concatenate([tri, jnp.ones((CHUNK, CHUNK), BF16)], axis=1)
    tt = jnp.concatenate([tt, tt], axis=0)
    return pl.pallas_call(
        functools.partial(_sb_kernel, nblk=nblk),
        grid=(b, groups),
        in_specs=[
            pl.BlockSpec((tp, width), lambda i, h: (i, base + h),
                         pipeline_mode=pl.Buffered(1)),
            pl.BlockSpec((tp, width), lambda i, h: (i, base + groups + h),
                         pipeline_mode=pl.Buffered(1)),
            pl.BlockSpec((tp, width), lambda i, h: (i, base + 2 * groups + h),
                         pipeline_mode=pl.Buffered(1)),
            pl.BlockSpec((2 * CHUNK, 2 * CHUNK), lambda i, h: (0, 0)),
            pl.BlockSpec((1, width), lambda i, h: (0, h)),
        ],
        out_specs=pl.BlockSpec((tp, width), lambda i, h: (i, h)),
        out_shape=jax.ShapeDtypeStruct((m, SB_WIDTH), BF16),
        compiler_params=_params("parallel", "parallel"),
        name="stick_breaking",
    )(rest, rest, rest, tt, nw)


def _out_proj_kernel(r_ref, s_ref, w1_ref, w2_ref, o_ref):
    o_ref[...] = (jnp.dot(r_ref[...], w1_ref[...], preferred_element_type=F32)
                  + jnp.dot(s_ref[...], w2_ref[...], preferred_element_type=F32)
                  ).astype(o_ref.dtype)


def _out_proj(ret_out, sb_out, w_out, tp):
    m = ret_out.shape[0]
    d = w_out.shape[1]
    bm = _largest_divisor(tp, 1056, BF16_SUBLANES)
    bn = 1024
    return pl.pallas_call(
        _out_proj_kernel,
        grid=(m // bm, d // bn),
        in_specs=[
            pl.BlockSpec((bm, RET_WIDTH), lambda i, j: (i, 0)),
            pl.BlockSpec((bm, SB_WIDTH), lambda i, j: (i, 0)),
            pl.BlockSpec((RET_WIDTH, bn), lambda i, j: (0, j)),
            pl.BlockSpec((SB_WIDTH, bn), lambda i, j: (1, j)),
        ],
        out_specs=pl.BlockSpec((bm, bn), lambda i, j: (i, j)),
        out_shape=jax.ShapeDtypeStruct((m, d), BF16),
        compiler_params=_params("parallel", "arbitrary"),
        name="out_proj",
    )(ret_out, sb_out, w_out, w_out)


def _post_attn_kernel(a_ref, x_ref, meta_ref, w1_ref, w2_ref, h1_ref, hn_ref):
    h = _padded_rows(x_ref, meta_ref, pl.program_id(1))
    h1 = h + _rms(a_ref[...].astype(F32), w1_ref[...])
    h1_ref[...] = h1
    hn_ref[...] = _rms(h1, w2_ref[...]).astype(BF16)


def _post_attn(a, x, meta, w_post, w_pre):
    b, tp, d = a.shape
    vec = pl.BlockSpec((1, d), lambda i, r: (0, 0))
    return pl.pallas_call(
        _post_attn_kernel,
        grid=(b // ROW_BATCH, tp // CHUNK),
        in_specs=[_row_spec(d), _row_spec(d, -1),
                  pl.BlockSpec((N_META, d), lambda i, r: (0, 0)), vec, vec],
        out_specs=[_row_spec(d), _row_spec(d)],
        out_shape=[jax.ShapeDtypeStruct((b, tp, d), F32),
                   jax.ShapeDtypeStruct((b, tp, d), BF16)],
        compiler_params=_params("parallel", "arbitrary"),
        name="post_attn",
    )(a, x, meta, w_post, w_pre)


HALO = BF16_SUBLANES


FFN_SUB_ROWS = 352


def _ffn_up_kernel(x_ref, xh_ref, wg_ref, wu_ref, cw_ref, cb_ref, o_ref, g_ref):
    bm = x_ref.shape[0]
    sub = FFN_SUB_ROWS if bm % FFN_SUB_ROWS == 0 else bm
    wg = wg_ref[...]
    wu = wu_ref[...]
    g_ref[:HALO, :] = jnp.dot(xh_ref[...], wg, preferred_element_type=F32)
    for r in range(bm // sub):
        x = x_ref[r * sub:(r + 1) * sub, :]
        lo = HALO + r * sub
        g_ref[lo:lo + sub, :] = jnp.dot(x, wg, preferred_element_type=F32)
        up = jnp.dot(x, wu, preferred_element_type=F32)
        ext = g_ref[lo - 8:lo + sub, :]
        prev1 = pltpu.roll(ext, 1, axis=0)
        prev2 = pltpu.roll(prev1, 1, axis=0)
        conv = cb_ref[...] + prev2[8:] * cw_ref[0:1, :]
        conv = conv + prev1[8:] * cw_ref[1:2, :]
        conv = conv + ext[8:] * cw_ref[2:3, :]
        o_ref[r * sub:(r + 1) * sub, :] = (conv * jax.nn.sigmoid(conv) * up).astype(BF16)


def _ffn_up(hn, w_up, conv_w, conv_b, tp):
    m, d = hn.shape
    dff = w_up.shape[1] // 2
    bm = _largest_divisor(tp, 2112, BF16_SUBLANES)
    bn = 256
    halo_blocks = bm // HALO
    up_base = dff // bn
    return pl.pallas_call(
        _ffn_up_kernel,
        grid=(m // bm, dff // bn),
        in_specs=[
            pl.BlockSpec((bm, d), lambda i, j: (i, 0), pipeline_mode=pl.Buffered(1)),
            pl.BlockSpec((HALO, d), lambda i, j: (jnp.maximum(i * halo_blocks - 1, 0), 0)),
            pl.BlockSpec((d, bn), lambda i, j: (0, j)),
            pl.BlockSpec((d, bn), lambda i, j: (0, j + up_base)),
            pl.BlockSpec((CONV_W, bn), lambda i, j: (0, j)),
            pl.BlockSpec((1, bn), lambda i, j: (0, j)),
        ],
        out_specs=pl.BlockSpec((bm, bn), lambda i, j: (i, j)),
        out_shape=jax.ShapeDtypeStruct((m, dff), BF16),
        scratch_shapes=[pltpu.VMEM((bm + HALO, bn), F32)],
        compiler_params=_params("parallel", "arbitrary"),
        name="ffn_up",
    )(hn, hn, w_up, w_up, conv_w, conv_b)


def _ffn_down(act, w_down, tp):
    m, dff = act.shape
    d = w_down.shape[1]
    bm = _largest_divisor(tp, 528, BF16_SUBLANES)
    bn = 512
    return pl.pallas_call(
        _proj_plain_kernel,
        grid=(m // bm, d // bn),
        in_specs=[
            pl.BlockSpec((bm, dff), lambda i, j: (i, 0)),
            pl.BlockSpec((dff, bn), lambda i, j: (0, j)),
        ],
        out_specs=pl.BlockSpec((bm, bn), lambda i, j: (i, j)),
        out_shape=jax.ShapeDtypeStruct((m, d), BF16),
        compiler_params=_params("parallel", "arbitrary"),
        name="ffn_down",
    )(act, w_down)


def _final_kernel(f_ref, h_ref, w_ref, o_ref):
    o_ref[...] = h_ref[...] + _rms(f_ref[...].astype(F32), w_ref[...])


def _final(f, h1, w):
    b, tp, d = f.shape
    return pl.pallas_call(
        _final_kernel,
        grid=(b // ROW_BATCH, tp // CHUNK - 1),
        in_specs=[_row_spec(d, 1), _row_spec(d, 1),
                  pl.BlockSpec((1, d), lambda i, r: (0, 0))],
        out_specs=_row_spec(d),
        out_shape=jax.ShapeDtypeStruct((b, tp - CHUNK, d), F32),
        compiler_params=_params("parallel", "parallel"),
        name="final",
    )(f, h1, w)


def _rotary_tables(tp):
    half = RET_DK // 2
    pos = (jnp.arange(tp) - N_PAD).astype(F32)
    inv = ROPE_BASE ** (-jnp.arange(half, dtype=F32) / half)
    ang = pos[:, None] * inv[None, :]
    return jnp.cos(ang), jnp.sin(ang)


def kernel(x, meta_tokens, attn_pre_norm_w, w_in, ret_gn_w, sb_norm_w, w_out,
           attn_post_norm_w, ffn_pre_norm_w, w_up, conv_w, conv_b, w_down, ffn_post_norm_w):
    b, seq, d = x.shape
    assert seq % CHUNK == 0 and w_in.shape[0] == 1
    tp = seq + CHUNK
    m = b * tp

    hn = _prep(x, meta_tokens, attn_pre_norm_w).reshape(m, d)

    cos, sin = _rotary_tables(tp)
    qk, rest, (w_down_bf,), (w_up_bf,) = _in_proj(
        hn, w_in[0].astype(BF16), cos, sin, tp, [w_down], [w_up])
    ret_out, (w_out_bf,) = _retention(qk, rest, ret_gn_w, b, tp, [w_out])
    sb_out = _stick_breaking(rest, sb_norm_w, b, tp)
    a = _out_proj(ret_out, sb_out, w_out_bf, tp)
    h1, hn2 = _post_attn(a.reshape(b, tp, d), x, meta_tokens, attn_post_norm_w,
                         ffn_pre_norm_w)

    act = _ffn_up(hn2.reshape(m, d), w_up_bf, conv_w[0], conv_b, tp)
    f = _ffn_down(act, w_down_bf, tp)
    return _final(f.reshape(b, tp, d), h1, ffn_post_norm_w)
```

```python
import functools
import math

import numpy as np
import jax
import jax.numpy as jnp
from jax import lax
from jax.experimental import pallas as pl
from jax.experimental.pallas import tpu as pltpu

N_META = 16
CHUNK = 128
N_PAD = CHUNK - N_META
RET_HEADS = 8
RET_DK = 256
RET_DV = 256
SB_HEADS = 16
SB_DH = 128
RET_WIDTH = RET_HEADS * RET_DV
RET_QK_WIDTH = RET_HEADS * RET_DK
SB_WIDTH = SB_HEADS * SB_DH
CONV_W = 3
EPS = 1e-6
ROPE_BASE = 10000.0

BF16_SUBLANES = 16
VMEM_LIMIT_BYTES = 58 * 1024 * 1024

F32 = jnp.float32
BF16 = jnp.bfloat16

_NT = (((1,), (1,)), ((), ()))
_TN = (((0,), (0,)), ((), ()))


def _params(*semantics):
    return pltpu.CompilerParams(dimension_semantics=semantics,
                                vmem_limit_bytes=VMEM_LIMIT_BYTES)


def _largest_divisor(n, target, multiple):
    best = None
    for d in range(multiple, min(n, target) + 1, multiple):
        if n % d == 0:
            best = d
    assert best is not None, (n, target, multiple)
    return best


def _rms(v, w):
    ms = jnp.mean(v * v, axis=-1, keepdims=True)
    return v * lax.rsqrt(ms + EPS) * w


ROW_BATCH = 2


def _padded_rows(x_ref, meta_ref, r):
    d = x_ref.shape[-1]
    first = jnp.concatenate([jnp.zeros((N_PAD, d), F32), meta_ref[...]], axis=0)
    return jnp.where(r == 0, first[None], x_ref[...])


def _row_spec(d, shift=0):
    return pl.BlockSpec((ROW_BATCH, CHUNK, d),
                        lambda i, r: (i, jnp.maximum(r + shift, 0), 0))


def _prep_kernel(x_ref, meta_ref, w_ref, hn_ref):
    h = _padded_rows(x_ref, meta_ref, pl.program_id(1))
    hn_ref[...] = _rms(h, w_ref[...]).astype(BF16)


def _prep(x, meta, w):
    b, seq, d = x.shape
    nblk = seq // CHUNK + 1
    tp = nblk * CHUNK
    return pl.pallas_call(
        _prep_kernel,
        grid=(b // ROW_BATCH, nblk),
        in_specs=[
            _row_spec(d, -1),
            pl.BlockSpec((N_META, d), lambda i, r: (0, 0)),
            pl.BlockSpec((1, d), lambda i, r: (0, 0)),
        ],
        out_specs=_row_spec(d),
        out_shape=jax.ShapeDtypeStruct((b, tp, d), BF16),
        compiler_params=_params("parallel", "arbitrary"),
        name="prep",
    )(x, meta, w)


CAST_BLOCK_BYTES = 3 * 1024 * 1024


def _cast_plan(rows, cols, steps):
    best = None
    for br in range(BF16_SUBLANES, rows + 1, BF16_SUBLANES):
        if rows % br:
            continue
        for bc in range(128, cols + 1, 128):
            if cols % bc or br * bc * 4 > CAST_BLOCK_BYTES:
                continue
            if (rows // br) * (cols // bc) <= steps and (best is None or br * bc < best[0] * best[1]):
                best = (br, bc)
    return best


def _with_casts(kernel_fn, n_in, n_out, weights, steps, nj):
    plans, in_specs, out_specs, out_shapes, operands = [], [], [], [], []
    start = 0
    for w in weights:
        _, rows, cols = w.shape
        plan = _cast_plan(rows, cols, steps - start)
        plans.append(plan)
        if plan is None:
            continue
        br, bc = plan
        ncb = cols // bc
        nb = (rows // br) * ncb

        def block(i, j, start=start, nb=nb, ncb=ncb):
            t = jnp.clip(i * nj + j - start, 0, nb - 1)
            return t // ncb, t % ncb

        in_specs.append(pl.BlockSpec((None, br, bc), lambda i, j, block=block: (0,) + block(i, j)))
        out_specs.append(pl.BlockSpec((br, bc), block))
        out_shapes.append(jax.ShapeDtypeStruct((rows, cols), BF16))
        operands.append(w)
        start += nb
    n_cast = len(operands)

    def wrapped(*refs):
        ins, rest = refs[:n_in], refs[n_in:]
        cast_in, rest = rest[:n_cast], rest[n_cast:]
        outs, rest = rest[:n_out], rest[n_out:]
        cast_out, scratch = rest[:n_cast], rest[n_cast:]
        kernel_fn(*ins, *outs, *scratch)
        for ci, co in zip(cast_in, cast_out):
            co[...] = ci[...].astype(BF16)

    def finish(cast_results):
        it = iter(cast_results)
        return [w[0].astype(BF16) if p is None else next(it)
                for w, p in zip(weights, plans)]

    return wrapped, in_specs, out_specs, out_shapes, operands, finish


def _proj_rot_kernel(x_ref, w_ref, cos_ref, sin_ref, o_ref, *, n_q_blocks):
    j = pl.program_id(1)
    acc = jnp.dot(x_ref[...], w_ref[...], preferred_element_type=F32)
    scale = jnp.where(j < n_q_blocks, RET_DK ** -0.5, 1.0).astype(F32)
    cos = cos_ref[...]
    sin = sin_ref[...]
    half = RET_DK // 2
    for h in range(o_ref.shape[1] // RET_DK):
        lo = h * RET_DK
        x1 = acc[:, lo:lo + half]
        x2 = acc[:, lo + half:lo + RET_DK]
        o_ref[:, lo:lo + half] = ((x1 * cos - x2 * sin) * scale).astype(BF16)
        o_ref[:, lo + half:lo + RET_DK] = ((x1 * sin + x2 * cos) * scale).astype(BF16)


def _proj_plain_kernel(x_ref, w_ref, o_ref):
    o_ref[...] = jnp.dot(x_ref[...], w_ref[...],
                         preferred_element_type=F32).astype(o_ref.dtype)


def _in_proj(hn, w_in, cos, sin, tp, later_weights_rot, later_weights_rest):
    m, d = hn.shape
    bm = _largest_divisor(tp, 1056, BF16_SUBLANES)
    bn = 1024
    pos_blocks = tp // bm
    n_rot = 2 * RET_QK_WIDTH
    n_rest = w_in.shape[1] - n_rot

    grid = (m // bm, n_rot // bn)
    body, c_in, c_out, c_shape, c_ops, finish_rot = _with_casts(
        functools.partial(_proj_rot_kernel, n_q_blocks=RET_QK_WIDTH // bn),
        4, 1, later_weights_rot, grid[0] * grid[1], grid[1])
    qk, *cast_rot = pl.pallas_call(
        body,
        grid=grid,
        in_specs=[
            pl.BlockSpec((bm, d), lambda i, j: (i, 0)),
            pl.BlockSpec((d, bn), lambda i, j: (0, j)),
            pl.BlockSpec((bm, RET_DK // 2), lambda i, j: (i % pos_blocks, 0)),
            pl.BlockSpec((bm, RET_DK // 2), lambda i, j: (i % pos_blocks, 0)),
        ] + c_in,
        out_specs=[pl.BlockSpec((bm, bn), lambda i, j: (i, j))] + c_out,
        out_shape=[jax.ShapeDtypeStruct((m, n_rot), BF16)] + c_shape,
        compiler_params=_params("arbitrary", "arbitrary"),
        name="in_proj_rot",
    )(hn, w_in, cos, sin, *c_ops)

    rot_blocks = n_rot // bn
    grid = (m // bm, n_rest // bn)
    body, c_in, c_out, c_shape, c_ops, finish_rest = _with_casts(
        _proj_plain_kernel, 2, 1, later_weights_rest, grid[0] * grid[1], grid[1])
    rest, *cast_rest = pl.pallas_call(
        body,
        grid=grid,
        in_specs=[
            pl.BlockSpec((bm, d), lambda i, j: (i, 0)),
            pl.BlockSpec((d, bn), lambda i, j: (0, j + rot_blocks)),
        ] + c_in,
        out_specs=[pl.BlockSpec((bm, bn), lambda i, j: (i, j))] + c_out,
        out_shape=[jax.ShapeDtypeStruct((m, n_rest), BF16)] + c_shape,
        compiler_params=_params("arbitrary", "arbitrary"),
        name="in_proj_rest",
    )(hn, w_in, *c_ops)
    return qk, rest, finish_rot(cast_rot), finish_rest(cast_rest)


def _ret_log_gamma():
    return np.log(1.0 - 2.0 ** (-5.0 - np.arange(RET_HEADS, dtype=np.float64)))


def _ret_kernel(q_ref, k_ref, v_ref, g_ref, intra_ref, qd_ref, kd_ref, gn_ref,
                o_ref, state_ref):
    c = pl.program_id(1)

    @pl.when(c == 0)
    def _():
        state_ref[...] = jnp.zeros(state_ref.shape, F32)

    chunk_decay = np.exp(CHUNK * _ret_log_gamma()).astype(np.float32)
    for n in range(q_ref.shape[0]):
        for h in range(RET_HEADS):
            sl = slice(h * RET_DK, (h + 1) * RET_DK)
            q = q_ref[n, :, sl]
            k = k_ref[n, :, sl]
            v = v_ref[n, :, sl]
            s = lax.dot_general(q, k, _NT, preferred_element_type=F32) * intra_ref[h]
            intra = jnp.dot(s.astype(BF16), v, preferred_element_type=F32)
            st = state_ref[n, h]
            qd = qd_ref[h]
            kd = kd_ref[h]
            cross = jnp.dot(q, st.astype(BF16), preferred_element_type=F32)
            cross = cross * jnp.concatenate([qd, qd], axis=1)
            kdk = (k.astype(F32) * jnp.concatenate([kd, kd], axis=1)).astype(BF16)
            upd = lax.dot_general(kdk, v, _TN, preferred_element_type=F32)
            state_ref[n, h] = st * float(chunk_decay[h]) + upd
            y = intra + cross
            yn = _rms(y, gn_ref[:, sl])
            g = g_ref[n, :, sl].astype(F32)
            o_ref[n, :, sl] = (g * jax.nn.sigmoid(g) * yn).astype(BF16)


def _retention(qk, rest, gn_w, b, tp, later_weights):
    m = qk.shape[0]
    nblk = tp // CHUNK
    lg = jnp.log(1.0 - 2.0 ** (-5.0 - jnp.arange(RET_HEADS, dtype=F32)))
    idx = jnp.arange(CHUNK, dtype=F32)
    diff = idx[:, None] - idx[None, :]
    intra = jnp.where(diff[None] >= 0,
                      jnp.exp(jnp.maximum(diff, 0.0)[None] * lg[:, None, None]), 0.0)
    qd = jnp.exp((idx[None, :] + 1.0) * lg[:, None])
    kd = jnp.exp((CHUNK - 1.0 - idx[None, :]) * lg[:, None])
    qd = jnp.broadcast_to(qd[:, :, None], (RET_HEADS, CHUNK, CHUNK))
    kd = jnp.broadcast_to(kd[:, :, None], (RET_HEADS, CHUNK, CHUNK))
    table = pl.BlockSpec((RET_HEADS, CHUNK, CHUNK), lambda i, c: (0, 0, 0))
    grid = (b // ROW_BATCH, nblk)
    body, c_in, c_out, c_shape, c_ops, finish = _with_casts(
        _ret_kernel, 8, 1, later_weights, grid[0] * grid[1], nblk)

    def cols(block, width):
        return pl.BlockSpec((ROW_BATCH, CHUNK, width), lambda i, c: (i, c, block))

    qk = qk.reshape(b, tp, -1)
    rest = rest.reshape(b, tp, -1)
    out, *cast = pl.pallas_call(
        body,
        grid=grid,
        in_specs=[cols(0, RET_QK_WIDTH), cols(1, RET_QK_WIDTH),
                  cols(0, RET_WIDTH), cols(1, RET_WIDTH),
                  table, table, table,
                  pl.BlockSpec((1, RET_WIDTH), lambda i, c: (0, 0))] + c_in,
        out_specs=[cols(0, RET_WIDTH)] + c_out,
        out_shape=[jax.ShapeDtypeStruct((b, tp, RET_WIDTH), BF16)] + c_shape,
        scratch_shapes=[pltpu.VMEM((ROW_BATCH, RET_HEADS, RET_DK, RET_DV), F32)],
        compiler_params=_params("arbitrary", "arbitrary"),
        name="retention",
    )(qk, qk, rest, rest, intra, qd, kd, gn_w, *c_ops)
    return out.reshape(m, RET_WIDTH), finish(cast)


SB_GROUP = 8
SB_UNDERFLOW = 110.0
SB_FIRST_TILES = 3
LOG2E = 1.4426950408889634


def _sb_kernel(q_ref, k_ref, v_ref, tt_ref, nw_ref, o_ref, *, nblk):
    scale = LOG2E / math.sqrt(SB_DH)
    row = lax.broadcasted_iota(jnp.int32, (CHUNK, CHUNK), 0)
    col = lax.broadcasted_iota(jnp.int32, (CHUNK, CHUNK), 1)
    heads = [slice(g * SB_DH, (g + 1) * SB_DH) for g in range(SB_GROUP)]

    def step(qs, tiles, cs, accs):
        vs, masks, log_beta, split = [], [], [], []
        for j, mask in tiles:
            start = pl.multiple_of(jnp.maximum(j, 0) * CHUNK, CHUNK)
            k = k_ref[pl.ds(start, CHUNK), :]
            vs.append(v_ref[pl.ds(start, CHUNK), :])
            masks.append(mask)
            for g in range(SB_GROUP):
                zs = lax.dot_general(qs[g], k[:, heads[g]], _NT,
                                     preferred_element_type=F32) * scale
                sp = jnp.maximum(zs, 0.0) + jnp.log2(1.0 + jnp.exp2(-jnp.abs(zs)))
                log_beta.append(zs - sp)
                log_keep = jnp.where(mask, -sp, 0.0)
                hi = log_keep.astype(BF16)
                lo = (log_keep - hi.astype(F32)).astype(BF16)
                split.append(jnp.concatenate([hi, lo], axis=1))
        sums = jnp.dot(jnp.concatenate(split, axis=0), tt_ref[...],
                       preferred_element_type=F32)
        cs, accs = list(cs), list(accs)
        for t in range(len(tiles)):
            for g in range(SB_GROUP):
                n = t * SB_GROUP + g
                s = sums[n * CHUNK:(n + 1) * CHUNK]
                after = s[:, :CHUNK] + cs[g]
                w = jnp.where(masks[t], jnp.exp2(log_beta[n] + after), 0.0)
                accs[g] = accs[g] + jnp.dot(w.astype(BF16), vs[t][:, heads[g]],
                                            preferred_element_type=F32)
                cs[g] = cs[g] + s[:, CHUNK:]
        return tuple(cs), tuple(accs)

    def least_decayed(cs):
        return jnp.max(functools.reduce(jnp.maximum, cs))

    def valid_keys(j):
        return j * CHUNK + col >= N_PAD

    def q_block(i, carry):
        start = pl.multiple_of(i * CHUNK, CHUNK)
        q = q_ref[pl.ds(start, CHUNK), :]
        qs = [q[:, heads[g]] for g in range(SB_GROUP)]
        zero = (jnp.zeros((CHUNK, CHUNK), F32),) * SB_GROUP
        first = [(i, jnp.logical_and(col < row, valid_keys(i)))]
        first += [(i - t, valid_keys(i - t)) for t in range(1, SB_FIRST_TILES)]
        cs, accs = step(qs, first, zero, zero)

        def cond(st):
            return jnp.logical_and(st[0] >= 0, st[1] > -SB_UNDERFLOW * LOG2E)

        def body(st):
            j = st[0]
            cs_, accs_ = step(qs, [(j, valid_keys(j))], st[2:2 + SB_GROUP],
                              st[2 + SB_GROUP:])
            return (j - 1, least_decayed(cs_)) + cs_ + accs_

        st = lax.while_loop(cond, body,
                            (i - SB_FIRST_TILES, least_decayed(cs)) + cs + accs)
        accs = st[2 + SB_GROUP:]
        for g in range(SB_GROUP):
            o_ref[pl.ds(start, CHUNK), heads[g]] = _rms(
                accs[g], nw_ref[:, heads[g]]).astype(BF16)
        return carry

    lax.fori_loop(0, nblk, q_block, 0)


def _stick_breaking(rest, nw, b, tp):
    m = rest.shape[0]
    nblk = tp // CHUNK
    width = SB_GROUP * SB_DH
    groups = SB_HEADS // SB_GROUP
    base = 2 * RET_WIDTH // width
    idx = jnp.arange(CHUNK)
    tri = (idx[:, None] > idx[None, :]).astype(BF16)
    tt = jnp.concatenate([tri, jnp.ones((CHUNK, CHUNK), BF16)], axis=1)
    tt = jnp.concatenate([tt, tt], axis=0)
    return pl.pallas_call(
        functools.partial(_sb_kernel, nblk=nblk),
        grid=(b, groups),
        in_specs=[
            pl.BlockSpec((tp, width), lambda i, h: (i, base + h),
                         pipeline_mode=pl.Buffered(1)),
            pl.BlockSpec((tp, width), lambda i, h: (i, base + groups + h),
                         pipeline_mode=pl.Buffered(1)),
            pl.BlockSpec((tp, width), lambda i, h: (i, base + 2 * groups + h),
                         pipeline_mode=pl.Buffered(1)),
            pl.BlockSpec((2 * CHUNK, 2 * CHUNK), lambda i, h: (0, 0)),
            pl.BlockSpec((1, width), lambda i, h: (0, h)),
        ],
        out_specs=pl.BlockSpec((tp, width), lambda i, h: (i, h)),
        out_shape=jax.ShapeDtypeStruct((m, SB_WIDTH), BF16),
        compiler_params=_params("parallel", "parallel"),
        name="stick_breaking",
    )(rest, rest, rest, tt, nw)


def _out_proj_kernel(r_ref, s_ref, w1_ref, w2_ref, o_ref):
    o_ref[...] = (jnp.dot(r_ref[...], w1_ref[...], preferred_element_type=F32)
                  + jnp.dot(s_ref[...], w2_ref[...], preferred_element_type=F32)
                  ).astype(o_ref.dtype)


def _out_proj(ret_out, sb_out, w_out, tp):
    m = ret_out.shape[0]
    d = w_out.shape[1]
    bm = _largest_divisor(tp, 1056, BF16_SUBLANES)
    bn = 1024
    return pl.pallas_call(
        _out_proj_kernel,
        grid=(m // bm, d // bn),
        in_specs=[
            pl.BlockSpec((bm, RET_WIDTH), lambda i, j: (i, 0)),
            pl.BlockSpec((bm, SB_WIDTH), lambda i, j: (i, 0)),
            pl.BlockSpec((RET_WIDTH, bn), lambda i, j: (0, j)),
            pl.BlockSpec((SB_WIDTH, bn), lambda i, j: (1, j)),
        ],
        out_specs=pl.BlockSpec((bm, bn), lambda i, j: (i, j)),
        out_shape=jax.ShapeDtypeStruct((m, d), BF16),
        compiler_params=_params("parallel", "arbitrary"),
        name="out_proj",
    )(ret_out, sb_out, w_out, w_out)


def _post_attn_kernel(a_ref, x_ref, meta_ref, w1_ref, w2_ref, hn_ref):
    h = _padded_rows(x_ref, meta_ref, pl.program_id(1))
    h1 = h + _rms(a_ref[...].astype(F32), w1_ref[...])
    hn_ref[...] = _rms(h1, w2_ref[...]).astype(BF16)


def _post_attn(a, x, meta, w_post, w_pre):
    b, tp, d = a.shape
    vec = pl.BlockSpec((1, d), lambda i, r: (0, 0))
    return pl.pallas_call(
        _post_attn_kernel,
        grid=(b // ROW_BATCH, tp // CHUNK),
        in_specs=[_row_spec(d), _row_spec(d, -1),
                  pl.BlockSpec((N_META, d), lambda i, r: (0, 0)), vec, vec],
        out_specs=_row_spec(d),
        out_shape=jax.ShapeDtypeStruct((b, tp, d), BF16),
        compiler_params=_params("parallel", "arbitrary"),
        name="post_attn",
    )(a, x, meta, w_post, w_pre)


HALO = BF16_SUBLANES


FFN_SUB_ROWS = 352


def _ffn_up_kernel(x_ref, xh_ref, wg_ref, wu_ref, cw_ref, cb_ref, o_ref, g_ref):
    bm = x_ref.shape[0]
    sub = FFN_SUB_ROWS if bm % FFN_SUB_ROWS == 0 else bm
    wg = wg_ref[...]
    wu = wu_ref[...]
    g_ref[:HALO, :] = jnp.dot(xh_ref[...], wg, preferred_element_type=F32)
    for r in range(bm // sub):
        x = x_ref[r * sub:(r + 1) * sub, :]
        lo = HALO + r * sub
        g_ref[lo:lo + sub, :] = jnp.dot(x, wg, preferred_element_type=F32)
        up = jnp.dot(x, wu, preferred_element_type=F32)
        ext = g_ref[lo - 8:lo + sub, :]
        prev1 = pltpu.roll(ext, 1, axis=0)
        prev2 = pltpu.roll(prev1, 1, axis=0)
        conv = cb_ref[...] + prev2[8:] * cw_ref[0:1, :]
        conv = conv + prev1[8:] * cw_ref[1:2, :]
        conv = conv + ext[8:] * cw_ref[2:3, :]
        o_ref[r * sub:(r + 1) * sub, :] = (conv * jax.nn.sigmoid(conv) * up).astype(BF16)


def _ffn_up(hn, w_up, conv_w, conv_b, tp):
    m, d = hn.shape
    dff = w_up.shape[1] // 2
    bm = _largest_divisor(tp, 2112, BF16_SUBLANES)
    bn = 256
    halo_blocks = bm // HALO
    up_base = dff // bn
    return pl.pallas_call(
        _ffn_up_kernel,
        grid=(m // bm, dff // bn),
        in_specs=[
            pl.BlockSpec((bm, d), lambda i, j: (i, 0), pipeline_mode=pl.Buffered(1)),
            pl.BlockSpec((HALO, d), lambda i, j: (jnp.maximum(i * halo_blocks - 1, 0), 0)),
            pl.BlockSpec((d, bn), lambda i, j: (0, j)),
            pl.BlockSpec((d, bn), lambda i, j: (0, j + up_base)),
            pl.BlockSpec((CONV_W, bn), lambda i, j: (0, j)),
            pl.BlockSpec((1, bn), lambda i, j: (0, j)),
        ],
        out_specs=pl.BlockSpec((bm, bn), lambda i, j: (i, j)),
        out_shape=jax.ShapeDtypeStruct((m, dff), BF16),
        scratch_shapes=[pltpu.VMEM((bm + HALO, bn), F32)],
        compiler_params=_params("parallel", "arbitrary"),
        name="ffn_up",
    )(hn, hn, w_up, w_up, conv_w, conv_b)


def _ffn_down(act, w_down, tp):
    m, dff = act.shape
    d = w_down.shape[1]
    bm = _largest_divisor(tp, 528, BF16_SUBLANES)
    bn = 512
    return pl.pallas_call(
        _proj_plain_kernel,
        grid=(m // bm, d // bn),
        in_specs=[
            pl.BlockSpec((bm, dff), lambda i, j: (i, 0)),
            pl.BlockSpec((dff, bn), lambda i, j: (0, j)),
        ],
        out_specs=pl.BlockSpec((bm, bn), lambda i, j: (i, j)),
        out_shape=jax.ShapeDtypeStruct((m, d), BF16),
        compiler_params=_params("parallel", "arbitrary"),
        name="ffn_down",
    )(act, w_down)


def _final_kernel(f_ref, a_ref, x_ref, w1_ref, w2_ref, o_ref):
    h1 = x_ref[...] + _rms(a_ref[...].astype(F32), w1_ref[...])
    o_ref[...] = h1 + _rms(f_ref[...].astype(F32), w2_ref[...])


def _final(f, a, x, w_post, w_final):
    b, tp, d = f.shape
    vec = pl.BlockSpec((1, d), lambda i, r: (0, 0))
    return pl.pallas_call(
        _final_kernel,
        grid=(b // ROW_BATCH, tp // CHUNK - 1),
        in_specs=[_row_spec(d, 1), _row_spec(d, 1), _row_spec(d), vec, vec],
        out_specs=_row_spec(d),
        out_shape=jax.ShapeDtypeStruct((b, tp - CHUNK, d), F32),
        compiler_params=_params("parallel", "parallel"),
        name="final",
    )(f, a, x, w_post, w_final)


def _rotary_tables(tp):
    half = RET_DK // 2
    pos = (jnp.arange(tp) - N_PAD).astype(F32)
    inv = ROPE_BASE ** (-jnp.arange(half, dtype=F32) / half)
    ang = pos[:, None] * inv[None, :]
    return jnp.cos(ang), jnp.sin(ang)


def kernel(x, meta_tokens, attn_pre_norm_w, w_in, ret_gn_w, sb_norm_w, w_out,
           attn_post_norm_w, ffn_pre_norm_w, w_up, conv_w, conv_b, w_down, ffn_post_norm_w):
    b, seq, d = x.shape
    assert seq % CHUNK == 0 and w_in.shape[0] == 1
    tp = seq + CHUNK
    m = b * tp

    hn = _prep(x, meta_tokens, attn_pre_norm_w).reshape(m, d)

    cos, sin = _rotary_tables(tp)
    qk, rest, (w_down_bf,), (w_up_bf,) = _in_proj(
        hn, w_in[0].astype(BF16), cos, sin, tp, [w_down], [w_up])
    ret_out, (w_out_bf,) = _retention(qk, rest, ret_gn_w, b, tp, [w_out])
    sb_out = _stick_breaking(rest, sb_norm_w, b, tp)
    a = _out_proj(ret_out, sb_out, w_out_bf, tp)
    a = a.reshape(b, tp, d)
    hn2 = _post_attn(a, x, meta_tokens, attn_post_norm_w, ffn_pre_norm_w)

    act = _ffn_up(hn2.reshape(m, d), w_up_bf, conv_w[0], conv_b, tp)
    f = _ffn_down(act, w_down_bf, tp)
    return _final(f.reshape(b, tp, d), a, x, attn_post_norm_w, ffn_post_norm_w)
```

```python
import functools
import math

import numpy as np
import jax
import jax.numpy as jnp
from jax import lax
from jax.experimental import pallas as pl
from jax.experimental.pallas import tpu as pltpu

N_META = 16
CHUNK = 128
N_PAD = CHUNK - N_META
RET_HEADS = 8
RET_DK = 256
RET_DV = 256
SB_HEADS = 16
SB_DH = 128
RET_WIDTH = RET_HEADS * RET_DV
RET_QK_WIDTH = RET_HEADS * RET_DK
SB_WIDTH = SB_HEADS * SB_DH
CONV_W = 3
EPS = 1e-6
ROPE_BASE = 10000.0

BF16_SUBLANES = 16
VMEM_LIMIT_BYTES = 58 * 1024 * 1024

F32 = jnp.float32
BF16 = jnp.bfloat16

_NT = (((1,), (1,)), ((), ()))
_TN = (((0,), (0,)), ((), ()))


def _params(*semantics):
    return pltpu.CompilerParams(dimension_semantics=semantics,
                                vmem_limit_bytes=VMEM_LIMIT_BYTES)


def _largest_divisor(n, target, multiple):
    best = None
    for d in range(multiple, min(n, target) + 1, multiple):
        if n % d == 0:
            best = d
    assert best is not None, (n, target, multiple)
    return best


def _rms(v, w):
    ms = jnp.mean(v * v, axis=-1, keepdims=True)
    return v * lax.rsqrt(ms + EPS) * w


ROW_BATCH = 2


def _padded_rows(x_ref, meta_ref, r):
    d = x_ref.shape[-1]
    first = jnp.concatenate([jnp.zeros((N_PAD, d), F32), meta_ref[...]], axis=0)
    return jnp.where(r == 0, first[None], x_ref[...])


def _row_spec(d, shift=0):
    return pl.BlockSpec((ROW_BATCH, CHUNK, d),
                        lambda i, r: (i, jnp.maximum(r + shift, 0), 0))


def _prep_kernel(x_ref, meta_ref, w_ref, hn_ref):
    h = _padded_rows(x_ref, meta_ref, pl.program_id(1))
    hn_ref[...] = _rms(h, w_ref[...]).astype(BF16)


def _prep(x, meta, w):
    b, seq, d = x.shape
    nblk = seq // CHUNK + 1
    tp = nblk * CHUNK
    return pl.pallas_call(
        _prep_kernel,
        grid=(b // ROW_BATCH, nblk),
        in_specs=[
            _row_spec(d, -1),
            pl.BlockSpec((N_META, d), lambda i, r: (0, 0)),
            pl.BlockSpec((1, d), lambda i, r: (0, 0)),
        ],
        out_specs=_row_spec(d),
        out_shape=jax.ShapeDtypeStruct((b, tp, d), BF16),
        compiler_params=_params("parallel", "arbitrary"),
        name="prep",
    )(x, meta, w)


CAST_BLOCK_BYTES = 3 * 1024 * 1024


def _cast_plan(rows, cols, steps):
    best = None
    for br in range(BF16_SUBLANES, rows + 1, BF16_SUBLANES):
        if rows % br:
            continue
        for bc in range(128, cols + 1, 128):
            if cols % bc or br * bc * 4 > CAST_BLOCK_BYTES:
                continue
            if (rows // br) * (cols // bc) <= steps and (best is None or br * bc < best[0] * best[1]):
                best = (br, bc)
    return best


def _with_casts(kernel_fn, n_in, n_out, weights, steps, nj):
    plans, in_specs, out_specs, out_shapes, operands = [], [], [], [], []
    start = 0
    for w in weights:
        _, rows, cols = w.shape
        plan = _cast_plan(rows, cols, steps - start)
        plans.append(plan)
        if plan is None:
            continue
        br, bc = plan
        ncb = cols // bc
        nb = (rows // br) * ncb

        def block(i, j, start=start, nb=nb, ncb=ncb):
            t = jnp.clip(i * nj + j - start, 0, nb - 1)
            return t // ncb, t % ncb

        in_specs.append(pl.BlockSpec((None, br, bc), lambda i, j, block=block: (0,) + block(i, j)))
        out_specs.append(pl.BlockSpec((br, bc), block))
        out_shapes.append(jax.ShapeDtypeStruct((rows, cols), BF16))
        operands.append(w)
        start += nb
    n_cast = len(operands)

    def wrapped(*refs):
        ins, rest = refs[:n_in], refs[n_in:]
        cast_in, rest = rest[:n_cast], rest[n_cast:]
        outs, rest = rest[:n_out], rest[n_out:]
        cast_out, scratch = rest[:n_cast], rest[n_cast:]
        kernel_fn(*ins, *outs, *scratch)
        for ci, co in zip(cast_in, cast_out):
            co[...] = ci[...].astype(BF16)

    def finish(cast_results):
        it = iter(cast_results)
        return [w[0].astype(BF16) if p is None else next(it)
                for w, p in zip(weights, plans)]

    return wrapped, in_specs, out_specs, out_shapes, operands, finish


def _proj_rot_kernel(x_ref, w_ref, cos_ref, sin_ref, o_ref, *, n_q_blocks):
    j = pl.program_id(1)
    acc = jnp.dot(x_ref[...], w_ref[...], preferred_element_type=F32)
    scale = jnp.where(j < n_q_blocks, RET_DK ** -0.5, 1.0).astype(F32)
    cos = cos_ref[...]
    sin = sin_ref[...]
    half = RET_DK // 2
    for h in range(o_ref.shape[1] // RET_DK):
        lo = h * RET_DK
        x1 = acc[:, lo:lo + half]
        x2 = acc[:, lo + half:lo + RET_DK]
        o_ref[:, lo:lo + half] = ((x1 * cos - x2 * sin) * scale).astype(BF16)
        o_ref[:, lo + half:lo + RET_DK] = ((x1 * sin + x2 * cos) * scale).astype(BF16)


def _proj_plain_kernel(x_ref, w_ref, o_ref):
    o_ref[...] = jnp.dot(x_ref[...], w_ref[...],
                         preferred_element_type=F32).astype(o_ref.dtype)


def _in_proj(hn, w_in, cos, sin, tp, later_weights_rot, later_weights_rest):
    m, d = hn.shape
    bm = _largest_divisor(tp, 1056, BF16_SUBLANES)
    bn = 1024
    pos_blocks = tp // bm
    n_rot = 2 * RET_QK_WIDTH
    n_rest = w_in.shape[1] - n_rot

    grid = (m // bm, n_rot // bn)
    body, c_in, c_out, c_shape, c_ops, finish_rot = _with_casts(
        functools.partial(_proj_rot_kernel, n_q_blocks=RET_QK_WIDTH // bn),
        4, 1, later_weights_rot, grid[0] * grid[1], grid[1])
    qk, *cast_rot = pl.pallas_call(
        body,
        grid=grid,
        in_specs=[
            pl.BlockSpec((bm, d), lambda i, j: (i, 0)),
            pl.BlockSpec((d, bn), lambda i, j: (0, j)),
            pl.BlockSpec((bm, RET_DK // 2), lambda i, j: (i % pos_blocks, 0)),
            pl.BlockSpec((bm, RET_DK // 2), lambda i, j: (i % pos_blocks, 0)),
        ] + c_in,
        out_specs=[pl.BlockSpec((bm, bn), lambda i, j: (i, j))] + c_out,
        out_shape=[jax.ShapeDtypeStruct((m, n_rot), BF16)] + c_shape,
        compiler_params=_params("arbitrary", "arbitrary"),
        name="in_proj_rot",
    )(hn, w_in, cos, sin, *c_ops)

    rot_blocks = n_rot // bn
    grid = (m // bm, n_rest // bn)
    body, c_in, c_out, c_shape, c_ops, finish_rest = _with_casts(
        _proj_plain_kernel, 2, 1, later_weights_rest, grid[0] * grid[1], grid[1])
    rest, *cast_rest = pl.pallas_call(
        body,
        grid=grid,
        in_specs=[
            pl.BlockSpec((bm, d), lambda i, j: (i, 0)),
            pl.BlockSpec((d, bn), lambda i, j: (0, j + rot_blocks)),
        ] + c_in,
        out_specs=[pl.BlockSpec((bm, bn), lambda i, j: (i, j))] + c_out,
        out_shape=[jax.ShapeDtypeStruct((m, n_rest), BF16)] + c_shape,
        compiler_params=_params("arbitrary", "arbitrary"),
        name="in_proj_rest",
    )(hn, w_in, *c_ops)
    return qk, rest, finish_rot(cast_rot), finish_rest(cast_rest)


def _ret_log_gamma():
    return np.log(1.0 - 2.0 ** (-5.0 - np.arange(RET_HEADS, dtype=np.float64)))


def _ret_kernel(q_ref, k_ref, v_ref, g_ref, intra_ref, qd_ref, kd_ref, gn_ref,
                o_ref, state_ref):
    c = pl.program_id(1)

    @pl.when(c == 0)
    def _():
        state_ref[...] = jnp.zeros(state_ref.shape, F32)

    chunk_decay = np.exp(CHUNK * _ret_log_gamma()).astype(np.float32)
    for n in range(q_ref.shape[0]):
        for h in range(RET_HEADS):
            sl = slice(h * RET_DK, (h + 1) * RET_DK)
            q = q_ref[n, :, sl]
            k = k_ref[n, :, sl]
            v = v_ref[n, :, sl]
            s = lax.dot_general(q, k, _NT, preferred_element_type=F32) * intra_ref[h]
            intra = jnp.dot(s.astype(BF16), v, preferred_element_type=F32)
            st = state_ref[n, h]
            qd = qd_ref[h]
            kd = kd_ref[h]
            cross = jnp.dot(q, st.astype(BF16), preferred_element_type=F32)
            cross = cross * jnp.concatenate([qd, qd], axis=1)
            kdk = (k.astype(F32) * jnp.concatenate([kd, kd], axis=1)).astype(BF16)
            upd = lax.dot_general(kdk, v, _TN, preferred_element_type=F32)
            state_ref[n, h] = st * float(chunk_decay[h]) + upd
            y = intra + cross
            yn = _rms(y, gn_ref[:, sl])
            g = g_ref[n, :, sl].astype(F32)
            o_ref[n, :, sl] = (g * jax.nn.sigmoid(g) * yn).astype(BF16)


def _retention(qk, rest, gn_w, b, tp, later_weights):
    m = qk.shape[0]
    nblk = tp // CHUNK
    lg = jnp.log(1.0 - 2.0 ** (-5.0 - jnp.arange(RET_HEADS, dtype=F32)))
    idx = jnp.arange(CHUNK, dtype=F32)
    diff = idx[:, None] - idx[None, :]
    intra = jnp.where(diff[None] >= 0,
                      jnp.exp(jnp.maximum(diff, 0.0)[None] * lg[:, None, None]), 0.0)
    qd = jnp.exp((idx[None, :] + 1.0) * lg[:, None])
    kd = jnp.exp((CHUNK - 1.0 - idx[None, :]) * lg[:, None])
    qd = jnp.broadcast_to(qd[:, :, None], (RET_HEADS, CHUNK, CHUNK))
    kd = jnp.broadcast_to(kd[:, :, None], (RET_HEADS, CHUNK, CHUNK))
    table = pl.BlockSpec((RET_HEADS, CHUNK, CHUNK), lambda i, c: (0, 0, 0))
    grid = (b // ROW_BATCH, nblk)
    body, c_in, c_out, c_shape, c_ops, finish = _with_casts(
        _ret_kernel, 8, 1, later_weights, grid[0] * grid[1], nblk)

    def cols(block, width):
        return pl.BlockSpec((ROW_BATCH, CHUNK, width), lambda i, c: (i, c, block))

    qk = qk.reshape(b, tp, -1)
    rest = rest.reshape(b, tp, -1)
    out, *cast = pl.pallas_call(
        body,
        grid=grid,
        in_specs=[cols(0, RET_QK_WIDTH), cols(1, RET_QK_WIDTH),
                  cols(0, RET_WIDTH), cols(1, RET_WIDTH),
                  table, table, table,
                  pl.BlockSpec((1, RET_WIDTH), lambda i, c: (0, 0))] + c_in,
        out_specs=[cols(0, RET_WIDTH)] + c_out,
        out_shape=[jax.ShapeDtypeStruct((b, tp, RET_WIDTH), BF16)] + c_shape,
        scratch_shapes=[pltpu.VMEM((ROW_BATCH, RET_HEADS, RET_DK, RET_DV), F32)],
        compiler_params=_params("arbitrary", "arbitrary"),
        name="retention",
    )(qk, qk, rest, rest, intra, qd, kd, gn_w, *c_ops)
    return out.reshape(m, RET_WIDTH), finish(cast)


SB_GROUP = 8
SB_UNDERFLOW = 110.0
SB_FIRST_TILES = 3
LOG2E = 1.4426950408889634


def _sb_kernel(q_ref, k_ref, v_ref, tt_ref, nw_ref, o_ref, *, nblk):
    scale = LOG2E / math.sqrt(SB_DH)
    row = lax.broadcasted_iota(jnp.int32, (CHUNK, CHUNK), 0)
    col = lax.broadcasted_iota(jnp.int32, (CHUNK, CHUNK), 1)
    heads = [slice(g * SB_DH, (g + 1) * SB_DH) for g in range(SB_GROUP)]

    def step(qs, tiles, cs, accs):
        vs, masks, log_beta, split = [], [], [], []
        for j, mask in tiles:
            start = pl.multiple_of(jnp.maximum(j, 0) * CHUNK, CHUNK)
            k = k_ref[pl.ds(start, CHUNK), :]
            vs.append(v_ref[pl.ds(start, CHUNK), :])
            masks.append(mask)
            for g in range(SB_GROUP):
                zs = lax.dot_general(qs[g], k[:, heads[g]], _NT,
                                     preferred_element_type=F32) * scale
                sp = jnp.maximum(zs, 0.0) + jnp.log2(1.0 + jnp.exp2(-jnp.abs(zs)))
                log_beta.append(zs - sp)
                log_keep = jnp.where(mask, -sp, 0.0)
                hi = log_keep.astype(BF16)
                lo = (log_keep - hi.astype(F32)).astype(BF16)
                split.append(jnp.concatenate([hi, lo], axis=1))
        sums = jnp.dot(jnp.concatenate(split, axis=0), tt_ref[...],
                       preferred_element_type=F32)
        cs, accs = list(cs), list(accs)
        for t in range(len(tiles)):
            for g in range(SB_GROUP):
                n = t * SB_GROUP + g
                s = sums[n * CHUNK:(n + 1) * CHUNK]
                after = s[:, :CHUNK] + cs[g]
                w = jnp.where(masks[t], jnp.exp2(log_beta[n] + after), 0.0)
                accs[g] = accs[g] + jnp.dot(w.astype(BF16), vs[t][:, heads[g]],
                                            preferred_element_type=F32)
                cs[g] = cs[g] + s[:, CHUNK:]
        return tuple(cs), tuple(accs)

    def least_decayed(cs):
        return jnp.max(functools.reduce(jnp.maximum, cs))

    def valid_keys(j):
        return j * CHUNK + col >= N_PAD

    def q_block(i, carry):
        start = pl.multiple_of(i * CHUNK, CHUNK)
        q = q_ref[pl.ds(start, CHUNK), :]
        qs = [q[:, heads[g]] for g in range(SB_GROUP)]
        zero = (jnp.zeros((CHUNK, CHUNK), F32),) * SB_GROUP
        first = [(i, jnp.logical_and(col < row, valid_keys(i)))]
        first += [(i - t, valid_keys(i - t)) for t in range(1, SB_FIRST_TILES)]
        cs, accs = step(qs, first, zero, zero)

        def cond(st):
            return jnp.logical_and(st[0] >= 0, st[1] > -SB_UNDERFLOW * LOG2E)

        def body(st):
            j = st[0]
            cs_, accs_ = step(qs, [(j, valid_keys(j))], st[2:2 + SB_GROUP],
                              st[2 + SB_GROUP:])
            return (j - 1, least_decayed(cs_)) + cs_ + accs_

        st = lax.while_loop(cond, body,
                            (i - SB_FIRST_TILES, least_decayed(cs)) + cs + accs)
        accs = st[2 + SB_GROUP:]
        for g in range(SB_GROUP):
            o_ref[pl.ds(start, CHUNK), heads[g]] = _rms(
                accs[g], nw_ref[:, heads[g]]).astype(BF16)
        return carry

    lax.fori_loop(0, nblk, q_block, 0)


def _stick_breaking(rest, nw, b, tp):
    m = rest.shape[0]
    nblk = tp // CHUNK
    width = SB_GROUP * SB_DH
    groups = SB_HEADS // SB_GROUP
    base = 2 * RET_WIDTH // width
    idx = jnp.arange(CHUNK)
    tri = (idx[:, None] > idx[None, :]).astype(BF16)
    tt = jnp.concatenate([tri, jnp.ones((CHUNK, CHUNK), BF16)], axis=1)
    tt = jnp.concatenate([tt, tt], axis=0)
    return pl.pallas_call(
        functools.partial(_sb_kernel, nblk=nblk),
        grid=(b, groups),
        in_specs=[
            pl.BlockSpec((tp, width), lambda i, h: (i, base + h),
                         pipeline_mode=pl.Buffered(1)),
            pl.BlockSpec((tp, width), lambda i, h: (i, base + groups + h),
                         pipeline_mode=pl.Buffered(1)),
            pl.BlockSpec((tp, width), lambda i, h: (i, base + 2 * groups + h),
                         pipeline_mode=pl.Buffered(1)),
            pl.BlockSpec((2 * CHUNK, 2 * CHUNK), lambda i, h: (0, 0)),
            pl.BlockSpec((1, width), lambda i, h: (0, h)),
        ],
        out_specs=pl.BlockSpec((tp, width), lambda i, h: (i, h)),
        out_shape=jax.ShapeDtypeStruct((m, SB_WIDTH), BF16),
        compiler_params=_params("parallel", "parallel"),
        name="stick_breaking",
    )(rest, rest, rest, tt, nw)


def _out_proj_kernel(r_ref, s_ref, w1_ref, w2_ref, o_ref):
    o_ref[...] = (jnp.dot(r_ref[...], w1_ref[...], preferred_element_type=F32)
                  + jnp.dot(s_ref[...], w2_ref[...], preferred_element_type=F32)
                  ).astype(o_ref.dtype)


def _out_proj(ret_out, sb_out, w_out, tp):
    m = ret_out.shape[0]
    d = w_out.shape[1]
    bm = _largest_divisor(tp, 1056, BF16_SUBLANES)
    bn = 1024
    return pl.pallas_call(
        _out_proj_kernel,
        grid=(m // bm, d // bn),
        in_specs=[
            pl.BlockSpec((bm, RET_WIDTH), lambda i, j: (i, 0)),
            pl.BlockSpec((bm, SB_WIDTH), lambda i, j: (i, 0)),
            pl.BlockSpec((RET_WIDTH, bn), lambda i, j: (0, j)),
            pl.BlockSpec((SB_WIDTH, bn), lambda i, j: (1, j)),
        ],
        out_specs=pl.BlockSpec((bm, bn), lambda i, j: (i, j)),
        out_shape=jax.ShapeDtypeStruct((m, d), BF16),
        compiler_params=_params("parallel", "arbitrary"),
        name="out_proj",
    )(ret_out, sb_out, w_out, w_out)


def _post_attn_kernel(a_ref, x_ref, meta_ref, w1_ref, w2_ref, hn_ref):
    h = _padded_rows(x_ref, meta_ref, pl.program_id(1))
    h1 = h + _rms(a_ref[...].astype(F32), w1_ref[...])
    hn_ref[...] = _rms(h1, w2_ref[...]).astype(BF16)


def _post_attn(a, x, meta, w_post, w_pre):
    b, tp, d = a.shape
    vec = pl.BlockSpec((1, d), lambda i, r: (0, 0))
    return pl.pallas_call(
        _post_attn_kernel,
        grid=(b // ROW_BATCH, tp // CHUNK),
        in_specs=[_row_spec(d), _row_spec(d, -1),
                  pl.BlockSpec((N_META, d), lambda i, r: (0, 0)), vec, vec],
        out_specs=_row_spec(d),
        out_shape=jax.ShapeDtypeStruct((b, tp, d), BF16),
        compiler_params=_params("parallel", "arbitrary"),
        name="post_attn",
    )(a, x, meta, w_post, w_pre)


HALO = BF16_SUBLANES


FFN_SUB_ROWS = 352


def _ffn_up_kernel(x_ref, xh_ref, wg_ref, wu_ref, cw_ref, cb_ref, o_ref, g_ref):
    bm = x_ref.shape[0]
    sub = FFN_SUB_ROWS if bm % FFN_SUB_ROWS == 0 else bm
    wg = wg_ref[...]
    wu = wu_ref[...]
    g_ref[:HALO, :] = jnp.dot(xh_ref[...], wg, preferred_element_type=F32)
    for r in range(bm // sub):
        x = x_ref[r * sub:(r + 1) * sub, :]
        lo = HALO + r * sub
        g_ref[lo:lo + sub, :] = jnp.dot(x, wg, preferred_element_type=F32)
        up = jnp.dot(x, wu, preferred_element_type=F32)
        ext = g_ref[lo - 8:lo + sub, :]
        prev1 = pltpu.roll(ext, 1, axis=0)
        prev2 = pltpu.roll(prev1, 1, axis=0)
        conv = cb_ref[...] + prev2[8:] * cw_ref[0:1, :]
        conv = conv + prev1[8:] * cw_ref[1:2, :]
        conv = conv + ext[8:] * cw_ref[2:3, :]
        o_ref[r * sub:(r + 1) * sub, :] = (conv * jax.nn.sigmoid(conv) * up).astype(BF16)


def _ffn_up(hn, w_up, conv_w, conv_b, tp):
    m, d = hn.shape
    dff = w_up.shape[1] // 2
    bm = _largest_divisor(tp, 4224, BF16_SUBLANES)
    bn = 256
    halo_blocks = bm // HALO
    up_base = dff // bn
    return pl.pallas_call(
        _ffn_up_kernel,
        grid=(m // bm, dff // bn),
        in_specs=[
            pl.BlockSpec((bm, d), lambda i, j: (i, 0), pipeline_mode=pl.Buffered(1)),
            pl.BlockSpec((HALO, d), lambda i, j: (jnp.maximum(i * halo_blocks - 1, 0), 0)),
            pl.BlockSpec((d, bn), lambda i, j: (0, j)),
            pl.BlockSpec((d, bn), lambda i, j: (0, j + up_base)),
            pl.BlockSpec((CONV_W, bn), lambda i, j: (0, j)),
            pl.BlockSpec((1, bn), lambda i, j: (0, j)),
        ],
        out_specs=pl.BlockSpec((bm, bn), lambda i, j: (i, j)),
        out_shape=jax.ShapeDtypeStruct((m, dff), BF16),
        scratch_shapes=[pltpu.VMEM((bm + HALO, bn), F32)],
        compiler_params=_params("parallel", "arbitrary"),
        name="ffn_up",
    )(hn, hn, w_up, w_up, conv_w, conv_b)


def _ffn_down(act, w_down, tp):
    m, dff = act.shape
    d = w_down.shape[1]
    bm = _largest_divisor(tp, 528, BF16_SUBLANES)
    bn = 512
    return pl.pallas_call(
        _proj_plain_kernel,
        grid=(m // bm, d // bn),
        in_specs=[
            pl.BlockSpec((bm, dff), lambda i, j: (i, 0)),
            pl.BlockSpec((dff, bn), lambda i, j: (0, j)),
        ],
        out_specs=pl.BlockSpec((bm, bn), lambda i, j: (i, j)),
        out_shape=jax.ShapeDtypeStruct((m, d), BF16),
        compiler_params=_params("parallel", "arbitrary"),
        name="ffn_down",
    )(act, w_down)


def _final_kernel(f_ref, a_ref, x_ref, w1_ref, w2_ref, o_ref):
    h1 = x_ref[...] + _rms(a_ref[...].astype(F32), w1_ref[...])
    o_ref[...] = h1 + _rms(f_ref[...].astype(F32), w2_ref[...])


def _final(f, a, x, w_post, w_final):
    b, tp, d = f.shape
    vec = pl.BlockSpec((1, d), lambda i, r: (0, 0))
    return pl.pallas_call(
        _final_kernel,
        grid=(b // ROW_BATCH, tp // CHUNK - 1),
        in_specs=[_row_spec(d, 1), _row_spec(d, 1), _row_spec(d), vec, vec],
        out_specs=_row_spec(d),
        out_shape=jax.ShapeDtypeStruct((b, tp - CHUNK, d), F32),
        compiler_params=_params("parallel", "parallel"),
        name="final",
    )(f, a, x, w_post, w_final)


def _rotary_tables(tp):
    half = RET_DK // 2
    pos = (jnp.arange(tp) - N_PAD).astype(F32)
    inv = ROPE_BASE ** (-jnp.arange(half, dtype=F32) / half)
    ang = pos[:, None] * inv[None, :]
    return jnp.cos(ang), jnp.sin(ang)


def kernel(x, meta_tokens, attn_pre_norm_w, w_in, ret_gn_w, sb_norm_w, w_out,
           attn_post_norm_w, ffn_pre_norm_w, w_up, conv_w, conv_b, w_down, ffn_post_norm_w):
    b, seq, d = x.shape
    assert seq % CHUNK == 0 and w_in.shape[0] == 1
    tp = seq + CHUNK
    m = b * tp

    hn = _prep(x, meta_tokens, attn_pre_norm_w).reshape(m, d)

    cos, sin = _rotary_tables(tp)
    qk, rest, (w_down_bf,), (w_up_bf,) = _in_proj(
        hn, w_in[0].astype(BF16), cos, sin, tp, [w_down], [w_up])
    ret_out, (w_out_bf,) = _retention(qk, rest, ret_gn_w, b, tp, [w_out])
    sb_out = _stick_breaking(rest, sb_norm_w, b, tp)
    a = _out_proj(ret_out, sb_out, w_out_bf, tp)
    a = a.reshape(b, tp, d)
    hn2 = _post_attn(a, x, meta_tokens, attn_post_norm_w, ffn_pre_norm_w)

    act = _ffn_up(hn2.reshape(m, d), w_up_bf, conv_w[0], conv_b, tp)
    f = _ffn_down(act, w_down_bf, tp)
    return _final(f.reshape(b, tp, d), a, x, attn_post_norm_w, ffn_post_norm_w)
```

```python
import functools
import math

import numpy as np
import jax
import jax.numpy as jnp
from jax import lax
from jax.experimental import pallas as pl
from jax.experimental.pallas import tpu as pltpu

N_META = 16
CHUNK = 128
N_PAD = CHUNK - N_META
RET_HEADS = 8
RET_DK = 256
RET_DV = 256
SB_HEADS = 16
SB_DH = 128
RET_WIDTH = RET_HEADS * RET_DV
RET_QK_WIDTH = RET_HEADS * RET_DK
SB_WIDTH = SB_HEADS * SB_DH
CONV_W = 3
EPS = 1e-6
ROPE_BASE = 10000.0

BF16_SUBLANES = 16
HALO = BF16_SUBLANES
assert CONV_W - 1 <= HALO <= N_META
VMEM_LIMIT_BYTES = 58 * 1024 * 1024

F32 = jnp.float32
BF16 = jnp.bfloat16

_NT = (((1,), (1,)), ((), ()))
_TN = (((0,), (0,)), ((), ()))


def _params(*semantics):
    return pltpu.CompilerParams(dimension_semantics=semantics,
                                vmem_limit_bytes=VMEM_LIMIT_BYTES)


def _largest_divisor(n, target, multiple):
    best = None
    for d in range(multiple, min(n, target) + 1, multiple):
        if n % d == 0:
            best = d
    assert best is not None, (n, target, multiple)
    return best


def _rms(v, w):
    ms = jnp.mean(v * v, axis=-1, keepdims=True)
    return v * lax.rsqrt(ms + EPS) * w


ROW_BATCH = 2


def _padded_rows(x_ref, meta_ref, r):
    d = x_ref.shape[-1]
    first = jnp.concatenate([jnp.zeros((N_PAD, d), F32), meta_ref[...]], axis=0)
    return jnp.where(r == 0, first[None], x_ref[...])


def _row_spec(d, shift=0):
    return pl.BlockSpec((ROW_BATCH, CHUNK, d),
                        lambda i, r: (i, jnp.maximum(r + shift, 0), 0))


def _prep_kernel(x_ref, meta_ref, w_ref, hn_ref):
    h = _padded_rows(x_ref, meta_ref, pl.program_id(1))
    hn_ref[...] = _rms(h, w_ref[...]).astype(BF16)


def _prep(x, meta, w):
    b, seq, d = x.shape
    nblk = seq // CHUNK + 1
    tp = nblk * CHUNK
    return pl.pallas_call(
        _prep_kernel,
        grid=(b // ROW_BATCH, nblk),
        in_specs=[
            _row_spec(d, -1),
            pl.BlockSpec((N_META, d), lambda i, r: (0, 0)),
            pl.BlockSpec((1, d), lambda i, r: (0, 0)),
        ],
        out_specs=_row_spec(d),
        out_shape=jax.ShapeDtypeStruct((b, tp, d), BF16),
        compiler_params=_params("parallel", "arbitrary"),
        name="prep",
    )(x, meta, w)


CAST_BLOCK_BYTES = 3 * 1024 * 1024


def _cast_plan(rows, cols, steps):
    best = None
    for br in range(BF16_SUBLANES, rows + 1, BF16_SUBLANES):
        if rows % br:
            continue
        for bc in range(128, cols + 1, 128):
            if cols % bc or br * bc * 4 > CAST_BLOCK_BYTES:
                continue
            if (rows // br) * (cols // bc) <= steps and (best is None or br * bc < best[0] * best[1]):
                best = (br, bc)
    return best


def _with_casts(kernel_fn, n_in, n_out, weights, steps, nj):
    plans, in_specs, out_specs, out_shapes, operands = [], [], [], [], []
    start = 0
    for w in weights:
        _, rows, cols = w.shape
        plan = _cast_plan(rows, cols, steps - start)
        plans.append(plan)
        if plan is None:
            continue
        br, bc = plan
        ncb = cols // bc
        nb = (rows // br) * ncb

        def block(i, j, start=start, nb=nb, ncb=ncb):
            t = jnp.clip(i * nj + j - start, 0, nb - 1)
            return t // ncb, t % ncb

        in_specs.append(pl.BlockSpec((None, br, bc), lambda i, j, block=block: (0,) + block(i, j)))
        out_specs.append(pl.BlockSpec((br, bc), block))
        out_shapes.append(jax.ShapeDtypeStruct((rows, cols), BF16))
        operands.append(w)
        start += nb
    n_cast = len(operands)

    def wrapped(*refs):
        ins, rest = refs[:n_in], refs[n_in:]
        cast_in, rest = rest[:n_cast], rest[n_cast:]
        outs, rest = rest[:n_out], rest[n_out:]
        cast_out, scratch = rest[:n_cast], rest[n_cast:]
        kernel_fn(*ins, *outs, *scratch)
        for ci, co in zip(cast_in, cast_out):
            co[...] = ci[...].astype(BF16)

    def finish(cast_results):
        it = iter(cast_results)
        return [w[0].astype(BF16) if p is None else next(it)
                for w, p in zip(weights, plans)]

    return wrapped, in_specs, out_specs, out_shapes, operands, finish


def _proj_rot_kernel(x_ref, w_ref, cos_ref, sin_ref, o_ref, *, n_q_blocks):
    j = pl.program_id(1)
    acc = jnp.dot(x_ref[...], w_ref[...], preferred_element_type=F32)
    scale = jnp.where(j < n_q_blocks, RET_DK ** -0.5, 1.0).astype(F32)
    cos = cos_ref[...]
    sin = sin_ref[...]
    half = RET_DK // 2
    for h in range(o_ref.shape[1] // RET_DK):
        lo = h * RET_DK
        x1 = acc[:, lo:lo + half]
        x2 = acc[:, lo + half:lo + RET_DK]
        o_ref[:, lo:lo + half] = ((x1 * cos - x2 * sin) * scale).astype(BF16)
        o_ref[:, lo + half:lo + RET_DK] = ((x1 * sin + x2 * cos) * scale).astype(BF16)


def _proj_plain_kernel(x_ref, w_ref, o_ref):
    o_ref[...] = jnp.dot(x_ref[...], w_ref[...],
                         preferred_element_type=F32).astype(o_ref.dtype)


def _in_proj(hn, w_in, cos, sin, tp, later_weights_rot, later_weights_rest):
    m, d = hn.shape
    bm = _largest_divisor(tp, 1056, BF16_SUBLANES)
    bn = 1024
    pos_blocks = tp // bm
    n_rot = 2 * RET_QK_WIDTH
    n_rest = w_in.shape[1] - n_rot

    grid = (m // bm, n_rot // bn)
    body, c_in, c_out, c_shape, c_ops, finish_rot = _with_casts(
        functools.partial(_proj_rot_kernel, n_q_blocks=RET_QK_WIDTH // bn),
        4, 1, later_weights_rot, grid[0] * grid[1], grid[1])
    qk, *cast_rot = pl.pallas_call(
        body,
        grid=grid,
        in_specs=[
            pl.BlockSpec((bm, d), lambda i, j: (i, 0)),
            pl.BlockSpec((d, bn), lambda i, j: (0, j)),
            pl.BlockSpec((bm, RET_DK // 2), lambda i, j: (i % pos_blocks, 0)),
            pl.BlockSpec((bm, RET_DK // 2), lambda i, j: (i % pos_blocks, 0)),
        ] + c_in,
        out_specs=[pl.BlockSpec((bm, bn), lambda i, j: (i, j))] + c_out,
        out_shape=[jax.ShapeDtypeStruct((m, n_rot), BF16)] + c_shape,
        compiler_params=_params("arbitrary", "arbitrary"),
        name="in_proj_rot",
    )(hn, w_in, cos, sin, *c_ops)

    rot_blocks = n_rot // bn
    grid = (m // bm, n_rest // bn)
    body, c_in, c_out, c_shape, c_ops, finish_rest = _with_casts(
        _proj_plain_kernel, 2, 1, later_weights_rest, grid[0] * grid[1], grid[1])
    rest, *cast_rest = pl.pallas_call(
        body,
        grid=grid,
        in_specs=[
            pl.BlockSpec((bm, d), lambda i, j: (i, 0)),
            pl.BlockSpec((d, bn), lambda i, j: (0, j + rot_blocks)),
        ] + c_in,
        out_specs=[pl.BlockSpec((bm, bn), lambda i, j: (i, j))] + c_out,
        out_shape=[jax.ShapeDtypeStruct((m, n_rest), BF16)] + c_shape,
        compiler_params=_params("arbitrary", "arbitrary"),
        name="in_proj_rest",
    )(hn, w_in, *c_ops)
    return qk, rest, finish_rot(cast_rot), finish_rest(cast_rest)


def _ret_log_gamma():
    return np.log(1.0 - 2.0 ** (-5.0 - np.arange(RET_HEADS, dtype=np.float64)))


def _ret_kernel(q_ref, k_ref, v_ref, g_ref, intra_ref, qd_ref, kd_ref, gn_ref,
                o_ref, state_ref):
    c = pl.program_id(1)

    @pl.when(c == 0)
    def _():
        state_ref[...] = jnp.zeros(state_ref.shape, F32)

    chunk_decay = np.exp(CHUNK * _ret_log_gamma()).astype(np.float32)
    for n in range(q_ref.shape[0]):
        for h in range(RET_HEADS):
            sl = slice(h * RET_DK, (h + 1) * RET_DK)
            q = q_ref[n, :, sl]
            k = k_ref[n, :, sl]
            v = v_ref[n, :, sl]
            s = lax.dot_general(q, k, _NT, preferred_element_type=F32) * intra_ref[h]
            intra = jnp.dot(s.astype(BF16), v, preferred_element_type=F32)
            st = state_ref[n, h]
            qd = qd_ref[h]
            kd = kd_ref[h]
            cross = jnp.dot(q, st.astype(BF16), preferred_element_type=F32)
            cross = cross * jnp.concatenate([qd, qd], axis=1)
            kdk = (k.astype(F32) * jnp.concatenate([kd, kd], axis=1)).astype(BF16)
            upd = lax.dot_general(kdk, v, _TN, preferred_element_type=F32)
            state_ref[n, h] = st * float(chunk_decay[h]) + upd
            y = intra + cross
            yn = _rms(y, gn_ref[:, sl])
            g = g_ref[n, :, sl].astype(F32)
            o_ref[n, :, sl] = (g * jax.nn.sigmoid(g) * yn).astype(BF16)


def _retention(qk, rest, gn_w, b, tp, later_weights):
    m = qk.shape[0]
    nblk = tp // CHUNK
    lg = jnp.log(1.0 - 2.0 ** (-5.0 - jnp.arange(RET_HEADS, dtype=F32)))
    idx = jnp.arange(CHUNK, dtype=F32)
    diff = idx[:, None] - idx[None, :]
    intra = jnp.where(diff[None] >= 0,
                      jnp.exp(jnp.maximum(diff, 0.0)[None] * lg[:, None, None]), 0.0)
    qd = jnp.exp((idx[None, :] + 1.0) * lg[:, None])
    kd = jnp.exp((CHUNK - 1.0 - idx[None, :]) * lg[:, None])
    qd = jnp.broadcast_to(qd[:, :, None], (RET_HEADS, CHUNK, CHUNK))
    kd = jnp.broadcast_to(kd[:, :, None], (RET_HEADS, CHUNK, CHUNK))
    table = pl.BlockSpec((RET_HEADS, CHUNK, CHUNK), lambda i, c: (0, 0, 0))
    grid = (b // ROW_BATCH, nblk)
    body, c_in, c_out, c_shape, c_ops, finish = _with_casts(
        _ret_kernel, 8, 1, later_weights, grid[0] * grid[1], nblk)

    def cols(block, width):
        return pl.BlockSpec((ROW_BATCH, CHUNK, width), lambda i, c: (i, c, block))

    qk = qk.reshape(b, tp, -1)
    rest = rest.reshape(b, tp, -1)
    out, *cast = pl.pallas_call(
        body,
        grid=grid,
        in_specs=[cols(0, RET_QK_WIDTH), cols(1, RET_QK_WIDTH),
                  cols(0, RET_WIDTH), cols(1, RET_WIDTH),
                  table, table, table,
                  pl.BlockSpec((1, RET_WIDTH), lambda i, c: (0, 0))] + c_in,
        out_specs=[cols(0, RET_WIDTH)] + c_out,
        out_shape=[jax.ShapeDtypeStruct((b, tp, RET_WIDTH), BF16)] + c_shape,
        scratch_shapes=[pltpu.VMEM((ROW_BATCH, RET_HEADS, RET_DK, RET_DV), F32)],
        compiler_params=_params("arbitrary", "arbitrary"),
        name="retention",
    )(qk, qk, rest, rest, intra, qd, kd, gn_w, *c_ops)
    return out.reshape(m, RET_WIDTH), finish(cast)


SB_GROUP = 8
SB_UNDERFLOW = 110.0
SB_FIRST_TILES = 3
LOG2E = 1.4426950408889634


def _sb_kernel(q_ref, k_ref, v_ref, tt_ref, nw_ref, o_ref, *, nblk):
    scale = LOG2E / math.sqrt(SB_DH)
    row = lax.broadcasted_iota(jnp.int32, (CHUNK, CHUNK), 0)
    col = lax.broadcasted_iota(jnp.int32, (CHUNK, CHUNK), 1)
    heads = [slice(g * SB_DH, (g + 1) * SB_DH) for g in range(SB_GROUP)]

    def step(qs, tiles, cs, accs):
        vs, masks, log_beta, split = [], [], [], []
        for j, mask in tiles:
            start = pl.multiple_of(jnp.maximum(j, 0) * CHUNK, CHUNK)
            k = k_ref[pl.ds(start, CHUNK), :]
            vs.append(v_ref[pl.ds(start, CHUNK), :])
            masks.append(mask)
            for g in range(SB_GROUP):
                zs = lax.dot_general(qs[g], k[:, heads[g]], _NT,
                                     preferred_element_type=F32) * scale
                sp = jnp.maximum(zs, 0.0) + jnp.log2(1.0 + jnp.exp2(-jnp.abs(zs)))
                log_beta.append(zs - sp)
                log_keep = jnp.where(mask, -sp, 0.0)
                hi = log_keep.astype(BF16)
                lo = (log_keep - hi.astype(F32)).astype(BF16)
                split.append(jnp.concatenate([hi, lo], axis=1))
        sums = jnp.dot(jnp.concatenate(split, axis=0), tt_ref[...],
                       preferred_element_type=F32)
        cs, accs = list(cs), list(accs)
        for t in range(len(tiles)):
            for g in range(SB_GROUP):
                n = t * SB_GROUP + g
                s = sums[n * CHUNK:(n + 1) * CHUNK]
                after = s[:, :CHUNK] + cs[g]
                w = jnp.where(masks[t], jnp.exp2(log_beta[n] + after), 0.0)
                accs[g] = accs[g] + jnp.dot(w.astype(BF16), vs[t][:, heads[g]],
                                            preferred_element_type=F32)
                cs[g] = cs[g] + s[:, CHUNK:]
        return tuple(cs), tuple(accs)

    def least_decayed(cs):
        return jnp.max(functools.reduce(jnp.maximum, cs))

    def valid_keys(j):
        return j * CHUNK + col >= N_PAD

    def q_block(i, carry):
        start = pl.multiple_of(i * CHUNK, CHUNK)
        q = q_ref[pl.ds(start, CHUNK), :]
        qs = [q[:, heads[g]] for g in range(SB_GROUP)]
        zero = (jnp.zeros((CHUNK, CHUNK), F32),) * SB_GROUP
        first = [(i, jnp.logical_and(col < row, valid_keys(i)))]
        first += [(i - t, valid_keys(i - t)) for t in range(1, SB_FIRST_TILES)]
        cs, accs = step(qs, first, zero, zero)

        def cond(st):
            return jnp.logical_and(st[0] >= 0, st[1] > -SB_UNDERFLOW * LOG2E)

        def body(st):
            j = st[0]
            cs_, accs_ = step(qs, [(j, valid_keys(j))], st[2:2 + SB_GROUP],
                              st[2 + SB_GROUP:])
            return (j - 1, least_decayed(cs_)) + cs_ + accs_

        st = lax.while_loop(cond, body,
                            (i - SB_FIRST_TILES, least_decayed(cs)) + cs + accs)
        accs = st[2 + SB_GROUP:]
        for g in range(SB_GROUP):
            o_ref[pl.ds(start, CHUNK), heads[g]] = _rms(
                accs[g], nw_ref[:, heads[g]]).astype(BF16)
        return carry

    lax.fori_loop(0, nblk, q_block, 0)


def _stick_breaking(rest, nw, b, tp):
    m = rest.shape[0]
    nblk = tp // CHUNK
    width = SB_GROUP * SB_DH
    groups = SB_HEADS // SB_GROUP
    base = 2 * RET_WIDTH // width
    idx = jnp.arange(CHUNK)
    tri = (idx[:, None] > idx[None, :]).astype(BF16)
    tt = jnp.concatenate([tri, jnp.ones((CHUNK, CHUNK), BF16)], axis=1)
    tt = jnp.concatenate([tt, tt], axis=0)
    return pl.pallas_call(
        functools.partial(_sb_kernel, nblk=nblk),
        grid=(b, groups),
        in_specs=[
            pl.BlockSpec((tp, width), lambda i, h: (i, base + h),
                         pipeline_mode=pl.Buffered(1)),
            pl.BlockSpec((tp, width), lambda i, h: (i, base + groups + h),
                         pipeline_mode=pl.Buffered(1)),
            pl.BlockSpec((tp, width), lambda i, h: (i, base + 2 * groups + h),
                         pipeline_mode=pl.Buffered(1)),
            pl.BlockSpec((2 * CHUNK, 2 * CHUNK), lambda i, h: (0, 0)),
            pl.BlockSpec((1, width), lambda i, h: (0, h)),
        ],
        out_specs=pl.BlockSpec((tp, width), lambda i, h: (i, h)),
        out_shape=jax.ShapeDtypeStruct((m, SB_WIDTH), BF16),
        compiler_params=_params("parallel", "parallel"),
        name="stick_breaking",
    )(rest, rest, rest, tt, nw)


def _out_proj_kernel(r_ref, s_ref, w1_ref, w2_ref, o_ref):
    o_ref[...] = (jnp.dot(r_ref[...], w1_ref[...], preferred_element_type=F32)
                  + jnp.dot(s_ref[...], w2_ref[...], preferred_element_type=F32)
                  ).astype(o_ref.dtype)


def _out_proj(ret_out, sb_out, w_out, tp):
    m = ret_out.shape[0]
    d = w_out.shape[1]
    bm = _largest_divisor(tp, 1056, BF16_SUBLANES)
    bn = 1024
    return pl.pallas_call(
        _out_proj_kernel,
        grid=(m // bm, d // bn),
        in_specs=[
            pl.BlockSpec((bm, RET_WIDTH), lambda i, j: (i, 0)),
            pl.BlockSpec((bm, SB_WIDTH), lambda i, j: (i, 0)),
            pl.BlockSpec((RET_WIDTH, bn), lambda i, j: (0, j)),
            pl.BlockSpec((SB_WIDTH, bn), lambda i, j: (1, j)),
        ],
        out_specs=pl.BlockSpec((bm, bn), lambda i, j: (i, j)),
        out_shape=jax.ShapeDtypeStruct((m, d), BF16),
        compiler_params=_params("parallel", "arbitrary"),
        name="out_proj",
    )(ret_out, sb_out, w_out, w_out)


def _post_attn_kernel(a_ref, x_ref, meta_ref, w1_ref, w2_ref, tok_ref, tail_ref):
    r = pl.program_id(1)
    h = _padded_rows(x_ref, meta_ref, r)
    h1 = h + _rms(a_ref[...].astype(F32), w1_ref[...])
    hn = _rms(h1, w2_ref[...]).astype(BF16)

    @pl.when(r == 0)
    def _():
        tail_ref[...] = hn[:, CHUNK - HALO:, :]

    @pl.when(r > 0)
    def _():
        tok_ref[...] = hn


def _post_attn(a, x, meta, w_post, w_pre):
    b, tp, d = a.shape
    vec = pl.BlockSpec((1, d), lambda i, r: (0, 0))
    return pl.pallas_call(
        _post_attn_kernel,
        grid=(b // ROW_BATCH, tp // CHUNK),
        in_specs=[_row_spec(d), _row_spec(d, -1),
                  pl.BlockSpec((N_META, d), lambda i, r: (0, 0)), vec, vec],
        out_specs=[_row_spec(d, -1),
                   pl.BlockSpec((ROW_BATCH, HALO, d), lambda i, r: (i, 0, 0))],
        out_shape=[jax.ShapeDtypeStruct((b, tp - CHUNK, d), BF16),
                   jax.ShapeDtypeStruct((b, HALO, d), BF16)],
        compiler_params=_params("parallel", "arbitrary"),
        name="post_attn",
    )(a, x, meta, w_post, w_pre)


FFN_SUB_ROWS = 352


def _ffn_up_kernel(x_ref, xh_ref, wg_ref, wu_ref, cw_ref, cb_ref, o_ref, g_ref):
    bm = x_ref.shape[0]
    wg = wg_ref[...]
    wu = wu_ref[...]
    g_ref[:HALO, :] = jnp.dot(xh_ref[...], wg, preferred_element_type=F32)
    for start in range(0, bm, FFN_SUB_ROWS):
        sub = min(FFN_SUB_ROWS, bm - start)
        x = x_ref[start:start + sub, :]
        lo = HALO + start
        g_ref[lo:lo + sub, :] = jnp.dot(x, wg, preferred_element_type=F32)
        up = jnp.dot(x, wu, preferred_element_type=F32)
        ext = g_ref[lo - 8:lo + sub, :]
        prev1 = pltpu.roll(ext, 1, axis=0)
        prev2 = pltpu.roll(prev1, 1, axis=0)
        conv = cb_ref[...] + prev2[8:] * cw_ref[0:1, :]
        conv = conv + prev1[8:] * cw_ref[1:2, :]
        conv = conv + ext[8:] * cw_ref[2:3, :]
        o_ref[start:start + sub, :] = (conv * jax.nn.sigmoid(conv) * up).astype(BF16)


def _ffn_up(hn, hn_tail, w_up, conv_w, conv_b):
    m, d = hn.shape
    bm = m // (hn_tail.shape[0] // HALO)
    dff = w_up.shape[1] // 2
    bn = 256
    up_base = dff // bn
    return pl.pallas_call(
        _ffn_up_kernel,
        grid=(m // bm, dff // bn),
        in_specs=[
            pl.BlockSpec((bm, d), lambda i, j: (i, 0), pipeline_mode=pl.Buffered(1)),
            pl.BlockSpec((HALO, d), lambda i, j: (i, 0)),
            pl.BlockSpec((d, bn), lambda i, j: (0, j)),
            pl.BlockSpec((d, bn), lambda i, j: (0, j + up_base)),
            pl.BlockSpec((CONV_W, bn), lambda i, j: (0, j)),
            pl.BlockSpec((1, bn), lambda i, j: (0, j)),
        ],
        out_specs=pl.BlockSpec((bm, bn), lambda i, j: (i, j)),
        out_shape=jax.ShapeDtypeStruct((m, dff), BF16),
        scratch_shapes=[pltpu.VMEM((bm + HALO, bn), F32)],
        compiler_params=_params("parallel", "arbitrary"),
        name="ffn_up",
    )(hn, hn_tail, w_up, w_up, conv_w, conv_b)


def _ffn_down(act, w_down):
    m, dff = act.shape
    d = w_down.shape[1]
    bm = _largest_divisor(m, 528, BF16_SUBLANES)
    bn = 512
    return pl.pallas_call(
        _proj_plain_kernel,
        grid=(m // bm, d // bn),
        in_specs=[
            pl.BlockSpec((bm, dff), lambda i, j: (i, 0)),
            pl.BlockSpec((dff, bn), lambda i, j: (0, j)),
        ],
        out_specs=pl.BlockSpec((bm, bn), lambda i, j: (i, j)),
        out_shape=jax.ShapeDtypeStruct((m, d), BF16),
        compiler_params=_params("parallel", "arbitrary"),
        name="ffn_down",
    )(act, w_down)


def _final_kernel(f_ref, a_ref, x_ref, w1_ref, w2_ref, o_ref):
    h1 = x_ref[...] + _rms(a_ref[...].astype(F32), w1_ref[...])
    o_ref[...] = h1 + _rms(f_ref[...].astype(F32), w2_ref[...])


def _final(f, a, x, w_post, w_final):
    b, seq, d = x.shape
    vec = pl.BlockSpec((1, d), lambda i, r: (0, 0))
    return pl.pallas_call(
        _final_kernel,
        grid=(b // ROW_BATCH, seq // CHUNK),
        in_specs=[_row_spec(d), _row_spec(d, 1), _row_spec(d), vec, vec],
        out_specs=_row_spec(d),
        out_shape=jax.ShapeDtypeStruct((b, seq, d), F32),
        compiler_params=_params("parallel", "parallel"),
        name="final",
    )(f, a, x, w_post, w_final)


def _rotary_tables(tp):
    half = RET_DK // 2
    pos = (jnp.arange(tp) - N_PAD).astype(F32)
    inv = ROPE_BASE ** (-jnp.arange(half, dtype=F32) / half)
    ang = pos[:, None] * inv[None, :]
    return jnp.cos(ang), jnp.sin(ang)


def kernel(x, meta_tokens, attn_pre_norm_w, w_in, ret_gn_w, sb_norm_w, w_out,
           attn_post_norm_w, ffn_pre_norm_w, w_up, conv_w, conv_b, w_down, ffn_post_norm_w):
    b, seq, d = x.shape
    assert seq % CHUNK == 0 and w_in.shape[0] == 1
    tp = seq + CHUNK
    m = b * tp

    hn = _prep(x, meta_tokens, attn_pre_norm_w).reshape(m, d)

    cos, sin = _rotary_tables(tp)
    qk, rest, (w_down_bf,), (w_up_bf,) = _in_proj(
        hn, w_in[0].astype(BF16), cos, sin, tp, [w_down], [w_up])
    ret_out, (w_out_bf,) = _retention(qk, rest, ret_gn_w, b, tp, [w_out])
    sb_out = _stick_breaking(rest, sb_norm_w, b, tp)
    a = _out_proj(ret_out, sb_out, w_out_bf, tp)
    a = a.reshape(b, tp, d)
    hn2, hn2_tail = _post_attn(a, x, meta_tokens, attn_post_norm_w, ffn_pre_norm_w)

    act = _ffn_up(hn2.reshape(b * seq, d), hn2_tail.reshape(b * HALO, d), w_up_bf,
                  conv_w[0], conv_b)
    f = _ffn_down(act, w_down_bf)
    return _final(f.reshape(b, seq, d), a, x, attn_post_norm_w, ffn_post_norm_w)
```

```python
import functools
import math

import numpy as np
import jax
import jax.numpy as jnp
from jax import lax
from jax.experimental import pallas as pl
from jax.experimental.pallas import tpu as pltpu

N_META = 16
CHUNK = 128
N_PAD = CHUNK - N_META
RET_HEADS = 8
RET_DK = 256
RET_DV = 256
SB_HEADS = 16
SB_DH = 128
RET_WIDTH = RET_HEADS * RET_DV
RET_QK_WIDTH = RET_HEADS * RET_DK
SB_WIDTH = SB_HEADS * SB_DH
CONV_W = 3
EPS = 1e-6
ROPE_BASE = 10000.0

BF16_SUBLANES = 16
HALO = BF16_SUBLANES
assert CONV_W - 1 <= HALO <= N_META
VMEM_LIMIT_BYTES = 58 * 1024 * 1024

F32 = jnp.float32
BF16 = jnp.bfloat16

_NT = (((1,), (1,)), ((), ()))
_TN = (((0,), (0,)), ((), ()))


def _params(*semantics):
    return pltpu.CompilerParams(dimension_semantics=semantics,
                                vmem_limit_bytes=VMEM_LIMIT_BYTES)


def _largest_divisor(n, target, multiple):
    best = None
    for d in range(multiple, min(n, target) + 1, multiple):
        if n % d == 0:
            best = d
    assert best is not None, (n, target, multiple)
    return best


def _rms(v, w):
    ms = jnp.mean(v * v, axis=-1, keepdims=True)
    return v * lax.rsqrt(ms + EPS) * w


ROW_BATCH = 2


def _padded_rows(x_ref, meta_ref, r):
    d = x_ref.shape[-1]
    first = jnp.concatenate([jnp.zeros((N_PAD, d), F32), meta_ref[...]], axis=0)
    return jnp.where(r == 0, first[None], x_ref[...])


def _row_spec(d, shift=0):
    return pl.BlockSpec((ROW_BATCH, CHUNK, d),
                        lambda i, r: (i, jnp.maximum(r + shift, 0), 0))


def _prep_kernel(x_ref, meta_ref, w_ref, hn_ref):
    h = _padded_rows(x_ref, meta_ref, pl.program_id(1))
    hn_ref[...] = _rms(h, w_ref[...]).astype(BF16)


def _prep(x, meta, w):
    b, seq, d = x.shape
    nblk = seq // CHUNK + 1
    tp = nblk * CHUNK
    return pl.pallas_call(
        _prep_kernel,
        grid=(b // ROW_BATCH, nblk),
        in_specs=[
            _row_spec(d, -1),
            pl.BlockSpec((N_META, d), lambda i, r: (0, 0)),
            pl.BlockSpec((1, d), lambda i, r: (0, 0)),
        ],
        out_specs=_row_spec(d),
        out_shape=jax.ShapeDtypeStruct((b, tp, d), BF16),
        compiler_params=_params("parallel", "arbitrary"),
        name="prep",
    )(x, meta, w)


CAST_BLOCK_BYTES = 3 * 1024 * 1024


def _cast_plan(rows, cols, steps):
    best = None
    for br in range(BF16_SUBLANES, rows + 1, BF16_SUBLANES):
        if rows % br:
            continue
        for bc in range(128, cols + 1, 128):
            if cols % bc or br * bc * 4 > CAST_BLOCK_BYTES:
                continue
            if (rows // br) * (cols // bc) <= steps and (best is None or br * bc < best[0] * best[1]):
                best = (br, bc)
    return best


def _with_casts(kernel_fn, n_in, n_out, weights, steps, nj):
    plans, in_specs, out_specs, out_shapes, operands = [], [], [], [], []
    start = 0
    for w in weights:
        _, rows, cols = w.shape
        plan = _cast_plan(rows, cols, steps - start)
        plans.append(plan)
        if plan is None:
            continue
        br, bc = plan
        ncb = cols // bc
        nb = (rows // br) * ncb

        def block(i, j, start=start, nb=nb, ncb=ncb):
            t = jnp.clip(i * nj + j - start, 0, nb - 1)
            return t // ncb, t % ncb

        in_specs.append(pl.BlockSpec((None, br, bc), lambda i, j, block=block: (0,) + block(i, j)))
        out_specs.append(pl.BlockSpec((br, bc), block))
        out_shapes.append(jax.ShapeDtypeStruct((rows, cols), BF16))
        operands.append(w)
        start += nb
    n_cast = len(operands)

    def wrapped(*refs):
        ins, rest = refs[:n_in], refs[n_in:]
        cast_in, rest = rest[:n_cast], rest[n_cast:]
        outs, rest = rest[:n_out], rest[n_out:]
        cast_out, scratch = rest[:n_cast], rest[n_cast:]
        kernel_fn(*ins, *outs, *scratch)
        for ci, co in zip(cast_in, cast_out):
            co[...] = ci[...].astype(BF16)

    def finish(cast_results):
        it = iter(cast_results)
        return [w[0].astype(BF16) if p is None else next(it)
                for w, p in zip(weights, plans)]

    return wrapped, in_specs, out_specs, out_shapes, operands, finish


def _proj_rot_kernel(x_ref, w_ref, cos_ref, sin_ref, o_ref, *, n_q_blocks):
    j = pl.program_id(1)
    acc = jnp.dot(x_ref[...], w_ref[...], preferred_element_type=F32)
    scale = jnp.where(j < n_q_blocks, RET_DK ** -0.5, 1.0).astype(F32)
    cos = cos_ref[...]
    sin = sin_ref[...]
    half = RET_DK // 2
    for h in range(o_ref.shape[1] // RET_DK):
        lo = h * RET_DK
        x1 = acc[:, lo:lo + half]
        x2 = acc[:, lo + half:lo + RET_DK]
        o_ref[:, lo:lo + half] = ((x1 * cos - x2 * sin) * scale).astype(BF16)
        o_ref[:, lo + half:lo + RET_DK] = ((x1 * sin + x2 * cos) * scale).astype(BF16)


def _proj_plain_kernel(x_ref, w_ref, o_ref):
    o_ref[...] = jnp.dot(x_ref[...], w_ref[...],
                         preferred_element_type=F32).astype(o_ref.dtype)


def _in_proj(hn, w_in, cos, sin, tp, later_weights_rot, later_weights_rest):
    m, d = hn.shape
    bm = _largest_divisor(tp, 1056, BF16_SUBLANES)
    bn = 1024
    pos_blocks = tp // bm
    n_rot = 2 * RET_QK_WIDTH
    n_rest = w_in.shape[1] - n_rot

    grid = (m // bm, n_rot // bn)
    body, c_in, c_out, c_shape, c_ops, finish_rot = _with_casts(
        functools.partial(_proj_rot_kernel, n_q_blocks=RET_QK_WIDTH // bn),
        4, 1, later_weights_rot, grid[0] * grid[1], grid[1])
    qk, *cast_rot = pl.pallas_call(
        body,
        grid=grid,
        in_specs=[
            pl.BlockSpec((bm, d), lambda i, j: (i, 0)),
            pl.BlockSpec((d, bn), lambda i, j: (0, j)),
            pl.BlockSpec((bm, RET_DK // 2), lambda i, j: (i % pos_blocks, 0)),
            pl.BlockSpec((bm, RET_DK // 2), lambda i, j: (i % pos_blocks, 0)),
        ] + c_in,
        out_specs=[pl.BlockSpec((bm, bn), lambda i, j: (i, j))] + c_out,
        out_shape=[jax.ShapeDtypeStruct((m, n_rot), BF16)] + c_shape,
        compiler_params=_params("arbitrary", "arbitrary"),
        name="in_proj_rot",
    )(hn, w_in, cos, sin, *c_ops)

    rot_blocks = n_rot // bn
    grid = (m // bm, n_rest // bn)
    body, c_in, c_out, c_shape, c_ops, finish_rest = _with_casts(
        _proj_plain_kernel, 2, 1, later_weights_rest, grid[0] * grid[1], grid[1])
    rest, *cast_rest = pl.pallas_call(
        body,
        grid=grid,
        in_specs=[
            pl.BlockSpec((bm, d), lambda i, j: (i, 0)),
            pl.BlockSpec((d, bn), lambda i, j: (0, j + rot_blocks)),
        ] + c_in,
        out_specs=[pl.BlockSpec((bm, bn), lambda i, j: (i, j))] + c_out,
        out_shape=[jax.ShapeDtypeStruct((m, n_rest), BF16)] + c_shape,
        compiler_params=_params("arbitrary", "arbitrary"),
        name="in_proj_rest",
    )(hn, w_in, *c_ops)
    return qk, rest, finish_rot(cast_rot), finish_rest(cast_rest)


def _ret_log_gamma():
    return np.log(1.0 - 2.0 ** (-5.0 - np.arange(RET_HEADS, dtype=np.float64)))


def _ret_kernel(q_ref, k_ref, v_ref, g_ref, intra_ref, qd_ref, kd_ref, gn_ref,
                o_ref, state_ref):
    c = pl.program_id(1)

    @pl.when(c == 0)
    def _():
        state_ref[...] = jnp.zeros(state_ref.shape, F32)

    chunk_decay = np.exp(CHUNK * _ret_log_gamma()).astype(np.float32)
    for n in range(q_ref.shape[0]):
        for h in range(RET_HEADS):
            sl = slice(h * RET_DK, (h + 1) * RET_DK)
            q = q_ref[n, :, sl]
            k = k_ref[n, :, sl]
            v = v_ref[n, :, sl]
            s = lax.dot_general(q, k, _NT, preferred_element_type=F32) * intra_ref[h]
            intra = jnp.dot(s.astype(BF16), v, preferred_element_type=F32)
            st = state_ref[n, h]
            qd = qd_ref[h]
            kd = kd_ref[h]
            cross = jnp.dot(q, st.astype(BF16), preferred_element_type=F32)
            cross = cross * jnp.concatenate([qd, qd], axis=1)
            kdk = (k.astype(F32) * jnp.concatenate([kd, kd], axis=1)).astype(BF16)
            upd = lax.dot_general(kdk, v, _TN, preferred_element_type=F32)
            state_ref[n, h] = st * float(chunk_decay[h]) + upd
            y = intra + cross
            yn = _rms(y, gn_ref[:, sl])
            g = g_ref[n, :, sl].astype(F32)
            o_ref[n, :, sl] = (g * jax.nn.sigmoid(g) * yn).astype(BF16)


def _retention(qk, rest, gn_w, b, tp, later_weights):
    m = qk.shape[0]
    nblk = tp // CHUNK
    lg = jnp.log(1.0 - 2.0 ** (-5.0 - jnp.arange(RET_HEADS, dtype=F32)))
    idx = jnp.arange(CHUNK, dtype=F32)
    diff = idx[:, None] - idx[None, :]
    intra = jnp.where(diff[None] >= 0,
                      jnp.exp(jnp.maximum(diff, 0.0)[None] * lg[:, None, None]), 0.0)
    qd = jnp.exp((idx[None, :] + 1.0) * lg[:, None])
    kd = jnp.exp((CHUNK - 1.0 - idx[None, :]) * lg[:, None])
    qd = jnp.broadcast_to(qd[:, :, None], (RET_HEADS, CHUNK, CHUNK))
    kd = jnp.broadcast_to(kd[:, :, None], (RET_HEADS, CHUNK, CHUNK))
    table = pl.BlockSpec((RET_HEADS, CHUNK, CHUNK), lambda i, c: (0, 0, 0))
    grid = (b // ROW_BATCH, nblk)
    body, c_in, c_out, c_shape, c_ops, finish = _with_casts(
        _ret_kernel, 8, 1, later_weights, grid[0] * grid[1], nblk)

    def cols(block, width):
        return pl.BlockSpec((ROW_BATCH, CHUNK, width), lambda i, c: (i, c, block))

    qk = qk.reshape(b, tp, -1)
    rest = rest.reshape(b, tp, -1)
    out, *cast = pl.pallas_call(
        body,
        grid=grid,
        in_specs=[cols(0, RET_QK_WIDTH), cols(1, RET_QK_WIDTH),
                  cols(0, RET_WIDTH), cols(1, RET_WIDTH),
                  table, table, table,
                  pl.BlockSpec((1, RET_WIDTH), lambda i, c: (0, 0))] + c_in,
        out_specs=[cols(0, RET_WIDTH)] + c_out,
        out_shape=[jax.ShapeDtypeStruct((b, tp, RET_WIDTH), BF16)] + c_shape,
        scratch_shapes=[pltpu.VMEM((ROW_BATCH, RET_HEADS, RET_DK, RET_DV), F32)],
        compiler_params=_params("arbitrary", "arbitrary"),
        name="retention",
    )(qk, qk, rest, rest, intra, qd, kd, gn_w, *c_ops)
    return out.reshape(m, RET_WIDTH), finish(cast)


SB_GROUP = 8
SB_UNDERFLOW = 110.0
SB_FIRST_TILES = 3
LOG2E = 1.4426950408889634


def _sb_kernel(q_ref, k_ref, v_ref, tt_ref, nw_ref, o_ref, *, nblk):
    scale = LOG2E / math.sqrt(SB_DH)
    row = lax.broadcasted_iota(jnp.int32, (CHUNK, CHUNK), 0)
    col = lax.broadcasted_iota(jnp.int32, (CHUNK, CHUNK), 1)
    heads = [slice(g * SB_DH, (g + 1) * SB_DH) for g in range(SB_GROUP)]

    def step(qs, tiles, cs, accs):
        vs, masks, log_beta, split = [], [], [], []
        for j, mask in tiles:
            start = pl.multiple_of(jnp.maximum(j, 0) * CHUNK, CHUNK)
            k = k_ref[pl.ds(start, CHUNK), :]
            vs.append(v_ref[pl.ds(start, CHUNK), :])
            masks.append(mask)
            for g in range(SB_GROUP):
                zs = lax.dot_general(qs[g], k[:, heads[g]], _NT,
                                     preferred_element_type=F32) * scale
                sp = jnp.maximum(zs, 0.0) + jnp.log2(1.0 + jnp.exp2(-jnp.abs(zs)))
                log_beta.append(zs - sp)
                log_keep = jnp.where(mask, -sp, 0.0)
                hi = log_keep.astype(BF16)
                lo = (log_keep - hi.astype(F32)).astype(BF16)
                split.append(jnp.concatenate([hi, lo], axis=1))
        sums = jnp.dot(jnp.concatenate(split, axis=0), tt_ref[...],
                       preferred_element_type=F32)
        cs, accs = list(cs), list(accs)
        for t in range(len(tiles)):
            for g in range(SB_GROUP):
                n = t * SB_GROUP + g
                s = sums[n * CHUNK:(n + 1) * CHUNK]
                after = s[:, :CHUNK] + cs[g]
                w = jnp.where(masks[t], jnp.exp2(log_beta[n] + after), 0.0)
                accs[g] = accs[g] + jnp.dot(w.astype(BF16), vs[t][:, heads[g]],
                                            preferred_element_type=F32)
                cs[g] = cs[g] + s[:, CHUNK:]
        return tuple(cs), tuple(accs)

    def least_decayed(cs):
        return jnp.max(functools.reduce(jnp.maximum, cs))

    def valid_keys(j):
        return j * CHUNK + col >= N_PAD

    def q_block(i, carry):
        start = pl.multiple_of(i * CHUNK, CHUNK)
        q = q_ref[pl.ds(start, CHUNK), :]
        qs = [q[:, heads[g]] for g in range(SB_GROUP)]
        zero = (jnp.zeros((CHUNK, CHUNK), F32),) * SB_GROUP
        first = [(i, jnp.logical_and(col < row, valid_keys(i)))]
        first += [(i - t, valid_keys(i - t)) for t in range(1, SB_FIRST_TILES)]
        cs, accs = step(qs, first, zero, zero)

        def cond(st):
            return jnp.logical_and(st[0] >= 0, st[1] > -SB_UNDERFLOW * LOG2E)

        def body(st):
            j = st[0]
            cs_, accs_ = step(qs, [(j, valid_keys(j))], st[2:2 + SB_GROUP],
                              st[2 + SB_GROUP:])
            return (j - 1, least_decayed(cs_)) + cs_ + accs_

        st = lax.while_loop(cond, body,
                            (i - SB_FIRST_TILES, least_decayed(cs)) + cs + accs)
        accs = st[2 + SB_GROUP:]
        for g in range(SB_GROUP):
            o_ref[pl.ds(start, CHUNK), heads[g]] = _rms(
                accs[g], nw_ref[:, heads[g]]).astype(BF16)
        return carry

    lax.fori_loop(0, nblk, q_block, 0)


def _stick_breaking(rest, nw, b, tp):
    m = rest.shape[0]
    nblk = tp // CHUNK
    width = SB_GROUP * SB_DH
    groups = SB_HEADS // SB_GROUP
    base = 2 * RET_WIDTH // width
    idx = jnp.arange(CHUNK)
    tri = (idx[:, None] > idx[None, :]).astype(BF16)
    tt = jnp.concatenate([tri, jnp.ones((CHUNK, CHUNK), BF16)], axis=1)
    tt = jnp.concatenate([tt, tt], axis=0)
    return pl.pallas_call(
        functools.partial(_sb_kernel, nblk=nblk),
        grid=(b, groups),
        in_specs=[
            pl.BlockSpec((tp, width), lambda i, h: (i, base + h),
                         pipeline_mode=pl.Buffered(1)),
            pl.BlockSpec((tp, width), lambda i, h: (i, base + groups + h),
                         pipeline_mode=pl.Buffered(1)),
            pl.BlockSpec((tp, width), lambda i, h: (i, base + 2 * groups + h),
                         pipeline_mode=pl.Buffered(1)),
            pl.BlockSpec((2 * CHUNK, 2 * CHUNK), lambda i, h: (0, 0)),
            pl.BlockSpec((1, width), lambda i, h: (0, h)),
        ],
        out_specs=pl.BlockSpec((tp, width), lambda i, h: (i, h)),
        out_shape=jax.ShapeDtypeStruct((m, SB_WIDTH), BF16),
        compiler_params=_params("parallel", "parallel"),
        name="stick_breaking",
    )(rest, rest, rest, tt, nw)


def _out_proj_kernel(r_ref, s_ref, w1_ref, w2_ref, o_ref):
    o_ref[...] = (jnp.dot(r_ref[...], w1_ref[...], preferred_element_type=F32)
                  + jnp.dot(s_ref[...], w2_ref[...], preferred_element_type=F32)
                  ).astype(o_ref.dtype)


def _out_proj(ret_out, sb_out, w_out, tp):
    m = ret_out.shape[0]
    d = w_out.shape[1]
    bm = _largest_divisor(tp, 1056, BF16_SUBLANES)
    bn = 1024
    return pl.pallas_call(
        _out_proj_kernel,
        grid=(m // bm, d // bn),
        in_specs=[
            pl.BlockSpec((bm, RET_WIDTH), lambda i, j: (i, 0)),
            pl.BlockSpec((bm, SB_WIDTH), lambda i, j: (i, 0)),
            pl.BlockSpec((RET_WIDTH, bn), lambda i, j: (0, j)),
            pl.BlockSpec((SB_WIDTH, bn), lambda i, j: (1, j)),
        ],
        out_specs=pl.BlockSpec((bm, bn), lambda i, j: (i, j)),
        out_shape=jax.ShapeDtypeStruct((m, d), BF16),
        compiler_params=_params("parallel", "arbitrary"),
        name="out_proj",
    )(ret_out, sb_out, w_out, w_out)


def _post_attn_kernel(a_ref, x_ref, meta_ref, w1_ref, w2_ref, hn_ref):
    h = _padded_rows(x_ref, meta_ref, pl.program_id(1))
    h1 = h + _rms(a_ref[...].astype(F32), w1_ref[...])
    hn_ref[...] = _rms(h1, w2_ref[...]).astype(BF16)


def _post_attn(a, x, meta, w_post, w_pre):
    b, tp, d = a.shape
    vec = pl.BlockSpec((1, d), lambda i, r: (0, 0))
    return pl.pallas_call(
        _post_attn_kernel,
        grid=(b // ROW_BATCH, tp // CHUNK),
        in_specs=[_row_spec(d), _row_spec(d, -1),
                  pl.BlockSpec((N_META, d), lambda i, r: (0, 0)), vec, vec],
        out_specs=_row_spec(d),
        out_shape=jax.ShapeDtypeStruct((b, tp, d), BF16),
        compiler_params=_params("parallel", "arbitrary"),
        name="post_attn",
    )(a, x, meta, w_post, w_pre)


FFN_SUB_ROWS = 352


def _ffn_up_kernel(x_ref, wg_ref, wu_ref, cw_ref, cb_ref, o_ref, g_ref):
    seq = o_ref.shape[0]
    first = x_ref.shape[0] - seq
    wg = wg_ref[...]
    wu = wu_ref[...]
    g_ref[:HALO, :] = jnp.dot(x_ref[first - HALO:first, :], wg, preferred_element_type=F32)
    for start in range(0, seq, FFN_SUB_ROWS):
        sub = min(FFN_SUB_ROWS, seq - start)
        x = x_ref[first + start:first + start + sub, :]
        lo = HALO + start
        g_ref[lo:lo + sub, :] = jnp.dot(x, wg, preferred_element_type=F32)
        up = jnp.dot(x, wu, preferred_element_type=F32)
        ext = g_ref[lo - 8:lo + sub, :]
        prev1 = pltpu.roll(ext, 1, axis=0)
        prev2 = pltpu.roll(prev1, 1, axis=0)
        conv = cb_ref[...] + prev2[8:] * cw_ref[0:1, :]
        conv = conv + prev1[8:] * cw_ref[1:2, :]
        conv = conv + ext[8:] * cw_ref[2:3, :]
        o_ref[start:start + sub, :] = (conv * jax.nn.sigmoid(conv) * up).astype(BF16)


def _ffn_up(hn, w_up, conv_w, conv_b, tp):
    m, d = hn.shape
    seq = tp - CHUNK
    dff = w_up.shape[1] // 2
    bn = 256
    up_base = dff // bn
    return pl.pallas_call(
        _ffn_up_kernel,
        grid=(m // tp, dff // bn),
        in_specs=[
            pl.BlockSpec((tp, d), lambda i, j: (i, 0), pipeline_mode=pl.Buffered(1)),
            pl.BlockSpec((d, bn), lambda i, j: (0, j)),
            pl.BlockSpec((d, bn), lambda i, j: (0, j + up_base)),
            pl.BlockSpec((CONV_W, bn), lambda i, j: (0, j)),
            pl.BlockSpec((1, bn), lambda i, j: (0, j)),
        ],
        out_specs=pl.BlockSpec((seq, bn), lambda i, j: (i, j)),
        out_shape=jax.ShapeDtypeStruct((m // tp * seq, dff), BF16),
        scratch_shapes=[pltpu.VMEM((seq + HALO, bn), F32)],
        compiler_params=_params("parallel", "arbitrary"),
        name="ffn_up",
    )(hn, w_up, w_up, conv_w, conv_b)


def _ffn_down(act, w_down):
    m, dff = act.shape
    d = w_down.shape[1]
    bm = _largest_divisor(m, 528, BF16_SUBLANES)
    bn = 512
    return pl.pallas_call(
        _proj_plain_kernel,
        grid=(m // bm, d // bn),
        in_specs=[
            pl.BlockSpec((bm, dff), lambda i, j: (i, 0)),
            pl.BlockSpec((dff, bn), lambda i, j: (0, j)),
        ],
        out_specs=pl.BlockSpec((bm, bn), lambda i, j: (i, j)),
        out_shape=jax.ShapeDtypeStruct((m, d), BF16),
        compiler_params=_params("parallel", "arbitrary"),
        name="ffn_down",
    )(act, w_down)


def _final_kernel(f_ref, a_ref, x_ref, w1_ref, w2_ref, o_ref):
    h1 = x_ref[...] + _rms(a_ref[...].astype(F32), w1_ref[...])
    o_ref[...] = h1 + _rms(f_ref[...].astype(F32), w2_ref[...])


def _final(f, a, x, w_post, w_final):
    b, seq, d = x.shape
    vec = pl.BlockSpec((1, d), lambda i, r: (0, 0))
    return pl.pallas_call(
        _final_kernel,
        grid=(b // ROW_BATCH, seq // CHUNK),
        in_specs=[_row_spec(d), _row_spec(d, 1), _row_spec(d), vec, vec],
        out_specs=_row_spec(d),
        out_shape=jax.ShapeDtypeStruct((b, seq, d), F32),
        compiler_params=_params("parallel", "parallel"),
        name="final",
    )(f, a, x, w_post, w_final)


def _rotary_tables(tp):
    half = RET_DK // 2
    pos = (jnp.arange(tp) - N_PAD).astype(F32)
    inv = ROPE_BASE ** (-jnp.arange(half, dtype=F32) / half)
    ang = pos[:, None] * inv[None, :]
    return jnp.cos(ang), jnp.sin(ang)


def kernel(x, meta_tokens, attn_pre_norm_w, w_in, ret_gn_w, sb_norm_w, w_out,
           attn_post_norm_w, ffn_pre_norm_w, w_up, conv_w, conv_b, w_down, ffn_post_norm_w):
    b, seq, d = x.shape
    assert seq % CHUNK == 0 and w_in.shape[0] == 1
    tp = seq + CHUNK
    m = b * tp

    hn = _prep(x, meta_tokens, attn_pre_norm_w).reshape(m, d)

    cos, sin = _rotary_tables(tp)
    qk, rest, (w_down_bf,), (w_up_bf,) = _in_proj(
        hn, w_in[0].astype(BF16), cos, sin, tp, [w_down], [w_up])
    ret_out, (w_out_bf,) = _retention(qk, rest, ret_gn_w, b, tp, [w_out])
    sb_out = _stick_breaking(rest, sb_norm_w, b, tp)
    a = _out_proj(ret_out, sb_out, w_out_bf, tp)
    a = a.reshape(b, tp, d)
    hn2 = _post_attn(a, x, meta_tokens, attn_post_norm_w, ffn_pre_norm_w)

    act = _ffn_up(hn2.reshape(m, d), w_up_bf, conv_w[0], conv_b, tp)
    f = _ffn_down(act, w_down_bf)
    return _final(f.reshape(b, seq, d), a, x, attn_post_norm_w, ffn_post_norm_w)
```

```python
import functools
import math

import numpy as np
import jax
import jax.numpy as jnp
from jax import lax
from jax.experimental import pallas as pl
from jax.experimental.pallas import tpu as pltpu

N_META = 16
CHUNK = 128
N_PAD = CHUNK - N_META
RET_HEADS = 8
RET_DK = 256
RET_DV = 256
SB_HEADS = 16
SB_DH = 128
RET_WIDTH = RET_HEADS * RET_DV
RET_QK_WIDTH = RET_HEADS * RET_DK
SB_WIDTH = SB_HEADS * SB_DH
CONV_W = 3
EPS = 1e-6
ROPE_BASE = 10000.0

BF16_SUBLANES = 16
HALO = BF16_SUBLANES
assert CONV_W - 1 <= HALO <= N_META
VMEM_LIMIT_BYTES = 58 * 1024 * 1024

F32 = jnp.float32
BF16 = jnp.bfloat16

_NT = (((1,), (1,)), ((), ()))
_TN = (((0,), (0,)), ((), ()))


def _params(*semantics):
    return pltpu.CompilerParams(dimension_semantics=semantics,
                                vmem_limit_bytes=VMEM_LIMIT_BYTES)


def _largest_divisor(n, target, multiple):
    best = None
    for d in range(multiple, min(n, target) + 1, multiple):
        if n % d == 0:
            best = d
    assert best is not None, (n, target, multiple)
    return best


def _rms(v, w):
    ms = jnp.mean(v * v, axis=-1, keepdims=True)
    return v * lax.rsqrt(ms + EPS) * w


ROW_BATCH = 2


def _padded_rows(x_ref, meta_ref, r):
    d = x_ref.shape[-1]
    first = jnp.concatenate([jnp.zeros((N_PAD, d), F32), meta_ref[...]], axis=0)
    return jnp.where(r == 0, first[None], x_ref[...])


def _row_spec(d, shift=0):
    return pl.BlockSpec((ROW_BATCH, CHUNK, d),
                        lambda i, r: (i, jnp.maximum(r + shift, 0), 0))


def _prep_kernel(x_ref, meta_ref, w_ref, hn_ref):
    h = _padded_rows(x_ref, meta_ref, pl.program_id(1))
    hn_ref[...] = _rms(h, w_ref[...]).astype(BF16)


def _prep(x, meta, w):
    b, seq, d = x.shape
    nblk = seq // CHUNK + 1
    tp = nblk * CHUNK
    return pl.pallas_call(
        _prep_kernel,
        grid=(b // ROW_BATCH, nblk),
        in_specs=[
            _row_spec(d, -1),
            pl.BlockSpec((N_META, d), lambda i, r: (0, 0)),
            pl.BlockSpec((1, d), lambda i, r: (0, 0)),
        ],
        out_specs=_row_spec(d),
        out_shape=jax.ShapeDtypeStruct((b, tp, d), BF16),
        compiler_params=_params("parallel", "arbitrary"),
        name="prep",
    )(x, meta, w)


CAST_BLOCK_BYTES = 3 * 1024 * 1024


def _cast_plan(rows, cols, steps):
    best = None
    for br in range(BF16_SUBLANES, rows + 1, BF16_SUBLANES):
        if rows % br:
            continue
        for bc in range(128, cols + 1, 128):
            if cols % bc or br * bc * 4 > CAST_BLOCK_BYTES:
                continue
            if (rows // br) * (cols // bc) <= steps and (best is None or br * bc < best[0] * best[1]):
                best = (br, bc)
    return best


def _with_casts(kernel_fn, n_in, n_out, weights, steps, nj):
    plans, in_specs, out_specs, out_shapes, operands = [], [], [], [], []
    start = 0
    for w in weights:
        _, rows, cols = w.shape
        plan = _cast_plan(rows, cols, steps - start)
        plans.append(plan)
        if plan is None:
            continue
        br, bc = plan
        ncb = cols // bc
        nb = (rows // br) * ncb

        def block(i, j, start=start, nb=nb, ncb=ncb):
            t = jnp.clip(i * nj + j - start, 0, nb - 1)
            return t // ncb, t % ncb

        in_specs.append(pl.BlockSpec((None, br, bc), lambda i, j, block=block: (0,) + block(i, j)))
        out_specs.append(pl.BlockSpec((br, bc), block))
        out_shapes.append(jax.ShapeDtypeStruct((rows, cols), BF16))
        operands.append(w)
        start += nb
    n_cast = len(operands)

    def wrapped(*refs):
        ins, rest = refs[:n_in], refs[n_in:]
        cast_in, rest = rest[:n_cast], rest[n_cast:]
        outs, rest = rest[:n_out], rest[n_out:]
        cast_out, scratch = rest[:n_cast], rest[n_cast:]
        kernel_fn(*ins, *outs, *scratch)
        for ci, co in zip(cast_in, cast_out):
            co[...] = ci[...].astype(BF16)

    def finish(cast_results):
        it = iter(cast_results)
        return [w[0].astype(BF16) if p is None else next(it)
                for w, p in zip(weights, plans)]

    return wrapped, in_specs, out_specs, out_shapes, operands, finish


def _proj_rot_kernel(x_ref, w_ref, cos_ref, sin_ref, o_ref, *, n_q_blocks):
    j = pl.program_id(1)
    acc = jnp.dot(x_ref[...], w_ref[...], preferred_element_type=F32)
    scale = jnp.where(j < n_q_blocks, RET_DK ** -0.5, 1.0).astype(F32)
    cos = cos_ref[...]
    sin = sin_ref[...]
    half = RET_DK // 2
    for h in range(o_ref.shape[1] // RET_DK):
        lo = h * RET_DK
        x1 = acc[:, lo:lo + half]
        x2 = acc[:, lo + half:lo + RET_DK]
        o_ref[:, lo:lo + half] = ((x1 * cos - x2 * sin) * scale).astype(BF16)
        o_ref[:, lo + half:lo + RET_DK] = ((x1 * sin + x2 * cos) * scale).astype(BF16)


def _proj_plain_kernel(x_ref, w_ref, o_ref):
    o_ref[...] = jnp.dot(x_ref[...], w_ref[...],
                         preferred_element_type=F32).astype(o_ref.dtype)


def _in_proj(hn, w_in, cos, sin, tp, later_weights_rot, later_weights_rest):
    m, d = hn.shape
    bm = _largest_divisor(tp, 1056, BF16_SUBLANES)
    bn = 1024
    pos_blocks = tp // bm
    n_rot = 2 * RET_QK_WIDTH
    n_rest = w_in.shape[1] - n_rot

    grid = (m // bm, n_rot // bn)
    body, c_in, c_out, c_shape, c_ops, finish_rot = _with_casts(
        functools.partial(_proj_rot_kernel, n_q_blocks=RET_QK_WIDTH // bn),
        4, 1, later_weights_rot, grid[0] * grid[1], grid[1])
    qk, *cast_rot = pl.pallas_call(
        body,
        grid=grid,
        in_specs=[
            pl.BlockSpec((bm, d), lambda i, j: (i, 0)),
            pl.BlockSpec((d, bn), lambda i, j: (0, j)),
            pl.BlockSpec((bm, RET_DK // 2), lambda i, j: (i % pos_blocks, 0)),
            pl.BlockSpec((bm, RET_DK // 2), lambda i, j: (i % pos_blocks, 0)),
        ] + c_in,
        out_specs=[pl.BlockSpec((bm, bn), lambda i, j: (i, j))] + c_out,
        out_shape=[jax.ShapeDtypeStruct((m, n_rot), BF16)] + c_shape,
        compiler_params=_params("arbitrary", "arbitrary"),
        name="in_proj_rot",
    )(hn, w_in, cos, sin, *c_ops)

    rot_blocks = n_rot // bn
    grid = (m // bm, n_rest // bn)
    body, c_in, c_out, c_shape, c_ops, finish_rest = _with_casts(
        _proj_plain_kernel, 2, 1, later_weights_rest, grid[0] * grid[1], grid[1])
    rest, *cast_rest = pl.pallas_call(
        body,
        grid=grid,
        in_specs=[
            pl.BlockSpec((bm, d), lambda i, j: (i, 0)),
            pl.BlockSpec((d, bn), lambda i, j: (0, j + rot_blocks)),
        ] + c_in,
        out_specs=[pl.BlockSpec((bm, bn), lambda i, j: (i, j))] + c_out,
        out_shape=[jax.ShapeDtypeStruct((m, n_rest), BF16)] + c_shape,
        compiler_params=_params("arbitrary", "arbitrary"),
        name="in_proj_rest",
    )(hn, w_in, *c_ops)
    return qk, rest, finish_rot(cast_rot), finish_rest(cast_rest)


def _ret_log_gamma():
    return np.log(1.0 - 2.0 ** (-5.0 - np.arange(RET_HEADS, dtype=np.float64)))


def _ret_kernel(q_ref, k_ref, v_ref, g_ref, intra_ref, qd_ref, kd_ref, gn_ref,
                o_ref, state_ref):
    c = pl.program_id(1)

    @pl.when(c == 0)
    def _():
        state_ref[...] = jnp.zeros(state_ref.shape, F32)

    chunk_decay = np.exp(CHUNK * _ret_log_gamma()).astype(np.float32)
    for n in range(q_ref.shape[0]):
        for h in range(RET_HEADS):
            sl = slice(h * RET_DK, (h + 1) * RET_DK)
            q = q_ref[n, :, sl]
            k = k_ref[n, :, sl]
            v = v_ref[n, :, sl]
            s = lax.dot_general(q, k, _NT, preferred_element_type=F32) * intra_ref[h]
            intra = jnp.dot(s.astype(BF16), v, preferred_element_type=F32)
            st = state_ref[n, h]
            qd = qd_ref[h]
            kd = kd_ref[h]
            cross = jnp.dot(q, st.astype(BF16), preferred_element_type=F32)
            cross = cross * jnp.concatenate([qd, qd], axis=1)
            kdk = (k.astype(F32) * jnp.concatenate([kd, kd], axis=1)).astype(BF16)
            upd = lax.dot_general(kdk, v, _TN, preferred_element_type=F32)
            state_ref[n, h] = st * float(chunk_decay[h]) + upd
            y = intra + cross
            yn = _rms(y, gn_ref[:, sl])
            g = g_ref[n, :, sl].astype(F32)
            o_ref[n, :, sl] = (g * jax.nn.sigmoid(g) * yn).astype(BF16)


def _retention(qk, rest, gn_w, b, tp, later_weights):
    m = qk.shape[0]
    nblk = tp // CHUNK
    lg = jnp.log(1.0 - 2.0 ** (-5.0 - jnp.arange(RET_HEADS, dtype=F32)))
    idx = jnp.arange(CHUNK, dtype=F32)
    diff = idx[:, None] - idx[None, :]
    intra = jnp.where(diff[None] >= 0,
                      jnp.exp(jnp.maximum(diff, 0.0)[None] * lg[:, None, None]), 0.0)
    qd = jnp.exp((idx[None, :] + 1.0) * lg[:, None])
    kd = jnp.exp((CHUNK - 1.0 - idx[None, :]) * lg[:, None])
    qd = jnp.broadcast_to(qd[:, :, None], (RET_HEADS, CHUNK, CHUNK))
    kd = jnp.broadcast_to(kd[:, :, None], (RET_HEADS, CHUNK, CHUNK))
    table = pl.BlockSpec((RET_HEADS, CHUNK, CHUNK), lambda i, c: (0, 0, 0))
    grid = (b // ROW_BATCH, nblk)
    body, c_in, c_out, c_shape, c_ops, finish = _with_casts(
        _ret_kernel, 8, 1, later_weights, grid[0] * grid[1], nblk)

    def cols(block, width):
        return pl.BlockSpec((ROW_BATCH, CHUNK, width), lambda i, c: (i, c, block))

    qk = qk.reshape(b, tp, -1)
    rest = rest.reshape(b, tp, -1)
    out, *cast = pl.pallas_call(
        body,
        grid=grid,
        in_specs=[cols(0, RET_QK_WIDTH), cols(1, RET_QK_WIDTH),
                  cols(0, RET_WIDTH), cols(1, RET_WIDTH),
                  table, table, table,
                  pl.BlockSpec((1, RET_WIDTH), lambda i, c: (0, 0))] + c_in,
        out_specs=[cols(0, RET_WIDTH)] + c_out,
        out_shape=[jax.ShapeDtypeStruct((b, tp, RET_WIDTH), BF16)] + c_shape,
        scratch_shapes=[pltpu.VMEM((ROW_BATCH, RET_HEADS, RET_DK, RET_DV), F32)],
        compiler_params=_params("arbitrary", "arbitrary"),
        name="retention",
    )(qk, qk, rest, rest, intra, qd, kd, gn_w, *c_ops)
    return out.reshape(m, RET_WIDTH), finish(cast)


SB_GROUP = 8
SB_UNDERFLOW = 110.0
SB_FIRST_TILES = 3
LOG2E = 1.4426950408889634


def _sb_kernel(q_ref, k_ref, v_ref, tt_ref, nw_ref, o_ref, *, nblk):
    scale = LOG2E / math.sqrt(SB_DH)
    row = lax.broadcasted_iota(jnp.int32, (CHUNK, CHUNK), 0)
    col = lax.broadcasted_iota(jnp.int32, (CHUNK, CHUNK), 1)
    heads = [slice(g * SB_DH, (g + 1) * SB_DH) for g in range(SB_GROUP)]

    def step(qs, tiles, cs, accs):
        vs, masks, log_beta, split = [], [], [], []
        for j, mask in tiles:
            start = pl.multiple_of(jnp.maximum(j, 0) * CHUNK, CHUNK)
            k = k_ref[pl.ds(start, CHUNK), :]
            vs.append(v_ref[pl.ds(start, CHUNK), :])
            masks.append(mask)
            for g in range(SB_GROUP):
                zs = lax.dot_general(qs[g], k[:, heads[g]], _NT,
                                     preferred_element_type=F32) * scale
                sp = jnp.maximum(zs, 0.0) + jnp.log2(1.0 + jnp.exp2(-jnp.abs(zs)))
                log_beta.append(zs - sp)
                log_keep = jnp.where(mask, -sp, 0.0)
                hi = log_keep.astype(BF16)
                lo = (log_keep - hi.astype(F32)).astype(BF16)
                split.append(jnp.concatenate([hi, lo], axis=1))
        sums = jnp.dot(jnp.concatenate(split, axis=0), tt_ref[...],
                       preferred_element_type=F32)
        cs, accs = list(cs), list(accs)
        for t in range(len(tiles)):
            for g in range(SB_GROUP):
                n = t * SB_GROUP + g
                s = sums[n * CHUNK:(n + 1) * CHUNK]
                after = s[:, :CHUNK] + cs[g]
                w = jnp.where(masks[t], jnp.exp2(log_beta[n] + after), 0.0)
                accs[g] = accs[g] + jnp.dot(w.astype(BF16), vs[t][:, heads[g]],
                                            preferred_element_type=F32)
                cs[g] = cs[g] + s[:, CHUNK:]
        return tuple(cs), tuple(accs)

    def least_decayed(cs):
        return jnp.max(functools.reduce(jnp.maximum, cs))

    def valid_keys(j):
        return j * CHUNK + col >= N_PAD

    def q_block(i, carry):
        start = pl.multiple_of(i * CHUNK, CHUNK)
        q = q_ref[pl.ds(start, CHUNK), :]
        qs = [q[:, heads[g]] for g in range(SB_GROUP)]
        zero = (jnp.zeros((CHUNK, CHUNK), F32),) * SB_GROUP
        first = [(i, jnp.logical_and(col < row, valid_keys(i)))]
        first += [(i - t, valid_keys(i - t)) for t in range(1, SB_FIRST_TILES)]
        cs, accs = step(qs, first, zero, zero)

        def cond(st):
            return jnp.logical_and(st[0] >= 0, st[1] > -SB_UNDERFLOW * LOG2E)

        def body(st):
            j = st[0]
            cs_, accs_ = step(qs, [(j, valid_keys(j))], st[2:2 + SB_GROUP],
                              st[2 + SB_GROUP:])
            return (j - 1, least_decayed(cs_)) + cs_ + accs_

        st = lax.while_loop(cond, body,
                            (i - SB_FIRST_TILES, least_decayed(cs)) + cs + accs)
        accs = st[2 + SB_GROUP:]
        for g in range(SB_GROUP):
            o_ref[pl.ds(start, CHUNK), heads[g]] = _rms(
                accs[g], nw_ref[:, heads[g]]).astype(BF16)
        return carry

    lax.fori_loop(0, nblk, q_block, 0)


def _stick_breaking(rest, nw, b, tp):
    m = rest.shape[0]
    nblk = tp // CHUNK
    width = SB_GROUP * SB_DH
    groups = SB_HEADS // SB_GROUP
    base = 2 * RET_WIDTH // width
    idx = jnp.arange(CHUNK)
    tri = (idx[:, None] > idx[None, :]).astype(BF16)
    tt = jnp.concatenate([tri, jnp.ones((CHUNK, CHUNK), BF16)], axis=1)
    tt = jnp.concatenate([tt, tt], axis=0)
    return pl.pallas_call(
        functools.partial(_sb_kernel, nblk=nblk),
        grid=(b, groups),
        in_specs=[
            pl.BlockSpec((tp, width), lambda i, h: (i, base + h),
                         pipeline_mode=pl.Buffered(1)),
            pl.BlockSpec((tp, width), lambda i, h: (i, base + groups + h),
                         pipeline_mode=pl.Buffered(1)),
            pl.BlockSpec((tp, width), lambda i, h: (i, base + 2 * groups + h),
                         pipeline_mode=pl.Buffered(1)),
            pl.BlockSpec((2 * CHUNK, 2 * CHUNK), lambda i, h: (0, 0)),
            pl.BlockSpec((1, width), lambda i, h: (0, h)),
        ],
        out_specs=pl.BlockSpec((tp, width), lambda i, h: (i, h)),
        out_shape=jax.ShapeDtypeStruct((m, SB_WIDTH), BF16),
        compiler_params=_params("parallel", "parallel"),
        name="stick_breaking",
    )(rest, rest, rest, tt, nw)


def _out_proj_kernel(r_ref, s_ref, w1_ref, w2_ref, o_ref):
    o_ref[...] = (jnp.dot(r_ref[...], w1_ref[...], preferred_element_type=F32)
                  + jnp.dot(s_ref[...], w2_ref[...], preferred_element_type=F32)
                  ).astype(o_ref.dtype)


def _out_proj(ret_out, sb_out, w_out, tp):
    m = ret_out.shape[0]
    d = w_out.shape[1]
    bm = _largest_divisor(tp, 1056, BF16_SUBLANES)
    bn = 1024
    return pl.pallas_call(
        _out_proj_kernel,
        grid=(m // bm, d // bn),
        in_specs=[
            pl.BlockSpec((bm, RET_WIDTH), lambda i, j: (i, 0)),
            pl.BlockSpec((bm, SB_WIDTH), lambda i, j: (i, 0)),
            pl.BlockSpec((RET_WIDTH, bn), lambda i, j: (0, j)),
            pl.BlockSpec((SB_WIDTH, bn), lambda i, j: (1, j)),
        ],
        out_specs=pl.BlockSpec((bm, bn), lambda i, j: (i, j)),
        out_shape=jax.ShapeDtypeStruct((m, d), BF16),
        compiler_params=_params("parallel", "arbitrary"),
        name="out_proj",
    )(ret_out, sb_out, w_out, w_out)


def _post_attn_kernel(a_ref, x_ref, meta_ref, w1_ref, w2_ref, hn_ref):
    h = _padded_rows(x_ref, meta_ref, pl.program_id(1))
    h1 = h + _rms(a_ref[...].astype(F32), w1_ref[...])
    hn_ref[...] = _rms(h1, w2_ref[...]).astype(BF16)


def _post_attn(a, x, meta, w_post, w_pre):
    b, tp, d = a.shape
    vec = pl.BlockSpec((1, d), lambda i, r: (0, 0))
    return pl.pallas_call(
        _post_attn_kernel,
        grid=(b // ROW_BATCH, tp // CHUNK),
        in_specs=[_row_spec(d), _row_spec(d, -1),
                  pl.BlockSpec((N_META, d), lambda i, r: (0, 0)), vec, vec],
        out_specs=_row_spec(d),
        out_shape=jax.ShapeDtypeStruct((b, tp, d), BF16),
        compiler_params=_params("parallel", "arbitrary"),
        name="post_attn",
    )(a, x, meta, w_post, w_pre)


FFN_SUB_ROWS = 352


def _ffn_up_kernel(x_ref, xh_ref, wg_ref, wu_ref, cw_ref, cb_ref, o_ref, g_ref):
    bm = x_ref.shape[0]
    sub = FFN_SUB_ROWS if bm % FFN_SUB_ROWS == 0 else bm
    wg = wg_ref[...]
    wu = wu_ref[...]
    g_ref[:HALO, :] = jnp.dot(xh_ref[...], wg, preferred_element_type=F32)
    for r in range(bm // sub):
        x = x_ref[r * sub:(r + 1) * sub, :]
        lo = HALO + r * sub
        g_ref[lo:lo + sub, :] = jnp.dot(x, wg, preferred_element_type=F32)
        up = jnp.dot(x, wu, preferred_element_type=F32)
        ext = g_ref[lo - 8:lo + sub, :]
        prev1 = pltpu.roll(ext, 1, axis=0)
        prev2 = pltpu.roll(prev1, 1, axis=0)
        conv = cb_ref[...] + prev2[8:] * cw_ref[0:1, :]
        conv = conv + prev1[8:] * cw_ref[1:2, :]
        conv = conv + ext[8:] * cw_ref[2:3, :]
        o_ref[r * sub:(r + 1) * sub, :] = (conv * jax.nn.sigmoid(conv) * up).astype(BF16)


def _ffn_up(hn, w_up, conv_w, conv_b, tp):
    m, d = hn.shape
    dff = w_up.shape[1] // 2
    bm = _largest_divisor(tp, 4224, BF16_SUBLANES)
    bn = 256
    halo_blocks = bm // HALO
    up_base = dff // bn
    return pl.pallas_call(
        _ffn_up_kernel,
        grid=(m // bm, dff // bn),
        in_specs=[
            pl.BlockSpec((bm, d), lambda i, j: (i, 0), pipeline_mode=pl.Buffered(1)),
            pl.BlockSpec((HALO, d), lambda i, j: (jnp.maximum(i * halo_blocks - 1, 0), 0)),
            pl.BlockSpec((d, bn), lambda i, j: (0, j)),
            pl.BlockSpec((d, bn), lambda i, j: (0, j + up_base)),
            pl.BlockSpec((CONV_W, bn), lambda i, j: (0, j)),
            pl.BlockSpec((1, bn), lambda i, j: (0, j)),
        ],
        out_specs=pl.BlockSpec((bm, bn), lambda i, j: (i, j)),
        out_shape=jax.ShapeDtypeStruct((m, dff), BF16),
        scratch_shapes=[pltpu.VMEM((bm + HALO, bn), F32)],
        compiler_params=_params("parallel", "arbitrary"),
        name="ffn_up",
    )(hn, hn, w_up, w_up, conv_w, conv_b)


def _ffn_down(act, w_down, tp):
    m, dff = act.shape
    d = w_down.shape[1]
    seq = tp - CHUNK
    bm = _largest_divisor(seq, 528, BF16_SUBLANES)
    bn = 512
    per_batch = seq // bm

    def token_rows(i, j):
        start = (i // per_batch) * tp + CHUNK + (i % per_batch) * bm
        return pl.multiple_of(start, BF16_SUBLANES), 0

    return pl.pallas_call(
        _proj_plain_kernel,
        grid=(m // tp * per_batch, d // bn),
        in_specs=[
            pl.BlockSpec((pl.Element(bm), pl.Element(dff)), token_rows),
            pl.BlockSpec((dff, bn), lambda i, j: (0, j)),
        ],
        out_specs=pl.BlockSpec((bm, bn), lambda i, j: (i, j)),
        out_shape=jax.ShapeDtypeStruct((m // tp * seq, d), BF16),
        compiler_params=_params("parallel", "arbitrary"),
        name="ffn_down",
    )(act, w_down)


def _final_kernel(f_ref, a_ref, x_ref, w1_ref, w2_ref, o_ref):
    h1 = x_ref[...] + _rms(a_ref[...].astype(F32), w1_ref[...])
    o_ref[...] = h1 + _rms(f_ref[...].astype(F32), w2_ref[...])


def _final(f, a, x, w_post, w_final):
    b, seq, d = x.shape
    vec = pl.BlockSpec((1, d), lambda i, r: (0, 0))
    return pl.pallas_call(
        _final_kernel,
        grid=(b // ROW_BATCH, seq // CHUNK),
        in_specs=[_row_spec(d), _row_spec(d, 1), _row_spec(d), vec, vec],
        out_specs=_row_spec(d),
        out_shape=jax.ShapeDtypeStruct((b, seq, d), F32),
        compiler_params=_params("parallel", "parallel"),
        name="final",
    )(f, a, x, w_post, w_final)


def _rotary_tables(tp):
    half = RET_DK // 2
    pos = (jnp.arange(tp) - N_PAD).astype(F32)
    inv = ROPE_BASE ** (-jnp.arange(half, dtype=F32) / half)
    ang = pos[:, None] * inv[None, :]
    return jnp.cos(ang), jnp.sin(ang)


def kernel(x, meta_tokens, attn_pre_norm_w, w_in, ret_gn_w, sb_norm_w, w_out,
           attn_post_norm_w, ffn_pre_norm_w, w_up, conv_w, conv_b, w_down, ffn_post_norm_w):
    b, seq, d = x.shape
    assert seq % CHUNK == 0 and w_in.shape[0] == 1
    tp = seq + CHUNK
    m = b * tp

    hn = _prep(x, meta_tokens, attn_pre_norm_w).reshape(m, d)

    cos, sin = _rotary_tables(tp)
    qk, rest, (w_down_bf,), (w_up_bf,) = _in_proj(
        hn, w_in[0].astype(BF16), cos, sin, tp, [w_down], [w_up])
    ret_out, (w_out_bf,) = _retention(qk, rest, ret_gn_w, b, tp, [w_out])
    sb_out = _stick_breaking(rest, sb_norm_w, b, tp)
    a = _out_proj(ret_out, sb_out, w_out_bf, tp)
    a = a.reshape(b, tp, d)
    hn2 = _post_attn(a, x, meta_tokens, attn_post_norm_w, ffn_pre_norm_w)

    act = _ffn_up(hn2.reshape(m, d), w_up_bf, conv_w[0], conv_b, tp)
    f = _ffn_down(act, w_down_bf, tp)
    return _final(f.reshape(b, seq, d), a, x, attn_post_norm_w, ffn_post_norm_w)
```

```python
import functools
import math

import numpy as np
import jax
import jax.numpy as jnp
from jax import lax
from jax.experimental import pallas as pl
from jax.experimental.pallas import tpu as pltpu

N_META = 16
CHUNK = 128
N_PAD = CHUNK - N_META
RET_HEADS = 8
RET_DK = 256
RET_DV = 256
SB_HEADS = 16
SB_DH = 128
RET_WIDTH = RET_HEADS * RET_DV
RET_QK_WIDTH = RET_HEADS * RET_DK
SB_WIDTH = SB_HEADS * SB_DH
CONV_W = 3
EPS = 1e-6
ROPE_BASE = 10000.0

BF16_SUBLANES = 16
HALO = BF16_SUBLANES
assert CONV_W - 1 <= HALO <= N_META
VMEM_LIMIT_BYTES = 58 * 1024 * 1024

PROJ_BLOCK_ROWS = 1056
PROJ_BLOCK_COLS = 1024
FFN_UP_BLOCK_COLS = 256
FFN_DOWN_BLOCK_ROWS = 528
FFN_DOWN_BLOCK_COLS = 512

F32 = jnp.float32
BF16 = jnp.bfloat16

_NT = (((1,), (1,)), ((), ()))
_TN = (((0,), (0,)), ((), ()))


def _params(*semantics):
    return pltpu.CompilerParams(dimension_semantics=semantics,
                                vmem_limit_bytes=VMEM_LIMIT_BYTES)


def _largest_divisor(n, target, multiple):
    best = None
    for d in range(multiple, min(n, target) + 1, multiple):
        if n % d == 0:
            best = d
    assert best is not None, (n, target, multiple)
    return best


def _rms(v, w):
    ms = jnp.mean(v * v, axis=-1, keepdims=True)
    return v * lax.rsqrt(ms + EPS) * w


ROW_BATCH = 2


def _padded_rows(x_ref, meta_ref, r):
    d = x_ref.shape[-1]
    first = jnp.concatenate([jnp.zeros((N_PAD, d), F32), meta_ref[...]], axis=0)
    return jnp.where(r == 0, first[None], x_ref[...])


def _row_spec(d, shift=0):
    return pl.BlockSpec((ROW_BATCH, CHUNK, d),
                        lambda i, r: (i, jnp.maximum(r + shift, 0), 0))


def _prep_kernel(x_ref, meta_ref, w_ref, hn_ref):
    h = _padded_rows(x_ref, meta_ref, pl.program_id(1))
    hn_ref[...] = _rms(h, w_ref[...]).astype(BF16)


def _prep(x, meta, w):
    b, seq, d = x.shape
    nblk = seq // CHUNK + 1
    tp = nblk * CHUNK
    return pl.pallas_call(
        _prep_kernel,
        grid=(b // ROW_BATCH, nblk),
        in_specs=[
            _row_spec(d, -1),
            pl.BlockSpec((N_META, d), lambda i, r: (0, 0)),
            pl.BlockSpec((1, d), lambda i, r: (0, 0)),
        ],
        out_specs=_row_spec(d),
        out_shape=jax.ShapeDtypeStruct((b, tp, d), BF16),
        compiler_params=_params("parallel", "arbitrary"),
        name="prep",
    )(x, meta, w)


CAST_BLOCK_BYTES = 3 * 1024 * 1024


def _cast_plan(rows, cols, steps):
    best = None
    for br in range(BF16_SUBLANES, rows + 1, BF16_SUBLANES):
        if rows % br:
            continue
        for bc in range(128, cols + 1, 128):
            if cols % bc or br * bc * 4 > CAST_BLOCK_BYTES:
                continue
            if (rows // br) * (cols // bc) <= steps and (best is None or br * bc < best[0] * best[1]):
                best = (br, bc)
    return best


def _with_casts(kernel_fn, n_in, n_out, weights, steps, nj):
    plans, in_specs, out_specs, out_shapes, operands = [], [], [], [], []
    start = 0
    for w in weights:
        _, rows, cols = w.shape
        plan = _cast_plan(rows, cols, steps - start)
        plans.append(plan)
        if plan is None:
            continue
        br, bc = plan
        ncb = cols // bc
        nb = (rows // br) * ncb

        def block(i, j, start=start, nb=nb, ncb=ncb):
            t = jnp.clip(i * nj + j - start, 0, nb - 1)
            return t // ncb, t % ncb

        in_specs.append(pl.BlockSpec((None, br, bc), lambda i, j, block=block: (0,) + block(i, j)))
        out_specs.append(pl.BlockSpec((br, bc), block))
        out_shapes.append(jax.ShapeDtypeStruct((rows, cols), BF16))
        operands.append(w)
        start += nb
    n_cast = len(operands)

    def wrapped(*refs):
        ins, rest = refs[:n_in], refs[n_in:]
        cast_in, rest = rest[:n_cast], rest[n_cast:]
        outs, rest = rest[:n_out], rest[n_out:]
        cast_out, scratch = rest[:n_cast], rest[n_cast:]
        kernel_fn(*ins, *outs, *scratch)
        for ci, co in zip(cast_in, cast_out):
            co[...] = ci[...].astype(BF16)

    def finish(cast_results):
        it = iter(cast_results)
        return [w[0].astype(BF16) if p is None else next(it)
                for w, p in zip(weights, plans)]

    return wrapped, in_specs, out_specs, out_shapes, operands, finish


def _proj_rot_kernel(x_ref, w_ref, cos_ref, sin_ref, o_ref, *, n_q_blocks):
    j = pl.program_id(1)
    acc = jnp.dot(x_ref[...], w_ref[...], preferred_element_type=F32)
    scale = jnp.where(j < n_q_blocks, RET_DK ** -0.5, 1.0).astype(F32)
    cos = cos_ref[...]
    sin = sin_ref[...]
    half = RET_DK // 2
    for h in range(o_ref.shape[1] // RET_DK):
        lo = h * RET_DK
        x1 = acc[:, lo:lo + half]
        x2 = acc[:, lo + half:lo + RET_DK]
        o_ref[:, lo:lo + half] = ((x1 * cos - x2 * sin) * scale).astype(BF16)
        o_ref[:, lo + half:lo + RET_DK] = ((x1 * sin + x2 * cos) * scale).astype(BF16)


def _proj_plain_kernel(x_ref, w_ref, o_ref):
    o_ref[...] = jnp.dot(x_ref[...], w_ref[...],
                         preferred_element_type=F32).astype(o_ref.dtype)


def _in_proj(hn, w_in, cos, sin, tp, later_weights_rot, later_weights_rest):
    m, d = hn.shape
    bm = _largest_divisor(tp, PROJ_BLOCK_ROWS, BF16_SUBLANES)
    bn = PROJ_BLOCK_COLS
    pos_blocks = tp // bm
    n_rot = 2 * RET_QK_WIDTH
    n_rest = w_in.shape[1] - n_rot

    grid = (m // bm, n_rot // bn)
    body, c_in, c_out, c_shape, c_ops, finish_rot = _with_casts(
        functools.partial(_proj_rot_kernel, n_q_blocks=RET_QK_WIDTH // bn),
        4, 1, later_weights_rot, grid[0] * grid[1], grid[1])
    qk, *cast_rot = pl.pallas_call(
        body,
        grid=grid,
        in_specs=[
            pl.BlockSpec((bm, d), lambda i, j: (i, 0)),
            pl.BlockSpec((d, bn), lambda i, j: (0, j)),
            pl.BlockSpec((bm, RET_DK // 2), lambda i, j: (i % pos_blocks, 0)),
            pl.BlockSpec((bm, RET_DK // 2), lambda i, j: (i % pos_blocks, 0)),
        ] + c_in,
        out_specs=[pl.BlockSpec((bm, bn), lambda i, j: (i, j))] + c_out,
        out_shape=[jax.ShapeDtypeStruct((m, n_rot), BF16)] + c_shape,
        compiler_params=_params("arbitrary", "arbitrary"),
        name="in_proj_rot",
    )(hn, w_in, cos, sin, *c_ops)

    rot_blocks = n_rot // bn
    grid = (m // bm, n_rest // bn)
    body, c_in, c_out, c_shape, c_ops, finish_rest = _with_casts(
        _proj_plain_kernel, 2, 1, later_weights_rest, grid[0] * grid[1], grid[1])
    rest, *cast_rest = pl.pallas_call(
        body,
        grid=grid,
        in_specs=[
            pl.BlockSpec((bm, d), lambda i, j: (i, 0)),
            pl.BlockSpec((d, bn), lambda i, j: (0, j + rot_blocks)),
        ] + c_in,
        out_specs=[pl.BlockSpec((bm, bn), lambda i, j: (i, j))] + c_out,
        out_shape=[jax.ShapeDtypeStruct((m, n_rest), BF16)] + c_shape,
        compiler_params=_params("arbitrary", "arbitrary"),
        name="in_proj_rest",
    )(hn, w_in, *c_ops)
    return qk, rest, finish_rot(cast_rot), finish_rest(cast_rest)


def _ret_log_gamma():
    return np.log(1.0 - 2.0 ** (-5.0 - np.arange(RET_HEADS, dtype=np.float64)))


def _ret_kernel(q_ref, k_ref, v_ref, g_ref, intra_ref, qd_ref, kd_ref, gn_ref,
                o_ref, state_ref):
    c = pl.program_id(1)

    @pl.when(c == 0)
    def _():
        state_ref[...] = jnp.zeros(state_ref.shape, F32)

    chunk_decay = np.exp(CHUNK * _ret_log_gamma()).astype(np.float32)
    for n in range(q_ref.shape[0]):
        for h in range(RET_HEADS):
            sl = slice(h * RET_DK, (h + 1) * RET_DK)
            q = q_ref[n, :, sl]
            k = k_ref[n, :, sl]
            v = v_ref[n, :, sl]
            s = lax.dot_general(q, k, _NT, preferred_element_type=F32) * intra_ref[h]
            intra = jnp.dot(s.astype(BF16), v, preferred_element_type=F32)
            st = state_ref[n, h]
            qd = qd_ref[h]
            kd = kd_ref[h]
            cross = jnp.dot(q, st.astype(BF16), preferred_element_type=F32)
            cross = cross * jnp.concatenate([qd, qd], axis=1)
            kdk = (k.astype(F32) * jnp.concatenate([kd, kd], axis=1)).astype(BF16)
            upd = lax.dot_general(kdk, v, _TN, preferred_element_type=F32)
            state_ref[n, h] = st * float(chunk_decay[h]) + upd
            y = intra + cross
            yn = _rms(y, gn_ref[:, sl])
            g = g_ref[n, :, sl].astype(F32)
            o_ref[n, :, sl] = (g * jax.nn.sigmoid(g) * yn).astype(BF16)


def _retention(qk, rest, gn_w, b, tp, later_weights):
    m = qk.shape[0]
    nblk = tp // CHUNK
    lg = jnp.log(1.0 - 2.0 ** (-5.0 - jnp.arange(RET_HEADS, dtype=F32)))
    idx = jnp.arange(CHUNK, dtype=F32)
    diff = idx[:, None] - idx[None, :]
    intra = jnp.where(diff[None] >= 0,
                      jnp.exp(jnp.maximum(diff, 0.0)[None] * lg[:, None, None]), 0.0)
    qd = jnp.exp((idx[None, :] + 1.0) * lg[:, None])
    kd = jnp.exp((CHUNK - 1.0 - idx[None, :]) * lg[:, None])
    qd = jnp.broadcast_to(qd[:, :, None], (RET_HEADS, CHUNK, CHUNK))
    kd = jnp.broadcast_to(kd[:, :, None], (RET_HEADS, CHUNK, CHUNK))
    table = pl.BlockSpec((RET_HEADS, CHUNK, CHUNK), lambda i, c: (0, 0, 0))
    grid = (b // ROW_BATCH, nblk)
    body, c_in, c_out, c_shape, c_ops, finish = _with_casts(
        _ret_kernel, 8, 1, later_weights, grid[0] * grid[1], nblk)

    def cols(block, width):
        return pl.BlockSpec((ROW_BATCH, CHUNK, width), lambda i, c: (i, c, block))

    qk = qk.reshape(b, tp, -1)
    rest = rest.reshape(b, tp, -1)
    out, *cast = pl.pallas_call(
        body,
        grid=grid,
        in_specs=[cols(0, RET_QK_WIDTH), cols(1, RET_QK_WIDTH),
                  cols(0, RET_WIDTH), cols(1, RET_WIDTH),
                  table, table, table,
                  pl.BlockSpec((1, RET_WIDTH), lambda i, c: (0, 0))] + c_in,
        out_specs=[cols(0, RET_WIDTH)] + c_out,
        out_shape=[jax.ShapeDtypeStruct((b, tp, RET_WIDTH), BF16)] + c_shape,
        scratch_shapes=[pltpu.VMEM((ROW_BATCH, RET_HEADS, RET_DK, RET_DV), F32)],
        compiler_params=_params("arbitrary", "arbitrary"),
        name="retention",
    )(qk, qk, rest, rest, intra, qd, kd, gn_w, *c_ops)
    return out.reshape(m, RET_WIDTH), finish(cast)


SB_GROUP = 8
SB_UNDERFLOW = 110.0
SB_FIRST_TILES = 3
LOG2E = 1.4426950408889634


def _sb_kernel(q_ref, k_ref, v_ref, tt_ref, nw_ref, o_ref, *, nblk):
    scale = LOG2E / math.sqrt(SB_DH)
    row = lax.broadcasted_iota(jnp.int32, (CHUNK, CHUNK), 0)
    col = lax.broadcasted_iota(jnp.int32, (CHUNK, CHUNK), 1)
    heads = [slice(g * SB_DH, (g + 1) * SB_DH) for g in range(SB_GROUP)]

    def step(qs, tiles, cs, accs):
        vs, masks, log_beta, split = [], [], [], []
        for j, mask in tiles:
            start = pl.multiple_of(jnp.maximum(j, 0) * CHUNK, CHUNK)
            k = k_ref[pl.ds(start, CHUNK), :]
            vs.append(v_ref[pl.ds(start, CHUNK), :])
            masks.append(mask)
            for g in range(SB_GROUP):
                zs = lax.dot_general(qs[g], k[:, heads[g]], _NT,
                                     preferred_element_type=F32) * scale
                sp = jnp.maximum(zs, 0.0) + jnp.log2(1.0 + jnp.exp2(-jnp.abs(zs)))
                log_beta.append(zs - sp)
                log_keep = jnp.where(mask, -sp, 0.0)
                hi = log_keep.astype(BF16)
                lo = (log_keep - hi.astype(F32)).astype(BF16)
                split.append(jnp.concatenate([hi, lo], axis=1))
        sums = jnp.dot(jnp.concatenate(split, axis=0), tt_ref[...],
                       preferred_element_type=F32)
        cs, accs = list(cs), list(accs)
        for t in range(len(tiles)):
            for g in range(SB_GROUP):
                n = t * SB_GROUP + g
                s = sums[n * CHUNK:(n + 1) * CHUNK]
                after = s[:, :CHUNK] + cs[g]
                w = jnp.where(masks[t], jnp.exp2(log_beta[n] + after), 0.0)
                accs[g] = accs[g] + jnp.dot(w.astype(BF16), vs[t][:, heads[g]],
                                            preferred_element_type=F32)
                cs[g] = cs[g] + s[:, CHUNK:]
        return tuple(cs), tuple(accs)

    def least_decayed(cs):
        return jnp.max(functools.reduce(jnp.maximum, cs))

    def valid_keys(j):
        return j * CHUNK + col >= N_PAD

    def q_block(i, carry):
        start = pl.multiple_of(i * CHUNK, CHUNK)
        q = q_ref[pl.ds(start, CHUNK), :]
        qs = [q[:, heads[g]] for g in range(SB_GROUP)]
        zero = (jnp.zeros((CHUNK, CHUNK), F32),) * SB_GROUP
        first = [(i, jnp.logical_and(col < row, valid_keys(i)))]
        first += [(i - t, valid_keys(i - t)) for t in range(1, SB_FIRST_TILES)]
        cs, accs = step(qs, first, zero, zero)

        def cond(st):
            return jnp.logical_and(st[0] >= 0, st[1] > -SB_UNDERFLOW * LOG2E)

        def body(st):
            j = st[0]
            cs_, accs_ = step(qs, [(j, valid_keys(j))], st[2:2 + SB_GROUP],
                              st[2 + SB_GROUP:])
            return (j - 1, least_decayed(cs_)) + cs_ + accs_

        st = lax.while_loop(cond, body,
                            (i - SB_FIRST_TILES, least_decayed(cs)) + cs + accs)
        accs = st[2 + SB_GROUP:]
        for g in range(SB_GROUP):
            o_ref[pl.ds(start, CHUNK), heads[g]] = _rms(
                accs[g], nw_ref[:, heads[g]]).astype(BF16)
        return carry

    lax.fori_loop(0, nblk, q_block, 0)


def _stick_breaking(rest, nw, b, tp):
    m = rest.shape[0]
    nblk = tp // CHUNK
    width = SB_GROUP * SB_DH
    groups = SB_HEADS // SB_GROUP
    base = 2 * RET_WIDTH // width
    idx = jnp.arange(CHUNK)
    tri = (idx[:, None] > idx[None, :]).astype(BF16)
    tt = jnp.concatenate([tri, jnp.ones((CHUNK, CHUNK), BF16)], axis=1)
    tt = jnp.concatenate([tt, tt], axis=0)
    return pl.pallas_call(
        functools.partial(_sb_kernel, nblk=nblk),
        grid=(b, groups),
        in_specs=[
            pl.BlockSpec((tp, width), lambda i, h: (i, base + h),
                         pipeline_mode=pl.Buffered(1)),
            pl.BlockSpec((tp, width), lambda i, h: (i, base + groups + h),
                         pipeline_mode=pl.Buffered(1)),
            pl.BlockSpec((tp, width), lambda i, h: (i, base + 2 * groups + h),
                         pipeline_mode=pl.Buffered(1)),
            pl.BlockSpec((2 * CHUNK, 2 * CHUNK), lambda i, h: (0, 0)),
            pl.BlockSpec((1, width), lambda i, h: (0, h)),
        ],
        out_specs=pl.BlockSpec((tp, width), lambda i, h: (i, h)),
        out_shape=jax.ShapeDtypeStruct((m, SB_WIDTH), BF16),
        compiler_params=_params("parallel", "parallel"),
        name="stick_breaking",
    )(rest, rest, rest, tt, nw)


def _out_proj_kernel(r_ref, s_ref, w1_ref, w2_ref, o_ref):
    o_ref[...] = (jnp.dot(r_ref[...], w1_ref[...], preferred_element_type=F32)
                  + jnp.dot(s_ref[...], w2_ref[...], preferred_element_type=F32)
                  ).astype(o_ref.dtype)


def _out_proj(ret_out, sb_out, w_out, tp):
    m = ret_out.shape[0]
    d = w_out.shape[1]
    bm = _largest_divisor(tp, PROJ_BLOCK_ROWS, BF16_SUBLANES)
    bn = PROJ_BLOCK_COLS
    return pl.pallas_call(
        _out_proj_kernel,
        grid=(m // bm, d // bn),
        in_specs=[
            pl.BlockSpec((bm, RET_WIDTH), lambda i, j: (i, 0)),
            pl.BlockSpec((bm, SB_WIDTH), lambda i, j: (i, 0)),
            pl.BlockSpec((RET_WIDTH, bn), lambda i, j: (0, j)),
            pl.BlockSpec((SB_WIDTH, bn), lambda i, j: (1, j)),
        ],
        out_specs=pl.BlockSpec((bm, bn), lambda i, j: (i, j)),
        out_shape=jax.ShapeDtypeStruct((m, d), BF16),
        compiler_params=_params("parallel", "arbitrary"),
        name="out_proj",
    )(ret_out, sb_out, w_out, w_out)


def _post_attn_kernel(a_ref, x_ref, meta_ref, w1_ref, w2_ref, hn_ref):
    h = _padded_rows(x_ref, meta_ref, pl.program_id(1))
    h1 = h + _rms(a_ref[...].astype(F32), w1_ref[...])
    hn_ref[...] = _rms(h1, w2_ref[...]).astype(BF16)


def _post_attn(a, x, meta, w_post, w_pre):
    b, tp, d = a.shape
    vec = pl.BlockSpec((1, d), lambda i, r: (0, 0))
    return pl.pallas_call(
        _post_attn_kernel,
        grid=(b // ROW_BATCH, tp // CHUNK),
        in_specs=[_row_spec(d), _row_spec(d, -1),
                  pl.BlockSpec((N_META, d), lambda i, r: (0, 0)), vec, vec],
        out_specs=_row_spec(d),
        out_shape=jax.ShapeDtypeStruct((b, tp, d), BF16),
        compiler_params=_params("parallel", "arbitrary"),
        name="post_attn",
    )(a, x, meta, w_post, w_pre)


FFN_SUB_ROWS = 352


def _ffn_up_kernel(x_ref, xh_ref, wg_ref, wu_ref, cw_ref, cb_ref, o_ref, g_ref):
    bm = x_ref.shape[0]
    sub = FFN_SUB_ROWS if bm % FFN_SUB_ROWS == 0 else bm
    wg = wg_ref[...]
    wu = wu_ref[...]
    g_ref[:HALO, :] = jnp.dot(xh_ref[...], wg, preferred_element_type=F32)
    for r in range(bm // sub):
        x = x_ref[r * sub:(r + 1) * sub, :]
        lo = HALO + r * sub
        g_ref[lo:lo + sub, :] = jnp.dot(x, wg, preferred_element_type=F32)
        up = jnp.dot(x, wu, preferred_element_type=F32)
        ext = g_ref[lo - 8:lo + sub, :]
        prev1 = pltpu.roll(ext, 1, axis=0)
        prev2 = pltpu.roll(prev1, 1, axis=0)
        conv = cb_ref[...] + prev2[8:] * cw_ref[0:1, :]
        conv = conv + prev1[8:] * cw_ref[1:2, :]
        conv = conv + ext[8:] * cw_ref[2:3, :]
        o_ref[r * sub:(r + 1) * sub, :] = (conv * jax.nn.sigmoid(conv) * up).astype(BF16)


def _ffn_up(hn, w_up, conv_w, conv_b, tp):
    m, d = hn.shape
    dff = w_up.shape[1] // 2
    bm = tp
    bn = FFN_UP_BLOCK_COLS
    halo_blocks = bm // HALO
    up_base = dff // bn
    return pl.pallas_call(
        _ffn_up_kernel,
        grid=(m // bm, dff // bn),
        in_specs=[
            pl.BlockSpec((bm, d), lambda i, j: (i, 0), pipeline_mode=pl.Buffered(1)),
            pl.BlockSpec((HALO, d), lambda i, j: (jnp.maximum(i * halo_blocks - 1, 0), 0)),
            pl.BlockSpec((d, bn), lambda i, j: (0, j)),
            pl.BlockSpec((d, bn), lambda i, j: (0, j + up_base)),
            pl.BlockSpec((CONV_W, bn), lambda i, j: (0, j)),
            pl.BlockSpec((1, bn), lambda i, j: (0, j)),
        ],
        out_specs=pl.BlockSpec((bm, bn), lambda i, j: (i, j)),
        out_shape=jax.ShapeDtypeStruct((m, dff), BF16),
        scratch_shapes=[pltpu.VMEM((bm + HALO, bn), F32)],
        compiler_params=_params("parallel", "arbitrary"),
        name="ffn_up",
    )(hn, hn, w_up, w_up, conv_w, conv_b)


def _ffn_down(act, w_down, tp):
    m, dff = act.shape
    d = w_down.shape[1]
    seq = tp - CHUNK
    bm = _largest_divisor(seq, FFN_DOWN_BLOCK_ROWS, BF16_SUBLANES)
    bn = FFN_DOWN_BLOCK_COLS
    per_batch = seq // bm

    def token_rows(i, j):
        start = (i // per_batch) * tp + CHUNK + (i % per_batch) * bm
        return pl.multiple_of(start, BF16_SUBLANES), 0

    return pl.pallas_call(
        _proj_plain_kernel,
        grid=(m // tp * per_batch, d // bn),
        in_specs=[
            pl.BlockSpec((pl.Element(bm), pl.Element(dff)), token_rows),
            pl.BlockSpec((dff, bn), lambda i, j: (0, j)),
        ],
        out_specs=pl.BlockSpec((bm, bn), lambda i, j: (i, j)),
        out_shape=jax.ShapeDtypeStruct((m // tp * seq, d), BF16),
        compiler_params=_params("parallel", "arbitrary"),
        name="ffn_down",
    )(act, w_down)


def _final_kernel(f_ref, a_ref, x_ref, w1_ref, w2_ref, o_ref):
    h1 = x_ref[...] + _rms(a_ref[...].astype(F32), w1_ref[...])
    o_ref[...] = h1 + _rms(f_ref[...].astype(F32), w2_ref[...])


def _final(f, a, x, w_post, w_final):
    b, seq, d = x.shape
    vec = pl.BlockSpec((1, d), lambda i, r: (0, 0))
    return pl.pallas_call(
        _final_kernel,
        grid=(b // ROW_BATCH, seq // CHUNK),
        in_specs=[_row_spec(d), _row_spec(d, 1), _row_spec(d), vec, vec],
        out_specs=_row_spec(d),
        out_shape=jax.ShapeDtypeStruct((b, seq, d), F32),
        compiler_params=_params("parallel", "parallel"),
        name="final",
    )(f, a, x, w_post, w_final)


def _rotary_tables(tp):
    half = RET_DK // 2
    pos = (jnp.arange(tp) - N_PAD).astype(F32)
    inv = ROPE_BASE ** (-jnp.arange(half, dtype=F32) / half)
    ang = pos[:, None] * inv[None, :]
    return jnp.cos(ang), jnp.sin(ang)


def kernel(x, meta_tokens, attn_pre_norm_w, w_in, ret_gn_w, sb_norm_w, w_out,
           attn_post_norm_w, ffn_pre_norm_w, w_up, conv_w, conv_b, w_down, ffn_post_norm_w):
    b, seq, d = x.shape
    assert seq % CHUNK == 0 and w_in.shape[0] == 1 and b % ROW_BATCH == 0
    tp = seq + CHUNK
    m = b * tp

    hn = _prep(x, meta_tokens, attn_pre_norm_w).reshape(m, d)

    cos, sin = _rotary_tables(tp)
    qk, rest, (w_down_bf,), (w_up_bf,) = _in_proj(
        hn, w_in[0].astype(BF16), cos, sin, tp, [w_down], [w_up])
    ret_out, (w_out_bf,) = _retention(qk, rest, ret_gn_w, b, tp, [w_out])
    sb_out = _stick_breaking(rest, sb_norm_w, b, tp)
    a = _out_proj(ret_out, sb_out, w_out_bf, tp)
    a = a.reshape(b, tp, d)
    hn2 = _post_attn(a, x, meta_tokens, attn_post_norm_w, ffn_pre_norm_w)

    act = _ffn_up(hn2.reshape(m, d), w_up_bf, conv_w[0], conv_b, tp)
    f = _ffn_down(act, w_down_bf, tp)
    return _final(f.reshape(b, seq, d), a, x, attn_post_norm_w, ffn_post_norm_w)
```

```python
import functools
import math

import numpy as np
import jax
import jax.numpy as jnp
from jax import lax
from jax.experimental import pallas as pl
from jax.experimental.pallas import tpu as pltpu

N_META = 16
CHUNK = 128
N_PAD = CHUNK - N_META
RET_HEADS = 8
RET_DK = 256
RET_DV = 256
SB_HEADS = 16
SB_DH = 128
RET_WIDTH = RET_HEADS * RET_DV
RET_QK_WIDTH = RET_HEADS * RET_DK
SB_WIDTH = SB_HEADS * SB_DH
CONV_W = 3
EPS = 1e-6
ROPE_BASE = 10000.0

BF16_SUBLANES = 16
HALO = BF16_SUBLANES
assert CONV_W - 1 <= HALO <= N_META
VMEM_LIMIT_BYTES = 58 * 1024 * 1024

PROJ_BLOCK_ROWS = 1056
PROJ_BLOCK_COLS = 1024
FFN_UP_BLOCK_COLS = 256
FFN_DOWN_BLOCK_ROWS = 528
FFN_DOWN_BLOCK_COLS = 512

F32 = jnp.float32
BF16 = jnp.bfloat16

_NT = (((1,), (1,)), ((), ()))
_TN = (((0,), (0,)), ((), ()))


def _params(*semantics):
    return pltpu.CompilerParams(dimension_semantics=semantics,
                                vmem_limit_bytes=VMEM_LIMIT_BYTES)


def _largest_divisor(n, target, multiple):
    best = None
    for d in range(multiple, min(n, target) + 1, multiple):
        if n % d == 0:
            best = d
    assert best is not None, (n, target, multiple)
    return best


def _rms(v, w):
    ms = jnp.mean(v * v, axis=-1, keepdims=True)
    return v * lax.rsqrt(ms + EPS) * w


ROW_BATCH = 2


def _padded_rows(x_ref, meta_ref, r):
    d = x_ref.shape[-1]
    first = jnp.concatenate([jnp.zeros((N_PAD, d), F32), meta_ref[...]], axis=0)
    return jnp.where(r == 0, first[None], x_ref[...])


def _row_spec(d, shift=0):
    return pl.BlockSpec((ROW_BATCH, CHUNK, d),
                        lambda i, r: (i, jnp.maximum(r + shift, 0), 0))


def _prep_kernel(x_ref, meta_ref, w_ref, hn_ref):
    h = _padded_rows(x_ref, meta_ref, pl.program_id(1))
    hn_ref[...] = _rms(h, w_ref[...]).astype(BF16)


def _prep(x, meta, w):
    b, seq, d = x.shape
    nblk = seq // CHUNK + 1
    tp = nblk * CHUNK
    return pl.pallas_call(
        _prep_kernel,
        grid=(b // ROW_BATCH, nblk),
        in_specs=[
            _row_spec(d, -1),
            pl.BlockSpec((N_META, d), lambda i, r: (0, 0)),
            pl.BlockSpec((1, d), lambda i, r: (0, 0)),
        ],
        out_specs=_row_spec(d),
        out_shape=jax.ShapeDtypeStruct((b, tp, d), BF16),
        compiler_params=_params("parallel", "arbitrary"),
        name="prep",
    )(x, meta, w)


CAST_BLOCK_BYTES = 3 * 1024 * 1024


def _cast_plan(rows, cols, steps):
    best = None
    for br in range(BF16_SUBLANES, rows + 1, BF16_SUBLANES):
        if rows % br:
            continue
        for bc in range(128, cols + 1, 128):
            if cols % bc or br * bc * 4 > CAST_BLOCK_BYTES:
                continue
            if (rows // br) * (cols // bc) <= steps and (best is None or br * bc < best[0] * best[1]):
                best = (br, bc)
    return best


def _with_casts(kernel_fn, n_in, n_out, weights, steps, nj):
    plans, in_specs, out_specs, out_shapes, operands = [], [], [], [], []
    start = 0
    for w in weights:
        _, rows, cols = w.shape
        plan = _cast_plan(rows, cols, steps - start)
        plans.append(plan)
        if plan is None:
            continue
        br, bc = plan
        ncb = cols // bc
        nb = (rows // br) * ncb

        def block(i, j, start=start, nb=nb, ncb=ncb):
            t = jnp.clip(i * nj + j - start, 0, nb - 1)
            return t // ncb, t % ncb

        in_specs.append(pl.BlockSpec((None, br, bc), lambda i, j, block=block: (0,) + block(i, j)))
        out_specs.append(pl.BlockSpec((br, bc), block))
        out_shapes.append(jax.ShapeDtypeStruct((rows, cols), BF16))
        operands.append(w)
        start += nb
    n_cast = len(operands)

    def wrapped(*refs):
        ins, rest = refs[:n_in], refs[n_in:]
        cast_in, rest = rest[:n_cast], rest[n_cast:]
        outs, rest = rest[:n_out], rest[n_out:]
        cast_out, scratch = rest[:n_cast], rest[n_cast:]
        kernel_fn(*ins, *outs, *scratch)
        for ci, co in zip(cast_in, cast_out):
            co[...] = ci[...].astype(BF16)

    def finish(cast_results):
        it = iter(cast_results)
        return [w[0].astype(BF16) if p is None else next(it)
                for w, p in zip(weights, plans)]

    return wrapped, in_specs, out_specs, out_shapes, operands, finish


def _proj_rot_kernel(x_ref, w_ref, cos_ref, sin_ref, o_ref, *, n_q_blocks):
    j = pl.program_id(1)
    acc = jnp.dot(x_ref[...], w_ref[...], preferred_element_type=F32)
    scale = jnp.where(j < n_q_blocks, RET_DK ** -0.5, 1.0).astype(F32)
    cos = cos_ref[...]
    sin = sin_ref[...]
    half = RET_DK // 2
    for h in range(o_ref.shape[1] // RET_DK):
        lo = h * RET_DK
        x1 = acc[:, lo:lo + half]
        x2 = acc[:, lo + half:lo + RET_DK]
        o_ref[:, lo:lo + half] = ((x1 * cos - x2 * sin) * scale).astype(BF16)
        o_ref[:, lo + half:lo + RET_DK] = ((x1 * sin + x2 * cos) * scale).astype(BF16)


def _proj_plain_kernel(x_ref, w_ref, o_ref):
    o_ref[...] = jnp.dot(x_ref[...], w_ref[...],
                         preferred_element_type=F32).astype(o_ref.dtype)


def _in_proj(hn, w_in, cos, sin, tp, later_weights_rot, later_weights_rest):
    m, d = hn.shape
    bm = _largest_divisor(tp, PROJ_BLOCK_ROWS, BF16_SUBLANES)
    bn = PROJ_BLOCK_COLS
    pos_blocks = tp // bm
    n_rot = 2 * RET_QK_WIDTH
    n_rest = w_in.shape[1] - n_rot

    grid = (m // bm, n_rot // bn)
    body, c_in, c_out, c_shape, c_ops, finish_rot = _with_casts(
        functools.partial(_proj_rot_kernel, n_q_blocks=RET_QK_WIDTH // bn),
        4, 1, later_weights_rot, grid[0] * grid[1], grid[1])
    qk, *cast_rot = pl.pallas_call(
        body,
        grid=grid,
        in_specs=[
            pl.BlockSpec((bm, d), lambda i, j: (i, 0)),
            pl.BlockSpec((d, bn), lambda i, j: (0, j)),
            pl.BlockSpec((bm, RET_DK // 2), lambda i, j: (i % pos_blocks, 0)),
            pl.BlockSpec((bm, RET_DK // 2), lambda i, j: (i % pos_blocks, 0)),
        ] + c_in,
        out_specs=[pl.BlockSpec((bm, bn), lambda i, j: (i, j))] + c_out,
        out_shape=[jax.ShapeDtypeStruct((m, n_rot), BF16)] + c_shape,
        compiler_params=_params("arbitrary", "arbitrary"),
        name="in_proj_rot",
    )(hn, w_in, cos, sin, *c_ops)

    rot_blocks = n_rot // bn
    grid = (m // bm, n_rest // bn)
    body, c_in, c_out, c_shape, c_ops, finish_rest = _with_casts(
        _proj_plain_kernel, 2, 1, later_weights_rest, grid[0] * grid[1], grid[1])
    rest, *cast_rest = pl.pallas_call(
        body,
        grid=grid,
        in_specs=[
            pl.BlockSpec((bm, d), lambda i, j: (i, 0)),
            pl.BlockSpec((d, bn), lambda i, j: (0, j + rot_blocks)),
        ] + c_in,
        out_specs=[pl.BlockSpec((bm, bn), lambda i, j: (i, j))] + c_out,
        out_shape=[jax.ShapeDtypeStruct((m, n_rest), BF16)] + c_shape,
        compiler_params=_params("arbitrary", "arbitrary"),
        name="in_proj_rest",
    )(hn, w_in, *c_ops)
    return qk, rest, finish_rot(cast_rot), finish_rest(cast_rest)


def _ret_log_gamma():
    return np.log(1.0 - 2.0 ** (-5.0 - np.arange(RET_HEADS, dtype=np.float64)))


def _ret_kernel(q_ref, k_ref, v_ref, g_ref, intra_ref, qd_ref, kd_ref, gn_ref,
                o_ref, state_ref):
    c = pl.program_id(1)

    @pl.when(c == 0)
    def _():
        state_ref[...] = jnp.zeros(state_ref.shape, F32)

    chunk_decay = np.exp(CHUNK * _ret_log_gamma()).astype(np.float32)
    for n in range(q_ref.shape[0]):
        for h in range(RET_HEADS):
            sl = slice(h * RET_DK, (h + 1) * RET_DK)
            q = q_ref[n, :, sl]
            k = k_ref[n, :, sl]
            v = v_ref[n, :, sl]
            s = lax.dot_general(q, k, _NT, preferred_element_type=F32) * intra_ref[h]
            intra = jnp.dot(s.astype(BF16), v, preferred_element_type=F32)
            st = state_ref[n, h]
            qd = qd_ref[h]
            kd = kd_ref[h]
            cross = jnp.dot(q, st.astype(BF16), preferred_element_type=F32)
            cross = cross * jnp.concatenate([qd, qd], axis=1)
            kdk = (k.astype(F32) * jnp.concatenate([kd, kd], axis=1)).astype(BF16)
            upd = lax.dot_general(kdk, v, _TN, preferred_element_type=F32)
            state_ref[n, h] = st * float(chunk_decay[h]) + upd
            y = intra + cross
            yn = _rms(y, gn_ref[:, sl])
            g = g_ref[n, :, sl].astype(F32)
            o_ref[n, :, sl] = (g * jax.nn.sigmoid(g) * yn).astype(BF16)


def _retention(qk, rest, gn_w, b, tp, later_weights):
    m = qk.shape[0]
    nblk = tp // CHUNK
    lg = jnp.log(1.0 - 2.0 ** (-5.0 - jnp.arange(RET_HEADS, dtype=F32)))
    idx = jnp.arange(CHUNK, dtype=F32)
    diff = idx[:, None] - idx[None, :]
    intra = jnp.where(diff[None] >= 0,
                      jnp.exp(jnp.maximum(diff, 0.0)[None] * lg[:, None, None]), 0.0)
    qd = jnp.exp((idx[None, :] + 1.0) * lg[:, None])
    kd = jnp.exp((CHUNK - 1.0 - idx[None, :]) * lg[:, None])
    qd = jnp.broadcast_to(qd[:, :, None], (RET_HEADS, CHUNK, CHUNK))
    kd = jnp.broadcast_to(kd[:, :, None], (RET_HEADS, CHUNK, CHUNK))
    table = pl.BlockSpec((RET_HEADS, CHUNK, CHUNK), lambda i, c: (0, 0, 0))
    grid = (b // ROW_BATCH, nblk)
    body, c_in, c_out, c_shape, c_ops, finish = _with_casts(
        _ret_kernel, 8, 1, later_weights, grid[0] * grid[1], nblk)

    def cols(block, width):
        return pl.BlockSpec((ROW_BATCH, CHUNK, width), lambda i, c: (i, c, block))

    qk = qk.reshape(b, tp, -1)
    rest = rest.reshape(b, tp, -1)
    out, *cast = pl.pallas_call(
        body,
        grid=grid,
        in_specs=[cols(0, RET_QK_WIDTH), cols(1, RET_QK_WIDTH),
                  cols(0, RET_WIDTH), cols(1, RET_WIDTH),
                  table, table, table,
                  pl.BlockSpec((1, RET_WIDTH), lambda i, c: (0, 0))] + c_in,
        out_specs=[cols(0, RET_WIDTH)] + c_out,
        out_shape=[jax.ShapeDtypeStruct((b, tp, RET_WIDTH), BF16)] + c_shape,
        scratch_shapes=[pltpu.VMEM((ROW_BATCH, RET_HEADS, RET_DK, RET_DV), F32)],
        compiler_params=_params("arbitrary", "arbitrary"),
        name="retention",
    )(qk, qk, rest, rest, intra, qd, kd, gn_w, *c_ops)
    return out.reshape(m, RET_WIDTH), finish(cast)


SB_GROUP = 8
SB_UNDERFLOW = 110.0
SB_FIRST_TILES = 3
LOG2E = 1.4426950408889634


def _sb_kernel(q_ref, k_ref, v_ref, tt_ref, nw_ref, o_ref, *, nblk):
    scale = LOG2E / math.sqrt(SB_DH)
    row = lax.broadcasted_iota(jnp.int32, (CHUNK, CHUNK), 0)
    col = lax.broadcasted_iota(jnp.int32, (CHUNK, CHUNK), 1)
    heads = [slice(g * SB_DH, (g + 1) * SB_DH) for g in range(SB_GROUP)]

    def step(qs, tiles, cs, accs):
        vs, masks, log_beta, split = [], [], [], []
        for j, mask in tiles:
            start = pl.multiple_of(jnp.maximum(j, 0) * CHUNK, CHUNK)
            k = k_ref[pl.ds(start, CHUNK), :]
            vs.append(v_ref[pl.ds(start, CHUNK), :])
            masks.append(mask)
            for g in range(SB_GROUP):
                zs = lax.dot_general(qs[g], k[:, heads[g]], _NT,
                                     preferred_element_type=F32) * scale
                sp = jnp.maximum(zs, 0.0) + jnp.log2(1.0 + jnp.exp2(-jnp.abs(zs)))
                log_beta.append(zs - sp)
                log_keep = jnp.where(mask, -sp, 0.0)
                hi = log_keep.astype(BF16)
                lo = (log_keep - hi.astype(F32)).astype(BF16)
                split.append(jnp.concatenate([hi, lo], axis=1))
        sums = jnp.dot(jnp.concatenate(split, axis=0), tt_ref[...],
                       preferred_element_type=F32)
        cs, accs = list(cs), list(accs)
        for t in range(len(tiles)):
            for g in range(SB_GROUP):
                n = t * SB_GROUP + g
                s = sums[n * CHUNK:(n + 1) * CHUNK]
                after = s[:, :CHUNK] + cs[g]
                w = jnp.where(masks[t], jnp.exp2(log_beta[n] + after), 0.0)
                accs[g] = accs[g] + jnp.dot(w.astype(BF16), vs[t][:, heads[g]],
                                            preferred_element_type=F32)
                cs[g] = cs[g] + s[:, CHUNK:]
        return tuple(cs), tuple(accs)

    def least_decayed(cs):
        return jnp.max(functools.reduce(jnp.maximum, cs))

    def valid_keys(j):
        return j * CHUNK + col >= N_PAD

    def load_q(i):
        start = pl.multiple_of(i * CHUNK, CHUNK)
        q = q_ref[pl.ds(start, CHUNK), :]
        return [q[:, heads[g]] for g in range(SB_GROUP)]

    def first_step(qs, i):
        zero = (jnp.zeros((CHUNK, CHUNK), F32),) * SB_GROUP
        first = [(i, jnp.logical_and(col < row, valid_keys(i)))]
        first += [(i - t, valid_keys(i - t)) for t in range(1, SB_FIRST_TILES)]
        return step(qs, first, zero, zero)

    def finish(qs, i, cs, accs):
        def cond(st):
            return jnp.logical_and(st[0] >= 0, st[1] > -SB_UNDERFLOW * LOG2E)

        def body(st):
            j = st[0]
            cs_, accs_ = step(qs, [(j, valid_keys(j))], st[2:2 + SB_GROUP],
                              st[2 + SB_GROUP:])
            return (j - 1, least_decayed(cs_)) + cs_ + accs_

        st = lax.while_loop(cond, body,
                            (i - SB_FIRST_TILES, least_decayed(cs)) + cs + accs)
        accs = st[2 + SB_GROUP:]
        start = pl.multiple_of(i * CHUNK, CHUNK)
        for g in range(SB_GROUP):
            o_ref[pl.ds(start, CHUNK), heads[g]] = _rms(
                accs[g], nw_ref[:, heads[g]]).astype(BF16)

    def q_block(i, carry):
        qs = load_q(i)
        finish(qs, i, *first_step(qs, i))
        return carry

    def q_block_pair(p, carry):
        i0 = 2 * p
        qs0, qs1 = load_q(i0), load_q(i0 + 1)
        first0 = first_step(qs0, i0)
        first1 = first_step(qs1, i0 + 1)
        finish(qs0, i0, *first0)
        finish(qs1, i0 + 1, *first1)
        return carry

    lax.fori_loop(0, nblk // 2, q_block_pair, 0)
    if nblk % 2:
        q_block(nblk - 1, 0)


def _stick_breaking(rest, nw, b, tp):
    m = rest.shape[0]
    nblk = tp // CHUNK
    width = SB_GROUP * SB_DH
    groups = SB_HEADS // SB_GROUP
    base = 2 * RET_WIDTH // width
    idx = jnp.arange(CHUNK)
    tri = (idx[:, None] > idx[None, :]).astype(BF16)
    tt = jnp.concatenate([tri, jnp.ones((CHUNK, CHUNK), BF16)], axis=1)
    tt = jnp.concatenate([tt, tt], axis=0)
    return pl.pallas_call(
        functools.partial(_sb_kernel, nblk=nblk),
        grid=(b, groups),
        in_specs=[
            pl.BlockSpec((tp, width), lambda i, h: (i, base + h),
                         pipeline_mode=pl.Buffered(1)),
            pl.BlockSpec((tp, width), lambda i, h: (i, base + groups + h),
                         pipeline_mode=pl.Buffered(1)),
            pl.BlockSpec((tp, width), lambda i, h: (i, base + 2 * groups + h),
                         pipeline_mode=pl.Buffered(1)),
            pl.BlockSpec((2 * CHUNK, 2 * CHUNK), lambda i, h: (0, 0)),
            pl.BlockSpec((1, width), lambda i, h: (0, h)),
        ],
        out_specs=pl.BlockSpec((tp, width), lambda i, h: (i, h)),
        out_shape=jax.ShapeDtypeStruct((m, SB_WIDTH), BF16),
        compiler_params=_params("parallel", "parallel"),
        name="stick_breaking",
    )(rest, rest, rest, tt, nw)


def _out_proj_kernel(r_ref, s_ref, w1_ref, w2_ref, o_ref):
    o_ref[...] = (jnp.dot(r_ref[...], w1_ref[...], preferred_element_type=F32)
                  + jnp.dot(s_ref[...], w2_ref[...], preferred_element_type=F32)
                  ).astype(o_ref.dtype)


def _out_proj(ret_out, sb_out, w_out, tp):
    m = ret_out.shape[0]
    d = w_out.shape[1]
    bm = _largest_divisor(tp, PROJ_BLOCK_ROWS, BF16_SUBLANES)
    bn = PROJ_BLOCK_COLS
    return pl.pallas_call(
        _out_proj_kernel,
        grid=(m // bm, d // bn),
        in_specs=[
            pl.BlockSpec((bm, RET_WIDTH), lambda i, j: (i, 0)),
            pl.BlockSpec((bm, SB_WIDTH), lambda i, j: (i, 0)),
            pl.BlockSpec((RET_WIDTH, bn), lambda i, j: (0, j)),
            pl.BlockSpec((SB_WIDTH, bn), lambda i, j: (1, j)),
        ],
        out_specs=pl.BlockSpec((bm, bn), lambda i, j: (i, j)),
        out_shape=jax.ShapeDtypeStruct((m, d), BF16),
        compiler_params=_params("parallel", "arbitrary"),
        name="out_proj",
    )(ret_out, sb_out, w_out, w_out)


def _post_attn_kernel(a_ref, x_ref, meta_ref, w1_ref, w2_ref, hn_ref):
    h = _padded_rows(x_ref, meta_ref, pl.program_id(1))
    h1 = h + _rms(a_ref[...].astype(F32), w1_ref[...])
    hn_ref[...] = _rms(h1, w2_ref[...]).astype(BF16)


def _post_attn(a, x, meta, w_post, w_pre):
    b, tp, d = a.shape
    vec = pl.BlockSpec((1, d), lambda i, r: (0, 0))
    return pl.pallas_call(
        _post_attn_kernel,
        grid=(b // ROW_BATCH, tp // CHUNK),
        in_specs=[_row_spec(d), _row_spec(d, -1),
                  pl.BlockSpec((N_META, d), lambda i, r: (0, 0)), vec, vec],
        out_specs=_row_spec(d),
        out_shape=jax.ShapeDtypeStruct((b, tp, d), BF16),
        compiler_params=_params("parallel", "arbitrary"),
        name="post_attn",
    )(a, x, meta, w_post, w_pre)


FFN_SUB_ROWS = 352


def _ffn_up_kernel(x_ref, xh_ref, wg_ref, wu_ref, cw_ref, cb_ref, o_ref, g_ref):
    bm = x_ref.shape[0]
    sub = FFN_SUB_ROWS if bm % FFN_SUB_ROWS == 0 else bm
    wg = wg_ref[...]
    wu = wu_ref[...]
    g_ref[:HALO, :] = jnp.dot(xh_ref[...], wg, preferred_element_type=F32)
    for r in range(bm // sub):
        x = x_ref[r * sub:(r + 1) * sub, :]
        lo = HALO + r * sub
        g_ref[lo:lo + sub, :] = jnp.dot(x, wg, preferred_element_type=F32)
        up = jnp.dot(x, wu, preferred_element_type=F32)
        ext = g_ref[lo - 8:lo + sub, :]
        prev1 = pltpu.roll(ext, 1, axis=0)
        prev2 = pltpu.roll(prev1, 1, axis=0)
        conv = cb_ref[...] + prev2[8:] * cw_ref[0:1, :]
        conv = conv + prev1[8:] * cw_ref[1:2, :]
        conv = conv + ext[8:] * cw_ref[2:3, :]
        o_ref[r * sub:(r + 1) * sub, :] = (conv * jax.nn.sigmoid(conv) * up).astype(BF16)


def _ffn_up(hn, w_up, conv_w, conv_b, tp):
    m, d = hn.shape
    dff = w_up.shape[1] // 2
    bm = tp
    bn = FFN_UP_BLOCK_COLS
    halo_blocks = bm // HALO
    up_base = dff // bn
    return pl.pallas_call(
        _ffn_up_kernel,
        grid=(m // bm, dff // bn),
        in_specs=[
            pl.BlockSpec((bm, d), lambda i, j: (i, 0), pipeline_mode=pl.Buffered(1)),
            pl.BlockSpec((HALO, d), lambda i, j: (jnp.maximum(i * halo_blocks - 1, 0), 0)),
            pl.BlockSpec((d, bn), lambda i, j: (0, j)),
            pl.BlockSpec((d, bn), lambda i, j: (0, j + up_base)),
            pl.BlockSpec((CONV_W, bn), lambda i, j: (0, j)),
            pl.BlockSpec((1, bn), lambda i, j: (0, j)),
        ],
        out_specs=pl.BlockSpec((bm, bn), lambda i, j: (i, j)),
        out_shape=jax.ShapeDtypeStruct((m, dff), BF16),
        scratch_shapes=[pltpu.VMEM((bm + HALO, bn), F32)],
        compiler_params=_params("parallel", "arbitrary"),
        name="ffn_up",
    )(hn, hn, w_up, w_up, conv_w, conv_b)


def _ffn_down(act, w_down, tp):
    m, dff = act.shape
    d = w_down.shape[1]
    seq = tp - CHUNK
    bm = _largest_divisor(seq, FFN_DOWN_BLOCK_ROWS, BF16_SUBLANES)
    bn = FFN_DOWN_BLOCK_COLS
    per_batch = seq // bm

    def token_rows(i, j):
        start = (i // per_batch) * tp + CHUNK + (i % per_batch) * bm
        return pl.multiple_of(start, BF16_SUBLANES), 0

    return pl.pallas_call(
        _proj_plain_kernel,
        grid=(m // tp * per_batch, d // bn),
        in_specs=[
            pl.BlockSpec((pl.Element(bm), pl.Element(dff)), token_rows),
            pl.BlockSpec((dff, bn), lambda i, j: (0, j)),
        ],
        out_specs=pl.BlockSpec((bm, bn), lambda i, j: (i, j)),
        out_shape=jax.ShapeDtypeStruct((m // tp * seq, d), BF16),
        compiler_params=_params("parallel", "arbitrary"),
        name="ffn_down",
    )(act, w_down)


def _final_kernel(f_ref, a_ref, x_ref, w1_ref, w2_ref, o_ref):
    h1 = x_ref[...] + _rms(a_ref[...].astype(F32), w1_ref[...])
    o_ref[...] = h1 + _rms(f_ref[...].astype(F32), w2_ref[...])


def _final(f, a, x, w_post, w_final):
    b, seq, d = x.shape
    vec = pl.BlockSpec((1, d), lambda i, r: (0, 0))
    return pl.pallas_call(
        _final_kernel,
        grid=(b // ROW_BATCH, seq // CHUNK),
        in_specs=[_row_spec(d), _row_spec(d, 1), _row_spec(d), vec, vec],
        out_specs=_row_spec(d),
        out_shape=jax.ShapeDtypeStruct((b, seq, d), F32),
        compiler_params=_params("parallel", "parallel"),
        name="final",
    )(f, a, x, w_post, w_final)


def _rotary_tables(tp):
    half = RET_DK // 2
    pos = (jnp.arange(tp) - N_PAD).astype(F32)
    inv = ROPE_BASE ** (-jnp.arange(half, dtype=F32) / half)
    ang = pos[:, None] * inv[None, :]
    return jnp.cos(ang), jnp.sin(ang)


def kernel(x, meta_tokens, attn_pre_norm_w, w_in, ret_gn_w, sb_norm_w, w_out,
           attn_post_norm_w, ffn_pre_norm_w, w_up, conv_w, conv_b, w_down, ffn_post_norm_w):
    b, seq, d = x.shape
    assert seq % CHUNK == 0 and w_in.shape[0] == 1 and b % ROW_BATCH == 0
    tp = seq + CHUNK
    m = b * tp

    hn = _prep(x, meta_tokens, attn_pre_norm_w).reshape(m, d)

    cos, sin = _rotary_tables(tp)
    qk, rest, (w_down_bf,), (w_up_bf,) = _in_proj(
        hn, w_in[0].astype(BF16), cos, sin, tp, [w_down], [w_up])
    ret_out, (w_out_bf,) = _retention(qk, rest, ret_gn_w, b, tp, [w_out])
    sb_out = _stick_breaking(rest, sb_norm_w, b, tp)
    a = _out_proj(ret_out, sb_out, w_out_bf, tp)
    a = a.reshape(b, tp, d)
    hn2 = _post_attn(a, x, meta_tokens, attn_post_norm_w, ffn_pre_norm_w)

    act = _ffn_up(hn2.reshape(m, d), w_up_bf, conv_w[0], conv_b, tp)
    f = _ffn_down(act, w_down_bf, tp)
    return _final(f.reshape(b, seq, d), a, x, attn_post_norm_w, ffn_post_norm_w)
```

```python
import functools
import math

import numpy as np
import jax
import jax.numpy as jnp
from jax import lax
from jax.experimental import pallas as pl
from jax.experimental.pallas import tpu as pltpu

N_META = 16
CHUNK = 128
N_PAD = CHUNK - N_META
RET_HEADS = 8
RET_DK = 256
RET_DV = 256
SB_HEADS = 16
SB_DH = 128
RET_WIDTH = RET_HEADS * RET_DV
RET_QK_WIDTH = RET_HEADS * RET_DK
SB_WIDTH = SB_HEADS * SB_DH
CONV_W = 3
EPS = 1e-6
ROPE_BASE = 10000.0

BF16_SUBLANES = 16
HALO = BF16_SUBLANES
assert CONV_W - 1 <= HALO <= N_META
VMEM_LIMIT_BYTES = 58 * 1024 * 1024

PROJ_BLOCK_ROWS = 1056
PROJ_BLOCK_COLS = 1024
FFN_UP_BLOCK_COLS = 256
FFN_DOWN_BLOCK_ROWS = 528
FFN_DOWN_BLOCK_COLS = 512

F32 = jnp.float32
BF16 = jnp.bfloat16

_NT = (((1,), (1,)), ((), ()))
_TN = (((0,), (0,)), ((), ()))


def _params(*semantics):
    return pltpu.CompilerParams(dimension_semantics=semantics,
                                vmem_limit_bytes=VMEM_LIMIT_BYTES)


def _largest_divisor(n, target, multiple):
    best = None
    for d in range(multiple, min(n, target) + 1, multiple):
        if n % d == 0:
            best = d
    assert best is not None, (n, target, multiple)
    return best


def _rms(v, w):
    ms = jnp.mean(v * v, axis=-1, keepdims=True)
    return v * lax.rsqrt(ms + EPS) * w


ROW_BATCH = 2


def _padded_rows(x_ref, meta_ref, r):
    d = x_ref.shape[-1]
    first = jnp.concatenate([jnp.zeros((N_PAD, d), F32), meta_ref[...]], axis=0)
    return jnp.where(r == 0, first[None], x_ref[...])


def _row_spec(d, shift=0):
    return pl.BlockSpec((ROW_BATCH, CHUNK, d),
                        lambda i, r: (i, jnp.maximum(r + shift, 0), 0))


def _prep_kernel(x_ref, meta_ref, w_ref, hn_ref):
    h = _padded_rows(x_ref, meta_ref, pl.program_id(1))
    hn_ref[...] = _rms(h, w_ref[...]).astype(BF16)


def _prep(x, meta, w):
    b, seq, d = x.shape
    nblk = seq // CHUNK + 1
    tp = nblk * CHUNK
    return pl.pallas_call(
        _prep_kernel,
        grid=(b // ROW_BATCH, nblk),
        in_specs=[
            _row_spec(d, -1),
            pl.BlockSpec((N_META, d), lambda i, r: (0, 0)),
            pl.BlockSpec((1, d), lambda i, r: (0, 0)),
        ],
        out_specs=_row_spec(d),
        out_shape=jax.ShapeDtypeStruct((b, tp, d), BF16),
        compiler_params=_params("parallel", "arbitrary"),
        name="prep",
    )(x, meta, w)


CAST_BLOCK_BYTES = 3 * 1024 * 1024


def _cast_plan(rows, cols, steps):
    best = None
    for br in range(BF16_SUBLANES, rows + 1, BF16_SUBLANES):
        if rows % br:
            continue
        for bc in range(128, cols + 1, 128):
            if cols % bc or br * bc * 4 > CAST_BLOCK_BYTES:
                continue
            if (rows // br) * (cols // bc) <= steps and (best is None or br * bc < best[0] * best[1]):
                best = (br, bc)
    return best


def _with_casts(kernel_fn, n_in, n_out, weights, steps, nj):
    plans, in_specs, out_specs, out_shapes, operands = [], [], [], [], []
    start = 0
    for w in weights:
        _, rows, cols = w.shape
        plan = _cast_plan(rows, cols, steps - start)
        plans.append(plan)
        if plan is None:
            continue
        br, bc = plan
        ncb = cols // bc
        nb = (rows // br) * ncb

        def block(i, j, start=start, nb=nb, ncb=ncb):
            t = jnp.clip(i * nj + j - start, 0, nb - 1)
            return t // ncb, t % ncb

        in_specs.append(pl.BlockSpec((None, br, bc), lambda i, j, block=block: (0,) + block(i, j)))
        out_specs.append(pl.BlockSpec((br, bc), block))
        out_shapes.append(jax.ShapeDtypeStruct((rows, cols), BF16))
        operands.append(w)
        start += nb
    n_cast = len(operands)

    def wrapped(*refs):
        ins, rest = refs[:n_in], refs[n_in:]
        cast_in, rest = rest[:n_cast], rest[n_cast:]
        outs, rest = rest[:n_out], rest[n_out:]
        cast_out, scratch = rest[:n_cast], rest[n_cast:]
        kernel_fn(*ins, *outs, *scratch)
        for ci, co in zip(cast_in, cast_out):
            co[...] = ci[...].astype(BF16)

    def finish(cast_results):
        it = iter(cast_results)
        return [w[0].astype(BF16) if p is None else next(it)
                for w, p in zip(weights, plans)]

    return wrapped, in_specs, out_specs, out_shapes, operands, finish


def _proj_rot_kernel(x_ref, w_ref, cos_ref, sin_ref, o_ref, *, n_q_blocks):
    j = pl.program_id(1)
    acc = jnp.dot(x_ref[...], w_ref[...], preferred_element_type=F32)
    scale = jnp.where(j < n_q_blocks, RET_DK ** -0.5, 1.0).astype(F32)
    cos = cos_ref[...]
    sin = sin_ref[...]
    half = RET_DK // 2
    for h in range(o_ref.shape[1] // RET_DK):
        lo = h * RET_DK
        x1 = acc[:, lo:lo + half]
        x2 = acc[:, lo + half:lo + RET_DK]
        o_ref[:, lo:lo + half] = ((x1 * cos - x2 * sin) * scale).astype(BF16)
        o_ref[:, lo + half:lo + RET_DK] = ((x1 * sin + x2 * cos) * scale).astype(BF16)


def _proj_plain_kernel(x_ref, w_ref, o_ref):
    o_ref[...] = jnp.dot(x_ref[...], w_ref[...],
                         preferred_element_type=F32).astype(o_ref.dtype)


def _in_proj(hn, w_in, cos, sin, tp, later_weights_rot, later_weights_rest):
    m, d = hn.shape
    bm = _largest_divisor(tp, PROJ_BLOCK_ROWS, BF16_SUBLANES)
    bn = PROJ_BLOCK_COLS
    pos_blocks = tp // bm
    n_rot = 2 * RET_QK_WIDTH
    n_rest = w_in.shape[1] - n_rot

    grid = (m // bm, n_rot // bn)
    body, c_in, c_out, c_shape, c_ops, finish_rot = _with_casts(
        functools.partial(_proj_rot_kernel, n_q_blocks=RET_QK_WIDTH // bn),
        4, 1, later_weights_rot, grid[0] * grid[1], grid[1])
    qk, *cast_rot = pl.pallas_call(
        body,
        grid=grid,
        in_specs=[
            pl.BlockSpec((bm, d), lambda i, j: (i, 0)),
            pl.BlockSpec((d, bn), lambda i, j: (0, j)),
            pl.BlockSpec((bm, RET_DK // 2), lambda i, j: (i % pos_blocks, 0)),
            pl.BlockSpec((bm, RET_DK // 2), lambda i, j: (i % pos_blocks, 0)),
        ] + c_in,
        out_specs=[pl.BlockSpec((bm, bn), lambda i, j: (i, j))] + c_out,
        out_shape=[jax.ShapeDtypeStruct((m, n_rot), BF16)] + c_shape,
        compiler_params=_params("arbitrary", "arbitrary"),
        name="in_proj_rot",
    )(hn, w_in, cos, sin, *c_ops)

    rot_blocks = n_rot // bn
    grid = (m // bm, n_rest // bn)
    body, c_in, c_out, c_shape, c_ops, finish_rest = _with_casts(
        _proj_plain_kernel, 2, 1, later_weights_rest, grid[0] * grid[1], grid[1])
    rest, *cast_rest = pl.pallas_call(
        body,
        grid=grid,
        in_specs=[
            pl.BlockSpec((bm, d), lambda i, j: (i, 0)),
            pl.BlockSpec((d, bn), lambda i, j: (0, j + rot_blocks)),
        ] + c_in,
        out_specs=[pl.BlockSpec((bm, bn), lambda i, j: (i, j))] + c_out,
        out_shape=[jax.ShapeDtypeStruct((m, n_rest), BF16)] + c_shape,
        compiler_params=_params("arbitrary", "arbitrary"),
        name="in_proj_rest",
    )(hn, w_in, *c_ops)
    return qk, rest, finish_rot(cast_rot), finish_rest(cast_rest)


def _ret_log_gamma():
    return np.log(1.0 - 2.0 ** (-5.0 - np.arange(RET_HEADS, dtype=np.float64)))


def _ret_kernel(q_ref, k_ref, v_ref, g_ref, intra_ref, qd_ref, kd_ref, gn_ref,
                o_ref, state_ref):
    c = pl.program_id(1)

    @pl.when(c == 0)
    def _():
        state_ref[...] = jnp.zeros(state_ref.shape, F32)

    chunk_decay = np.exp(CHUNK * _ret_log_gamma()).astype(np.float32)
    for n in range(q_ref.shape[0]):
        for h in range(RET_HEADS):
            sl = slice(h * RET_DK, (h + 1) * RET_DK)
            q = q_ref[n, :, sl]
            k = k_ref[n, :, sl]
            v = v_ref[n, :, sl]
            s = lax.dot_general(q, k, _NT, preferred_element_type=F32) * intra_ref[h]
            intra = jnp.dot(s.astype(BF16), v, preferred_element_type=F32)
            st = state_ref[n, h]
            qd = qd_ref[h]
            kd = kd_ref[h]
            cross = jnp.dot(q, st.astype(BF16), preferred_element_type=F32)
            cross = cross * jnp.concatenate([qd, qd], axis=1)
            kdk = (k.astype(F32) * jnp.concatenate([kd, kd], axis=1)).astype(BF16)
            upd = lax.dot_general(kdk, v, _TN, preferred_element_type=F32)
            state_ref[n, h] = st * float(chunk_decay[h]) + upd
            y = intra + cross
            yn = _rms(y, gn_ref[:, sl])
            g = g_ref[n, :, sl].astype(F32)
            o_ref[n, :, sl] = (g * jax.nn.sigmoid(g) * yn).astype(BF16)


def _retention(qk, rest, gn_w, b, tp, later_weights):
    m = qk.shape[0]
    nblk = tp // CHUNK
    lg = jnp.log(1.0 - 2.0 ** (-5.0 - jnp.arange(RET_HEADS, dtype=F32)))
    idx = jnp.arange(CHUNK, dtype=F32)
    diff = idx[:, None] - idx[None, :]
    intra = jnp.where(diff[None] >= 0,
                      jnp.exp(jnp.maximum(diff, 0.0)[None] * lg[:, None, None]), 0.0)
    qd = jnp.exp((idx[None, :] + 1.0) * lg[:, None])
    kd = jnp.exp((CHUNK - 1.0 - idx[None, :]) * lg[:, None])
    qd = jnp.broadcast_to(qd[:, :, None], (RET_HEADS, CHUNK, CHUNK))
    kd = jnp.broadcast_to(kd[:, :, None], (RET_HEADS, CHUNK, CHUNK))
    table = pl.BlockSpec((RET_HEADS, CHUNK, CHUNK), lambda i, c: (0, 0, 0))
    grid = (b // ROW_BATCH, nblk)
    body, c_in, c_out, c_shape, c_ops, finish = _with_casts(
        _ret_kernel, 8, 1, later_weights, grid[0] * grid[1], nblk)

    def cols(block, width):
        return pl.BlockSpec((ROW_BATCH, CHUNK, width), lambda i, c: (i, c, block))

    qk = qk.reshape(b, tp, -1)
    rest = rest.reshape(b, tp, -1)
    out, *cast = pl.pallas_call(
        body,
        grid=grid,
        in_specs=[cols(0, RET_QK_WIDTH), cols(1, RET_QK_WIDTH),
                  cols(0, RET_WIDTH), cols(1, RET_WIDTH),
                  table, table, table,
                  pl.BlockSpec((1, RET_WIDTH), lambda i, c: (0, 0))] + c_in,
        out_specs=[cols(0, RET_WIDTH)] + c_out,
        out_shape=[jax.ShapeDtypeStruct((b, tp, RET_WIDTH), BF16)] + c_shape,
        scratch_shapes=[pltpu.VMEM((ROW_BATCH, RET_HEADS, RET_DK, RET_DV), F32)],
        compiler_params=_params("arbitrary", "arbitrary"),
        name="retention",
    )(qk, qk, rest, rest, intra, qd, kd, gn_w, *c_ops)
    return out.reshape(m, RET_WIDTH), finish(cast)


SB_GROUP = 8
SB_UNDERFLOW = 110.0
SB_FIRST_TILES = 3
LOG2E = 1.4426950408889634


def _sb_kernel(q_ref, k_ref, v_ref, tt_ref, nw_ref, o_ref, *, nblk):
    scale = LOG2E / math.sqrt(SB_DH)
    row = lax.broadcasted_iota(jnp.int32, (CHUNK, CHUNK), 0)
    col = lax.broadcasted_iota(jnp.int32, (CHUNK, CHUNK), 1)
    heads = [slice(g * SB_DH, (g + 1) * SB_DH) for g in range(SB_GROUP)]

    def step(qs, tiles, cs, accs):
        starts, masks, log_beta, split = [], [], [], []
        for j, mask in tiles:
            start = pl.multiple_of(jnp.maximum(j, 0) * CHUNK, CHUNK)
            starts.append(start)
            masks.append(mask)
            for g in range(SB_GROUP):
                zs = lax.dot_general(qs[g], k_ref[pl.ds(start, CHUNK), heads[g]], _NT,
                                     preferred_element_type=F32) * scale
                sp = jnp.maximum(zs, 0.0) + jnp.log2(1.0 + jnp.exp2(-jnp.abs(zs)))
                log_beta.append(zs - sp)
                log_keep = jnp.where(mask, -sp, 0.0)
                hi = log_keep.astype(BF16)
                lo = (log_keep - hi.astype(F32)).astype(BF16)
                split.append(jnp.concatenate([hi, lo], axis=1))
        sums = jnp.dot(jnp.concatenate(split, axis=0), tt_ref[...],
                       preferred_element_type=F32)
        cs, accs = list(cs), list(accs)
        for t in range(len(tiles)):
            for g in range(SB_GROUP):
                n = t * SB_GROUP + g
                s = sums[n * CHUNK:(n + 1) * CHUNK]
                after = s[:, :CHUNK] + cs[g]
                w = jnp.where(masks[t], jnp.exp2(log_beta[n] + after), 0.0)
                accs[g] = accs[g] + jnp.dot(
                    w.astype(BF16), v_ref[pl.ds(starts[t], CHUNK), heads[g]],
                    preferred_element_type=F32)
                cs[g] = cs[g] + s[:, CHUNK:]
        return tuple(cs), tuple(accs)

    def least_decayed(cs):
        return jnp.max(functools.reduce(jnp.maximum, cs))

    def valid_keys(j):
        return j * CHUNK + col >= N_PAD

    def load_q(i):
        start = pl.multiple_of(i * CHUNK, CHUNK)
        q = q_ref[pl.ds(start, CHUNK), :]
        return [q[:, heads[g]] for g in range(SB_GROUP)]

    def first_step(qs, i):
        zero = (jnp.zeros((CHUNK, CHUNK), F32),) * SB_GROUP
        first = [(i, jnp.logical_and(col < row, valid_keys(i)))]
        first += [(i - t, valid_keys(i - t)) for t in range(1, SB_FIRST_TILES)]
        return step(qs, first, zero, zero)

    def finish(qs, i, cs, accs):
        def cond(st):
            return jnp.logical_and(st[0] >= 0, st[1] > -SB_UNDERFLOW * LOG2E)

        def body(st):
            j = st[0]
            cs_, accs_ = step(qs, [(j, valid_keys(j))], st[2:2 + SB_GROUP],
                              st[2 + SB_GROUP:])
            return (j - 1, least_decayed(cs_)) + cs_ + accs_

        st = lax.while_loop(cond, body,
                            (i - SB_FIRST_TILES, least_decayed(cs)) + cs + accs)
        accs = st[2 + SB_GROUP:]
        start = pl.multiple_of(i * CHUNK, CHUNK)
        for g in range(SB_GROUP):
            o_ref[pl.ds(start, CHUNK), heads[g]] = _rms(
                accs[g], nw_ref[:, heads[g]]).astype(BF16)

    def q_block(i, carry):
        qs = load_q(i)
        finish(qs, i, *first_step(qs, i))
        return carry

    def q_block_pair(p, carry):
        i0 = 2 * p
        qs0, qs1 = load_q(i0), load_q(i0 + 1)
        first0 = first_step(qs0, i0)
        first1 = first_step(qs1, i0 + 1)
        finish(qs0, i0, *first0)
        finish(qs1, i0 + 1, *first1)
        return carry

    lax.fori_loop(0, nblk // 2, q_block_pair, 0)
    if nblk % 2:
        q_block(nblk - 1, 0)


def _stick_breaking(rest, nw, b, tp):
    m = rest.shape[0]
    nblk = tp // CHUNK
    width = SB_GROUP * SB_DH
    groups = SB_HEADS // SB_GROUP
    base = 2 * RET_WIDTH // width
    idx = jnp.arange(CHUNK)
    tri = (idx[:, None] > idx[None, :]).astype(BF16)
    tt = jnp.concatenate([tri, jnp.ones((CHUNK, CHUNK), BF16)], axis=1)
    tt = jnp.concatenate([tt, tt], axis=0)
    return pl.pallas_call(
        functools.partial(_sb_kernel, nblk=nblk),
        grid=(b, groups),
        in_specs=[
            pl.BlockSpec((tp, width), lambda i, h: (i, base + h),
                         pipeline_mode=pl.Buffered(1)),
            pl.BlockSpec((tp, width), lambda i, h: (i, base + groups + h),
                         pipeline_mode=pl.Buffered(1)),
            pl.BlockSpec((tp, width), lambda i, h: (i, base + 2 * groups + h),
                         pipeline_mode=pl.Buffered(1)),
            pl.BlockSpec((2 * CHUNK, 2 * CHUNK), lambda i, h: (0, 0)),
            pl.BlockSpec((1, width), lambda i, h: (0, h)),
        ],
        out_specs=pl.BlockSpec((tp, width), lambda i, h: (i, h)),
        out_shape=jax.ShapeDtypeStruct((m, SB_WIDTH), BF16),
        compiler_params=_params("parallel", "parallel"),
        name="stick_breaking",
    )(rest, rest, rest, tt, nw)


def _out_proj_kernel(r_ref, s_ref, w1_ref, w2_ref, o_ref):
    o_ref[...] = (jnp.dot(r_ref[...], w1_ref[...], preferred_element_type=F32)
                  + jnp.dot(s_ref[...], w2_ref[...], preferred_element_type=F32)
                  ).astype(o_ref.dtype)


def _out_proj(ret_out, sb_out, w_out, tp):
    m = ret_out.shape[0]
    d = w_out.shape[1]
    bm = _largest_divisor(tp, PROJ_BLOCK_ROWS, BF16_SUBLANES)
    bn = PROJ_BLOCK_COLS
    return pl.pallas_call(
        _out_proj_kernel,
        grid=(m // bm, d // bn),
        in_specs=[
            pl.BlockSpec((bm, RET_WIDTH), lambda i, j: (i, 0)),
            pl.BlockSpec((bm, SB_WIDTH), lambda i, j: (i, 0)),
            pl.BlockSpec((RET_WIDTH, bn), lambda i, j: (0, j)),
            pl.BlockSpec((SB_WIDTH, bn), lambda i, j: (1, j)),
        ],
        out_specs=pl.BlockSpec((bm, bn), lambda i, j: (i, j)),
        out_shape=jax.ShapeDtypeStruct((m, d), BF16),
        compiler_params=_params("parallel", "arbitrary"),
        name="out_proj",
    )(ret_out, sb_out, w_out, w_out)


def _post_attn_kernel(a_ref, x_ref, meta_ref, w1_ref, w2_ref, hn_ref):
    h = _padded_rows(x_ref, meta_ref, pl.program_id(1))
    h1 = h + _rms(a_ref[...].astype(F32), w1_ref[...])
    hn_ref[...] = _rms(h1, w2_ref[...]).astype(BF16)


def _post_attn(a, x, meta, w_post, w_pre):
    b, tp, d = a.shape
    vec = pl.BlockSpec((1, d), lambda i, r: (0, 0))
    return pl.pallas_call(
        _post_attn_kernel,
        grid=(b // ROW_BATCH, tp // CHUNK),
        in_specs=[_row_spec(d), _row_spec(d, -1),
                  pl.BlockSpec((N_META, d), lambda i, r: (0, 0)), vec, vec],
        out_specs=_row_spec(d),
        out_shape=jax.ShapeDtypeStruct((b, tp, d), BF16),
        compiler_params=_params("parallel", "arbitrary"),
        name="post_attn",
    )(a, x, meta, w_post, w_pre)


FFN_SUB_ROWS = 352


def _ffn_up_kernel(x_ref, xh_ref, wg_ref, wu_ref, cw_ref, cb_ref, o_ref, g_ref):
    bm = x_ref.shape[0]
    sub = FFN_SUB_ROWS if bm % FFN_SUB_ROWS == 0 else bm
    wg = wg_ref[...]
    wu = wu_ref[...]
    g_ref[:HALO, :] = jnp.dot(xh_ref[...], wg, preferred_element_type=F32)
    for r in range(bm // sub):
        x = x_ref[r * sub:(r + 1) * sub, :]
        lo = HALO + r * sub
        g_ref[lo:lo + sub, :] = jnp.dot(x, wg, preferred_element_type=F32)
        up = jnp.dot(x, wu, preferred_element_type=F32)
        ext = g_ref[lo - 8:lo + sub, :]
        prev1 = pltpu.roll(ext, 1, axis=0)
        prev2 = pltpu.roll(prev1, 1, axis=0)
        conv = cb_ref[...] + prev2[8:] * cw_ref[0:1, :]
        conv = conv + prev1[8:] * cw_ref[1:2, :]
        conv = conv + ext[8:] * cw_ref[2:3, :]
        o_ref[r * sub:(r + 1) * sub, :] = (conv * jax.nn.sigmoid(conv) * up).astype(BF16)


def _ffn_up(hn, w_up, conv_w, conv_b, tp):
    m, d = hn.shape
    dff = w_up.shape[1] // 2
    bm = tp
    bn = FFN_UP_BLOCK_COLS
    halo_blocks = bm // HALO
    up_base = dff // bn
    return pl.pallas_call(
        _ffn_up_kernel,
        grid=(m // bm, dff // bn),
        in_specs=[
            pl.BlockSpec((bm, d), lambda i, j: (i, 0), pipeline_mode=pl.Buffered(1)),
            pl.BlockSpec((HALO, d), lambda i, j: (jnp.maximum(i * halo_blocks - 1, 0), 0)),
            pl.BlockSpec((d, bn), lambda i, j: (0, j)),
            pl.BlockSpec((d, bn), lambda i, j: (0, j + up_base)),
            pl.BlockSpec((CONV_W, bn), lambda i, j: (0, j)),
            pl.BlockSpec((1, bn), lambda i, j: (0, j)),
        ],
        out_specs=pl.BlockSpec((bm, bn), lambda i, j: (i, j)),
        out_shape=jax.ShapeDtypeStruct((m, dff), BF16),
        scratch_shapes=[pltpu.VMEM((bm + HALO, bn), F32)],
        compiler_params=_params("parallel", "arbitrary"),
        name="ffn_up",
    )(hn, hn, w_up, w_up, conv_w, conv_b)


def _ffn_down(act, w_down, tp):
    m, dff = act.shape
    d = w_down.shape[1]
    seq = tp - CHUNK
    bm = _largest_divisor(seq, FFN_DOWN_BLOCK_ROWS, BF16_SUBLANES)
    bn = FFN_DOWN_BLOCK_COLS
    per_batch = seq // bm

    def token_rows(i, j):
        start = (i // per_batch) * tp + CHUNK + (i % per_batch) * bm
        return pl.multiple_of(start, BF16_SUBLANES), 0

    return pl.pallas_call(
        _proj_plain_kernel,
        grid=(m // tp * per_batch, d // bn),
        in_specs=[
            pl.BlockSpec((pl.Element(bm), pl.Element(dff)), token_rows),
            pl.BlockSpec((dff, bn), lambda i, j: (0, j)),
        ],
        out_specs=pl.BlockSpec((bm, bn), lambda i, j: (i, j)),
        out_shape=jax.ShapeDtypeStruct((m // tp * seq, d), BF16),
        compiler_params=_params("parallel", "arbitrary"),
        name="ffn_down",
    )(act, w_down)


def _final_kernel(f_ref, a_ref, x_ref, w1_ref, w2_ref, o_ref):
    h1 = x_ref[...] + _rms(a_ref[...].astype(F32), w1_ref[...])
    o_ref[...] = h1 + _rms(f_ref[...].astype(F32), w2_ref[...])


def _final(f, a, x, w_post, w_final):
    b, seq, d = x.shape
    vec = pl.BlockSpec((1, d), lambda i, r: (0, 0))
    return pl.pallas_call(
        _final_kernel,
        grid=(b // ROW_BATCH, seq // CHUNK),
        in_specs=[_row_spec(d), _row_spec(d, 1), _row_spec(d), vec, vec],
        out_specs=_row_spec(d),
        out_shape=jax.ShapeDtypeStruct((b, seq, d), F32),
        compiler_params=_params("parallel", "parallel"),
        name="final",
    )(f, a, x, w_post, w_final)


def _rotary_tables(tp):
    half = RET_DK // 2
    pos = (jnp.arange(tp) - N_PAD).astype(F32)
    inv = ROPE_BASE ** (-jnp.arange(half, dtype=F32) / half)
    ang = pos[:, None] * inv[None, :]
    return jnp.cos(ang), jnp.sin(ang)


def kernel(x, meta_tokens, attn_pre_norm_w, w_in, ret_gn_w, sb_norm_w, w_out,
           attn_post_norm_w, ffn_pre_norm_w, w_up, conv_w, conv_b, w_down, ffn_post_norm_w):
    b, seq, d = x.shape
    assert seq % CHUNK == 0 and w_in.shape[0] == 1 and b % ROW_BATCH == 0
    tp = seq + CHUNK
    m = b * tp

    hn = _prep(x, meta_tokens, attn_pre_norm_w).reshape(m, d)

    cos, sin = _rotary_tables(tp)
    qk, rest, (w_down_bf,), (w_up_bf,) = _in_proj(
        hn, w_in[0].astype(BF16), cos, sin, tp, [w_down], [w_up])
    ret_out, (w_out_bf,) = _retention(qk, rest, ret_gn_w, b, tp, [w_out])
    sb_out = _stick_breaking(rest, sb_norm_w, b, tp)
    a = _out_proj(ret_out, sb_out, w_out_bf, tp)
    a = a.reshape(b, tp, d)
    hn2 = _post_attn(a, x, meta_tokens, attn_post_norm_w, ffn_pre_norm_w)

    act = _ffn_up(hn2.reshape(m, d), w_up_bf, conv_w[0], conv_b, tp)
    f = _ffn_down(act, w_down_bf, tp)
    return _final(f.reshape(b, seq, d), a, x, attn_post_norm_w, ffn_post_norm_w)
```

```python
import functools
import math

import numpy as np
import jax
import jax.numpy as jnp
from jax import lax
from jax.experimental import pallas as pl
from jax.experimental.pallas import tpu as pltpu

N_META = 16
CHUNK = 128
N_PAD = CHUNK - N_META
RET_HEADS = 8
RET_DK = 256
RET_DV = 256
SB_HEADS = 16
SB_DH = 128
RET_WIDTH = RET_HEADS * RET_DV
RET_QK_WIDTH = RET_HEADS * RET_DK
SB_WIDTH = SB_HEADS * SB_DH
CONV_W = 3
EPS = 1e-6
ROPE_BASE = 10000.0

BF16_SUBLANES = 16
HALO = BF16_SUBLANES
assert CONV_W - 1 <= HALO <= N_META
VMEM_LIMIT_BYTES = 58 * 1024 * 1024

PROJ_BLOCK_ROWS = 1056
PROJ_BLOCK_COLS = 1024
FFN_UP_BLOCK_COLS = 256
FFN_DOWN_BLOCK_ROWS = 528
FFN_DOWN_BLOCK_COLS = 512

F32 = jnp.float32
BF16 = jnp.bfloat16

_NT = (((1,), (1,)), ((), ()))
_TN = (((0,), (0,)), ((), ()))


def _params(*semantics):
    return pltpu.CompilerParams(dimension_semantics=semantics,
                                vmem_limit_bytes=VMEM_LIMIT_BYTES)


def _largest_divisor(n, target, multiple):
    best = None
    for d in range(multiple, min(n, target) + 1, multiple):
        if n % d == 0:
            best = d
    assert best is not None, (n, target, multiple)
    return best


def _rms(v, w):
    ms = jnp.mean(v * v, axis=-1, keepdims=True)
    return v * lax.rsqrt(ms + EPS) * w


ROW_BATCH = 2


def _padded_rows(x_ref, meta_ref, r):
    d = x_ref.shape[-1]
    first = jnp.concatenate([jnp.zeros((N_PAD, d), F32), meta_ref[...]], axis=0)
    return jnp.where(r == 0, first[None], x_ref[...])


def _row_spec(d, shift=0):
    return pl.BlockSpec((ROW_BATCH, CHUNK, d),
                        lambda i, r: (i, jnp.maximum(r + shift, 0), 0))


def _prep_kernel(x_ref, meta_ref, w_ref, hn_ref):
    h = _padded_rows(x_ref, meta_ref, pl.program_id(1))
    hn_ref[...] = _rms(h, w_ref[...]).astype(BF16)


def _prep(x, meta, w):
    b, seq, d = x.shape
    nblk = seq // CHUNK + 1
    tp = nblk * CHUNK
    return pl.pallas_call(
        _prep_kernel,
        grid=(b // ROW_BATCH, nblk),
        in_specs=[
            _row_spec(d, -1),
            pl.BlockSpec((N_META, d), lambda i, r: (0, 0)),
            pl.BlockSpec((1, d), lambda i, r: (0, 0)),
        ],
        out_specs=_row_spec(d),
        out_shape=jax.ShapeDtypeStruct((b, tp, d), BF16),
        compiler_params=_params("parallel", "arbitrary"),
        name="prep",
    )(x, meta, w)


CAST_BLOCK_BYTES = 3 * 1024 * 1024


def _cast_plan(rows, cols, steps):
    best = None
    for br in range(BF16_SUBLANES, rows + 1, BF16_SUBLANES):
        if rows % br:
            continue
        for bc in range(128, cols + 1, 128):
            if cols % bc or br * bc * 4 > CAST_BLOCK_BYTES:
                continue
            if (rows // br) * (cols // bc) <= steps and (best is None or br * bc < best[0] * best[1]):
                best = (br, bc)
    return best


def _with_casts(kernel_fn, n_in, n_out, weights, steps, nj):
    plans, in_specs, out_specs, out_shapes, operands = [], [], [], [], []
    start = 0
    for w in weights:
        _, rows, cols = w.shape
        plan = _cast_plan(rows, cols, steps - start)
        plans.append(plan)
        if plan is None:
            continue
        br, bc = plan
        ncb = cols // bc
        nb = (rows // br) * ncb

        def block(i, j, start=start, nb=nb, ncb=ncb):
            t = jnp.clip(i * nj + j - start, 0, nb - 1)
            return t // ncb, t % ncb

        in_specs.append(pl.BlockSpec((None, br, bc), lambda i, j, block=block: (0,) + block(i, j)))
        out_specs.append(pl.BlockSpec((br, bc), block))
        out_shapes.append(jax.ShapeDtypeStruct((rows, cols), BF16))
        operands.append(w)
        start += nb
    n_cast = len(operands)

    def wrapped(*refs):
        ins, rest = refs[:n_in], refs[n_in:]
        cast_in, rest = rest[:n_cast], rest[n_cast:]
        outs, rest = rest[:n_out], rest[n_out:]
        cast_out, scratch = rest[:n_cast], rest[n_cast:]
        kernel_fn(*ins, *outs, *scratch)
        for ci, co in zip(cast_in, cast_out):
            co[...] = ci[...].astype(BF16)

    def finish(cast_results):
        it = iter(cast_results)
        return [w[0].astype(BF16) if p is None else next(it)
                for w, p in zip(weights, plans)]

    return wrapped, in_specs, out_specs, out_shapes, operands, finish


def _proj_rot_kernel(x_ref, w_ref, cos_ref, sin_ref, o_ref, *, n_q_blocks):
    j = pl.program_id(1)
    acc = jnp.dot(x_ref[...], w_ref[...], preferred_element_type=F32)
    scale = jnp.where(j < n_q_blocks, RET_DK ** -0.5, 1.0).astype(F32)
    cos = cos_ref[...]
    sin = sin_ref[...]
    half = RET_DK // 2
    for h in range(o_ref.shape[1] // RET_DK):
        lo = h * RET_DK
        x1 = acc[:, lo:lo + half]
        x2 = acc[:, lo + half:lo + RET_DK]
        o_ref[:, lo:lo + half] = ((x1 * cos - x2 * sin) * scale).astype(BF16)
        o_ref[:, lo + half:lo + RET_DK] = ((x1 * sin + x2 * cos) * scale).astype(BF16)


def _proj_plain_kernel(x_ref, w_ref, o_ref):
    o_ref[...] = jnp.dot(x_ref[...], w_ref[...],
                         preferred_element_type=F32).astype(o_ref.dtype)


def _in_proj(hn, w_in, cos, sin, tp, later_weights_rot, later_weights_rest):
    m, d = hn.shape
    bm = _largest_divisor(tp, PROJ_BLOCK_ROWS, BF16_SUBLANES)
    bn = PROJ_BLOCK_COLS
    pos_blocks = tp // bm
    n_rot = 2 * RET_QK_WIDTH
    n_rest = w_in.shape[1] - n_rot

    grid = (m // bm, n_rot // bn)
    body, c_in, c_out, c_shape, c_ops, finish_rot = _with_casts(
        functools.partial(_proj_rot_kernel, n_q_blocks=RET_QK_WIDTH // bn),
        4, 1, later_weights_rot, grid[0] * grid[1], grid[1])
    qk, *cast_rot = pl.pallas_call(
        body,
        grid=grid,
        in_specs=[
            pl.BlockSpec((bm, d), lambda i, j: (i, 0)),
            pl.BlockSpec((d, bn), lambda i, j: (0, j)),
            pl.BlockSpec((bm, RET_DK // 2), lambda i, j: (i % pos_blocks, 0)),
            pl.BlockSpec((bm, RET_DK // 2), lambda i, j: (i % pos_blocks, 0)),
        ] + c_in,
        out_specs=[pl.BlockSpec((bm, bn), lambda i, j: (i, j))] + c_out,
        out_shape=[jax.ShapeDtypeStruct((m, n_rot), BF16)] + c_shape,
        compiler_params=_params("arbitrary", "arbitrary"),
        name="in_proj_rot",
    )(hn, w_in, cos, sin, *c_ops)

    rot_blocks = n_rot // bn
    grid = (m // bm, n_rest // bn)
    body, c_in, c_out, c_shape, c_ops, finish_rest = _with_casts(
        _proj_plain_kernel, 2, 1, later_weights_rest, grid[0] * grid[1], grid[1])
    rest, *cast_rest = pl.pallas_call(
        body,
        grid=grid,
        in_specs=[
            pl.BlockSpec((bm, d), lambda i, j: (i, 0)),
            pl.BlockSpec((d, bn), lambda i, j: (0, j + rot_blocks)),
        ] + c_in,
        out_specs=[pl.BlockSpec((bm, bn), lambda i, j: (i, j))] + c_out,
        out_shape=[jax.ShapeDtypeStruct((m, n_rest), BF16)] + c_shape,
        compiler_params=_params("arbitrary", "arbitrary"),
        name="in_proj_rest",
    )(hn, w_in, *c_ops)
    return qk, rest, finish_rot(cast_rot), finish_rest(cast_rest)


def _ret_log_gamma():
    return np.log(1.0 - 2.0 ** (-5.0 - np.arange(RET_HEADS, dtype=np.float64)))


def _ret_kernel(q_ref, k_ref, v_ref, g_ref, intra_ref, qd_ref, kd_ref, gn_ref,
                o_ref, state_ref):
    c = pl.program_id(1)

    @pl.when(c == 0)
    def _():
        state_ref[...] = jnp.zeros(state_ref.shape, F32)

    chunk_decay = np.exp(CHUNK * _ret_log_gamma()).astype(np.float32)
    for n in range(q_ref.shape[0]):
        for h in range(RET_HEADS):
            sl = slice(h * RET_DK, (h + 1) * RET_DK)
            q = q_ref[n, :, sl]
            k = k_ref[n, :, sl]
            v = v_ref[n, :, sl]
            s = lax.dot_general(q, k, _NT, preferred_element_type=F32) * intra_ref[h]
            intra = jnp.dot(s.astype(BF16), v, preferred_element_type=F32)
            st = state_ref[n, h]
            qd = qd_ref[h]
            kd = kd_ref[h]
            cross = jnp.dot(q, st.astype(BF16), preferred_element_type=F32)
            cross = cross * jnp.concatenate([qd, qd], axis=1)
            kdk = (k.astype(F32) * jnp.concatenate([kd, kd], axis=1)).astype(BF16)
            upd = lax.dot_general(kdk, v, _TN, preferred_element_type=F32)
            state_ref[n, h] = st * float(chunk_decay[h]) + upd
            y = intra + cross
            yn = _rms(y, gn_ref[:, sl])
            g = g_ref[n, :, sl].astype(F32)
            o_ref[n, :, sl] = (g * jax.nn.sigmoid(g) * yn).astype(BF16)


def _retention(qk, rest, gn_w, b, tp, later_weights):
    m = qk.shape[0]
    nblk = tp // CHUNK
    lg = jnp.log(1.0 - 2.0 ** (-5.0 - jnp.arange(RET_HEADS, dtype=F32)))
    idx = jnp.arange(CHUNK, dtype=F32)
    diff = idx[:, None] - idx[None, :]
    intra = jnp.where(diff[None] >= 0,
                      jnp.exp(jnp.maximum(diff, 0.0)[None] * lg[:, None, None]), 0.0)
    qd = jnp.exp((idx[None, :] + 1.0) * lg[:, None])
    kd = jnp.exp((CHUNK - 1.0 - idx[None, :]) * lg[:, None])
    qd = jnp.broadcast_to(qd[:, :, None], (RET_HEADS, CHUNK, CHUNK))
    kd = jnp.broadcast_to(kd[:, :, None], (RET_HEADS, CHUNK, CHUNK))
    table = pl.BlockSpec((RET_HEADS, CHUNK, CHUNK), lambda i, c: (0, 0, 0))
    grid = (b // ROW_BATCH, nblk)
    body, c_in, c_out, c_shape, c_ops, finish = _with_casts(
        _ret_kernel, 8, 1, later_weights, grid[0] * grid[1], nblk)

    def cols(block, width):
        return pl.BlockSpec((ROW_BATCH, CHUNK, width), lambda i, c: (i, c, block))

    qk = qk.reshape(b, tp, -1)
    rest = rest.reshape(b, tp, -1)
    out, *cast = pl.pallas_call(
        body,
        grid=grid,
        in_specs=[cols(0, RET_QK_WIDTH), cols(1, RET_QK_WIDTH),
                  cols(0, RET_WIDTH), cols(1, RET_WIDTH),
                  table, table, table,
                  pl.BlockSpec((1, RET_WIDTH), lambda i, c: (0, 0))] + c_in,
        out_specs=[cols(0, RET_WIDTH)] + c_out,
        out_shape=[jax.ShapeDtypeStruct((b, tp, RET_WIDTH), BF16)] + c_shape,
        scratch_shapes=[pltpu.VMEM((ROW_BATCH, RET_HEADS, RET_DK, RET_DV), F32)],
        compiler_params=_params("arbitrary", "arbitrary"),
        name="retention",
    )(qk, qk, rest, rest, intra, qd, kd, gn_w, *c_ops)
    return out.reshape(m, RET_WIDTH), finish(cast)


SB_GROUP = 8
SB_UNDERFLOW = 110.0
SB_FIRST_TILES = 3
SB_Q_GROUP = 3
LOG2E = 1.4426950408889634


def _sb_kernel(q_ref, k_ref, v_ref, tt_ref, nw_ref, o_ref, *, nblk):
    scale = LOG2E / math.sqrt(SB_DH)
    row = lax.broadcasted_iota(jnp.int32, (CHUNK, CHUNK), 0)
    col = lax.broadcasted_iota(jnp.int32, (CHUNK, CHUNK), 1)
    heads = [slice(g * SB_DH, (g + 1) * SB_DH) for g in range(SB_GROUP)]

    def step(qs, tiles, cs, accs):
        vs, masks, log_beta, split = [], [], [], []
        for j, mask in tiles:
            start = pl.multiple_of(jnp.maximum(j, 0) * CHUNK, CHUNK)
            k = k_ref[pl.ds(start, CHUNK), :]
            vs.append(v_ref[pl.ds(start, CHUNK), :])
            masks.append(mask)
            for g in range(SB_GROUP):
                zs = lax.dot_general(qs[g], k[:, heads[g]], _NT,
                                     preferred_element_type=F32) * scale
                sp = jnp.maximum(zs, 0.0) + jnp.log2(1.0 + jnp.exp2(-jnp.abs(zs)))
                log_beta.append(zs - sp)
                log_keep = jnp.where(mask, -sp, 0.0)
                hi = log_keep.astype(BF16)
                lo = (log_keep - hi.astype(F32)).astype(BF16)
                split.append(jnp.concatenate([hi, lo], axis=1))
        sums = jnp.dot(jnp.concatenate(split, axis=0), tt_ref[...],
                       preferred_element_type=F32)
        cs, accs = list(cs), list(accs)
        for t in range(len(tiles)):
            for g in range(SB_GROUP):
                n = t * SB_GROUP + g
                s = sums[n * CHUNK:(n + 1) * CHUNK]
                after = s[:, :CHUNK] + cs[g]
                w = jnp.where(masks[t], jnp.exp2(log_beta[n] + after), 0.0)
                accs[g] = accs[g] + jnp.dot(w.astype(BF16), vs[t][:, heads[g]],
                                            preferred_element_type=F32)
                cs[g] = cs[g] + s[:, CHUNK:]
        return tuple(cs), tuple(accs)

    def least_decayed(cs):
        return jnp.max(functools.reduce(jnp.maximum, cs))

    def valid_keys(j):
        return j * CHUNK + col >= N_PAD

    def load_q(i):
        start = pl.multiple_of(i * CHUNK, CHUNK)
        q = q_ref[pl.ds(start, CHUNK), :]
        return [q[:, heads[g]] for g in range(SB_GROUP)]

    def first_step(qs, i):
        zero = (jnp.zeros((CHUNK, CHUNK), F32),) * SB_GROUP
        first = [(i, jnp.logical_and(col < row, valid_keys(i)))]
        first += [(i - t, valid_keys(i - t)) for t in range(1, SB_FIRST_TILES)]
        return step(qs, first, zero, zero)

    def finish(qs, i, cs, accs):
        def cond(st):
            return jnp.logical_and(st[0] >= 0, st[1] > -SB_UNDERFLOW * LOG2E)

        def body(st):
            j = st[0]
            cs_, accs_ = step(qs, [(j, valid_keys(j))], st[2:2 + SB_GROUP],
                              st[2 + SB_GROUP:])
            return (j - 1, least_decayed(cs_)) + cs_ + accs_

        st = lax.while_loop(cond, body,
                            (i - SB_FIRST_TILES, least_decayed(cs)) + cs + accs)
        accs = st[2 + SB_GROUP:]
        start = pl.multiple_of(i * CHUNK, CHUNK)
        for g in range(SB_GROUP):
            o_ref[pl.ds(start, CHUNK), heads[g]] = _rms(
                accs[g], nw_ref[:, heads[g]]).astype(BF16)

    def q_block(i, carry):
        qs = load_q(i)
        finish(qs, i, *first_step(qs, i))
        return carry

    def q_block_group(p, carry):
        blocks = [SB_Q_GROUP * p + n for n in range(SB_Q_GROUP)]
        qs = [load_q(i) for i in blocks]
        first = [first_step(q, i) for q, i in zip(qs, blocks)]
        for q, i, f in zip(qs, blocks, first):
            finish(q, i, *f)
        return carry

    lax.fori_loop(0, nblk // SB_Q_GROUP, q_block_group, 0)
    for i in range(nblk - nblk % SB_Q_GROUP, nblk):
        q_block(i, 0)


def _stick_breaking(rest, nw, b, tp):
    m = rest.shape[0]
    nblk = tp // CHUNK
    width = SB_GROUP * SB_DH
    groups = SB_HEADS // SB_GROUP
    base = 2 * RET_WIDTH // width
    idx = jnp.arange(CHUNK)
    tri = (idx[:, None] > idx[None, :]).astype(BF16)
    tt = jnp.concatenate([tri, jnp.ones((CHUNK, CHUNK), BF16)], axis=1)
    tt = jnp.concatenate([tt, tt], axis=0)
    return pl.pallas_call(
        functools.partial(_sb_kernel, nblk=nblk),
        grid=(b, groups),
        in_specs=[
            pl.BlockSpec((tp, width), lambda i, h: (i, base + h),
                         pipeline_mode=pl.Buffered(1)),
            pl.BlockSpec((tp, width), lambda i, h: (i, base + groups + h),
                         pipeline_mode=pl.Buffered(1)),
            pl.BlockSpec((tp, width), lambda i, h: (i, base + 2 * groups + h),
                         pipeline_mode=pl.Buffered(1)),
            pl.BlockSpec((2 * CHUNK, 2 * CHUNK), lambda i, h: (0, 0)),
            pl.BlockSpec((1, width), lambda i, h: (0, h)),
        ],
        out_specs=pl.BlockSpec((tp, width), lambda i, h: (i, h)),
        out_shape=jax.ShapeDtypeStruct((m, SB_WIDTH), BF16),
        compiler_params=_params("parallel", "parallel"),
        name="stick_breaking",
    )(rest, rest, rest, tt, nw)


def _out_proj_kernel(r_ref, s_ref, w1_ref, w2_ref, o_ref):
    o_ref[...] = (jnp.dot(r_ref[...], w1_ref[...], preferred_element_type=F32)
                  + jnp.dot(s_ref[...], w2_ref[...], preferred_element_type=F32)
                  ).astype(o_ref.dtype)


def _out_proj(ret_out, sb_out, w_out, tp):
    m = ret_out.shape[0]
    d = w_out.shape[1]
    bm = _largest_divisor(tp, PROJ_BLOCK_ROWS, BF16_SUBLANES)
    bn = PROJ_BLOCK_COLS
    return pl.pallas_call(
        _out_proj_kernel,
        grid=(m // bm, d // bn),
        in_specs=[
            pl.BlockSpec((bm, RET_WIDTH), lambda i, j: (i, 0)),
            pl.BlockSpec((bm, SB_WIDTH), lambda i, j: (i, 0)),
            pl.BlockSpec((RET_WIDTH, bn), lambda i, j: (0, j)),
            pl.BlockSpec((SB_WIDTH, bn), lambda i, j: (1, j)),
        ],
        out_specs=pl.BlockSpec((bm, bn), lambda i, j: (i, j)),
        out_shape=jax.ShapeDtypeStruct((m, d), BF16),
        compiler_params=_params("parallel", "arbitrary"),
        name="out_proj",
    )(ret_out, sb_out, w_out, w_out)


def _post_attn_kernel(a_ref, x_ref, meta_ref, w1_ref, w2_ref, hn_ref):
    h = _padded_rows(x_ref, meta_ref, pl.program_id(1))
    h1 = h + _rms(a_ref[...].astype(F32), w1_ref[...])
    hn_ref[...] = _rms(h1, w2_ref[...]).astype(BF16)


def _post_attn(a, x, meta, w_post, w_pre):
    b, tp, d = a.shape
    vec = pl.BlockSpec((1, d), lambda i, r: (0, 0))
    return pl.pallas_call(
        _post_attn_kernel,
        grid=(b // ROW_BATCH, tp // CHUNK),
        in_specs=[_row_spec(d), _row_spec(d, -1),
                  pl.BlockSpec((N_META, d), lambda i, r: (0, 0)), vec, vec],
        out_specs=_row_spec(d),
        out_shape=jax.ShapeDtypeStruct((b, tp, d), BF16),
        compiler_params=_params("parallel", "arbitrary"),
        name="post_attn",
    )(a, x, meta, w_post, w_pre)


FFN_SUB_ROWS = 352


def _ffn_up_kernel(x_ref, xh_ref, wg_ref, wu_ref, cw_ref, cb_ref, o_ref, g_ref):
    bm = x_ref.shape[0]
    sub = FFN_SUB_ROWS if bm % FFN_SUB_ROWS == 0 else bm
    wg = wg_ref[...]
    wu = wu_ref[...]
    g_ref[:HALO, :] = jnp.dot(xh_ref[...], wg, preferred_element_type=F32)
    for r in range(bm // sub):
        x = x_ref[r * sub:(r + 1) * sub, :]
        lo = HALO + r * sub
        g_ref[lo:lo + sub, :] = jnp.dot(x, wg, preferred_element_type=F32)
        up = jnp.dot(x, wu, preferred_element_type=F32)
        ext = g_ref[lo - 8:lo + sub, :]
        prev1 = pltpu.roll(ext, 1, axis=0)
        prev2 = pltpu.roll(prev1, 1, axis=0)
        conv = cb_ref[...] + prev2[8:] * cw_ref[0:1, :]
        conv = conv + prev1[8:] * cw_ref[1:2, :]
        conv = conv + ext[8:] * cw_ref[2:3, :]
        o_ref[r * sub:(r + 1) * sub, :] = (conv * jax.nn.sigmoid(conv) * up).astype(BF16)


def _ffn_up(hn, w_up, conv_w, conv_b, tp):
    m, d = hn.shape
    dff = w_up.shape[1] // 2
    bm = tp
    bn = FFN_UP_BLOCK_COLS
    halo_blocks = bm // HALO
    up_base = dff // bn
    return pl.pallas_call(
        _ffn_up_kernel,
        grid=(m // bm, dff // bn),
        in_specs=[
            pl.BlockSpec((bm, d), lambda i, j: (i, 0), pipeline_mode=pl.Buffered(1)),
            pl.BlockSpec((HALO, d), lambda i, j: (jnp.maximum(i * halo_blocks - 1, 0), 0)),
            pl.BlockSpec((d, bn), lambda i, j: (0, j)),
            pl.BlockSpec((d, bn), lambda i, j: (0, j + up_base)),
            pl.BlockSpec((CONV_W, bn), lambda i, j: (0, j)),
            pl.BlockSpec((1, bn), lambda i, j: (0, j)),
        ],
        out_specs=pl.BlockSpec((bm, bn), lambda i, j: (i, j)),
        out_shape=jax.ShapeDtypeStruct((m, dff), BF16),
        scratch_shapes=[pltpu.VMEM((bm + HALO, bn), F32)],
        compiler_params=_params("parallel", "arbitrary"),
        name="ffn_up",
    )(hn, hn, w_up, w_up, conv_w, conv_b)


def _ffn_down(act, w_down, tp):
    m, dff = act.shape
    d = w_down.shape[1]
    seq = tp - CHUNK
    bm = _largest_divisor(seq, FFN_DOWN_BLOCK_ROWS, BF16_SUBLANES)
    bn = FFN_DOWN_BLOCK_COLS
    per_batch = seq // bm

    def token_rows(i, j):
        start = (i // per_batch) * tp + CHUNK + (i % per_batch) * bm
        return pl.multiple_of(start, BF16_SUBLANES), 0

    return pl.pallas_call(
        _proj_plain_kernel,
        grid=(m // tp * per_batch, d // bn),
        in_specs=[
            pl.BlockSpec((pl.Element(bm), pl.Element(dff)), token_rows),
            pl.BlockSpec((dff, bn), lambda i, j: (0, j)),
        ],
        out_specs=pl.BlockSpec((bm, bn), lambda i, j: (i, j)),
        out_shape=jax.ShapeDtypeStruct((m // tp * seq, d), BF16),
        compiler_params=_params("parallel", "arbitrary"),
        name="ffn_down",
    )(act, w_down)


def _final_kernel(f_ref, a_ref, x_ref, w1_ref, w2_ref, o_ref):
    h1 = x_ref[...] + _rms(a_ref[...].astype(F32), w1_ref[...])
    o_ref[...] = h1 + _rms(f_ref[...].astype(F32), w2_ref[...])


def _final(f, a, x, w_post, w_final):
    b, seq, d = x.shape
    vec = pl.BlockSpec((1, d), lambda i, r: (0, 0))
    return pl.pallas_call(
        _final_kernel,
        grid=(b // ROW_BATCH, seq // CHUNK),
        in_specs=[_row_spec(d), _row_spec(d, 1), _row_spec(d), vec, vec],
        out_specs=_row_spec(d),
        out_shape=jax.ShapeDtypeStruct((b, seq, d), F32),
        compiler_params=_params("parallel", "parallel"),
        name="final",
    )(f, a, x, w_post, w_final)


def _rotary_tables(tp):
    half = RET_DK // 2
    pos = (jnp.arange(tp) - N_PAD).astype(F32)
    inv = ROPE_BASE ** (-jnp.arange(half, dtype=F32) / half)
    ang = pos[:, None] * inv[None, :]
    return jnp.cos(ang), jnp.sin(ang)


def kernel(x, meta_tokens, attn_pre_norm_w, w_in, ret_gn_w, sb_norm_w, w_out,
           attn_post_norm_w, ffn_pre_norm_w, w_up, conv_w, conv_b, w_down, ffn_post_norm_w):
    b, seq, d = x.shape
    assert seq % CHUNK == 0 and w_in.shape[0] == 1 and b % ROW_BATCH == 0
    tp = seq + CHUNK
    m = b * tp

    hn = _prep(x, meta_tokens, attn_pre_norm_w).reshape(m, d)

    cos, sin = _rotary_tables(tp)
    qk, rest, (w_down_bf,), (w_up_bf,) = _in_proj(
        hn, w_in[0].astype(BF16), cos, sin, tp, [w_down], [w_up])
    ret_out, (w_out_bf,) = _retention(qk, rest, ret_gn_w, b, tp, [w_out])
    sb_out = _stick_breaking(rest, sb_norm_w, b, tp)
    a = _out_proj(ret_out, sb_out, w_out_bf, tp)
    a = a.reshape(b, tp, d)
    hn2 = _post_attn(a, x, meta_tokens, attn_post_norm_w, ffn_pre_norm_w)

    act = _ffn_up(hn2.reshape(m, d), w_up_bf, conv_w[0], conv_b, tp)
    f = _ffn_down(act, w_down_bf, tp)
    return _final(f.reshape(b, seq, d), a, x, attn_post_norm_w, ffn_post_norm_w)
```
